```python
import jax, jax.numpy as jnp
from jax import lax
import numpy as np

D_MODEL = 1024
BATCH = 8
SEQ = 4096
DEPTH = 2

N_META = 16
EPS = 1e-6
D_FF = 4 * D_MODEL
A_HEADS = 4
A_DQK = D_MODEL // 8
A_DV = D_MODEL // 4
A_CHUNK = 64
B_WIDTH = D_MODEL
B_BLOCKS = 8
B_BLOCK = B_WIDTH // B_BLOCKS
CONV_W = 4
LRU_C = 8.0
C_HEADS = 8
C_Q_LORA = 3 * D_MODEL // 8
C_KV_LORA = D_MODEL // 4
C_NOPE = 64
C_ROPE = 32
C_V = 64
ROPE_THETA = 10000.0
D_HEADS = 8
D_HD = 64
ATT_BLOCK = 128

N_EVEN = (DEPTH + 1) // 2
N_ODD = DEPTH // 2
EVEN_SIZES = (A_HEADS * A_DQK, A_HEADS * A_DQK, A_HEADS * A_DV, A_HEADS * A_DV, A_HEADS, A_HEADS, B_WIDTH, B_WIDTH)
ODD_SIZES = (C_Q_LORA, C_KV_LORA, C_ROPE, D_HEADS * D_HD, D_HEADS * D_HD, D_HEADS * D_HD, D_HEADS)
P_EVEN = sum(EVEN_SIZES)
P_ODD = sum(ODD_SIZES)

kernel_name = "hybrid_mlstm_rglru_mla_fox_trunk"


def rmsnorm(x, g):
    xf = x.astype(jnp.float32)
    y = xf * lax.rsqrt(jnp.mean(xf * xf, axis=-1, keepdims=True) + EPS) * g.astype(jnp.float32)
    return y.astype(x.dtype)


def split_cols(z, sizes):
    out, start = [], 0
    for s in sizes:
        out.append(z[..., start:start + s])
        start += s
    return out


def pad_time(a, n, value=0.0):
    widths = [(0, 0), (0, 0), (n, 0)] + [(0, 0)] * (a.ndim - 3)
    return jnp.pad(a, widths, constant_values=value)


def to_heads(a, n_heads, d):
    b, t, _ = a.shape
    return a.reshape(b, t, n_heads, d).transpose(0, 2, 1, 3)


def from_heads(a):
    b, h, t, d = a.shape
    return a.transpose(0, 2, 1, 3).reshape(b, t, h * d)


def rope(x, pos):
    half = x.shape[-1] // 2
    freqs = ROPE_THETA ** (-jnp.arange(half, dtype=jnp.float32) / half)
    ang = pos[:, None, :, None].astype(jnp.float32) * freqs
    cos, sin = jnp.cos(ang), jnp.sin(ang)
    xf = x.astype(jnp.float32)
    x1, x2 = xf[..., :half], xf[..., half:]
    return jnp.concatenate([x1 * cos - x2 * sin, x1 * sin + x2 * cos], axis=-1).astype(x.dtype)


def mlstm_chunkwise(q, k, v, logi, logf):
    bn, nh, tp, dqk = q.shape
    dv = v.shape[-1]
    nc = tp // A_CHUNK

    def chunks(a):
        a = a.astype(jnp.float32).reshape(bn, nh, nc, A_CHUNK, *a.shape[3:])
        return jnp.moveaxis(a, 2, 0)

    causal = jnp.tril(jnp.ones((A_CHUNK, A_CHUNK), dtype=bool))

    def step(carry, inp):
        c_state, n_state, m_state = carry
        qj, kj, vj, ij, fj = inp
        b = jnp.cumsum(fj, axis=-1)
        dmat = jnp.where(causal, b[..., :, None] - b[..., None, :] + ij[..., None, :], -jnp.inf)
        inter = b + m_state[..., None]
        m_row = jnp.maximum(inter, jnp.max(dmat, axis=-1))
        w_intra = jnp.exp(dmat - m_row[..., None])
        w_inter = jnp.exp(inter - m_row)
        s = jnp.einsum('bhtd,bhsd->bhts', qj, kj) * w_intra
        num = (w_inter[..., None] * jnp.einsum('bhtd,bhde->bhte', qj, c_state)
               + jnp.einsum('bhts,bhse->bhte', s, vj))
        den = w_inter * jnp.einsum('bhtd,bhd->bht', qj, n_state) + jnp.sum(s, axis=-1)
        h = num / jnp.maximum(jnp.abs(den), jnp.exp(-m_row))[..., None]
        g = b[..., -1]
        dk = g[..., None] - b + ij
        m_new = jnp.maximum(g + m_state, jnp.max(dk, axis=-1))
        wk = jnp.exp(dk - m_new[..., None])
        sc = jnp.exp(g + m_state - m_new)
        c_new = sc[..., None, None] * c_state + jnp.einsum('bhs,bhsd,bhse->bhde', wk, kj, vj)
        n_new = sc[..., None] * n_state + jnp.einsum('bhs,bhsd->bhd', wk, kj)
        return (c_new, n_new, m_new), h

    init = (jnp.zeros((bn, nh, dqk, dv), jnp.float32),
            jnp.zeros((bn, nh, dqk), jnp.float32),
            jnp.zeros((bn, nh), jnp.float32))
    _, hs = lax.scan(step, init, (chunks(q), chunks(k), chunks(v), chunks(logi), chunks(logf)))
    return jnp.moveaxis(hs, 0, 2).reshape(bn, nh, tp, dv)


def causal_depthwise_conv(x, w, b):
    k, c = w.shape
    y = lax.conv_general_dilated(x, w[:, None, :], window_strides=(1,), padding=[(k - 1, 0)],
                                 dimension_numbers=('NWC', 'WIO', 'NWC'), feature_group_count=c)
    return y + b


def rglru(x, w_a, b_a, w_x, b_x, lam):
    bn, t, w = x.shape
    xb = x.reshape(bn, t, B_BLOCKS, B_BLOCK)
    r = jax.nn.sigmoid(jnp.einsum('btgi,gij->btgj', xb, w_a).reshape(bn, t, w) + b_a)
    i = jax.nn.sigmoid(jnp.einsum('btgi,gij->btgj', xb, w_x).reshape(bn, t, w) + b_x)
    log_a = -LRU_C * r.astype(jnp.float32) * jax.nn.softplus(-lam.astype(jnp.float32))
    a = jnp.exp(log_a)
    u = jnp.sqrt(-jnp.expm1(2.0 * log_a)) * (i * x).astype(jnp.float32)

    def combine(left, right):
        a1, u1 = left
        a2, u2 = right
        return a1 * a2, a2 * u1 + u2

    _, h = lax.associative_scan(combine, (a, u), axis=1)
    return h.astype(x.dtype)


def blocked_causal_attention(q, k, v, scale, n_pad, fcum=None):
    bn, nh, tp, _ = q.shape
    nb = tp // ATT_BLOCK
    kpos = jnp.arange(tp)
    kvalid = kpos >= n_pad

    def one_block(i):
        start = i * ATT_BLOCK
        qb = lax.dynamic_slice_in_dim(q, start, ATT_BLOCK, axis=2)
        qpos = start + jnp.arange(ATT_BLOCK)
        logits = jnp.einsum('bhqd,bhkd->bhqk', qb, k, preferred_element_type=jnp.float32) * scale
        if fcum is not None:
            fq = lax.dynamic_slice_in_dim(fcum, start, ATT_BLOCK, axis=2)
            logits = logits + (fq[..., :, None] - fcum[..., None, :])
        mask = (kpos[None, :] <= qpos[:, None]) & (kvalid[None, :] | (kpos[None, :] == qpos[:, None]))
        p = jax.nn.softmax(jnp.where(mask, logits, -jnp.inf), axis=-1)
        return jnp.einsum('bhqk,bhkd->bhqd', p.astype(v.dtype), v)

    out = lax.map(one_block, jnp.arange(nb))
    return jnp.moveaxis(out, 0, 2).reshape(bn, nh, tp, v.shape[-1])


def even_mixer(h, ln, w_in, b_if, a_norm_g, conv_w, conv_b, w_ra, b_ra, w_rx, b_rx, lam, w_out):
    bn, t, _ = h.shape
    z = rmsnorm(h, ln) @ w_in
    q, k, v, o, ig, fg, xb, gb = split_cols(z, EVEN_SIZES)
    q = to_heads(q, A_HEADS, A_DQK)
    k = to_heads(k, A_HEADS, A_DQK) * (A_DQK ** -0.5)
    v = to_heads(v, A_HEADS, A_DV)
    gates = (jnp.concatenate([ig, fg], axis=-1) + b_if).astype(jnp.float32)
    logi = gates[..., :A_HEADS].transpose(0, 2, 1)
    logf = jax.nn.log_sigmoid(gates[..., A_HEADS:]).transpose(0, 2, 1)
    pad = A_CHUNK - N_META
    ha = mlstm_chunkwise(pad_time(q, pad), pad_time(k, pad), pad_time(v, pad),
                         pad_time(logi, pad, -jnp.inf), pad_time(logf, pad))[:, :, pad:]
    ha = rmsnorm(ha.astype(h.dtype), a_norm_g) * jax.nn.sigmoid(to_heads(o, A_HEADS, A_DV))
    xc = causal_depthwise_conv(xb, conv_w, conv_b)
    hb = rglru(xc, w_ra, b_ra, w_rx, b_rx, lam) * jax.nn.gelu(gb)
    return jnp.concatenate([from_heads(ha), hb], axis=-1) @ w_out


def odd_mixer(h, pos_full, ln, w_in, b_f, g_qa, g_kva, w_uq, w_ukv, g_qn, g_qr, g_kn, g_kr,
              g_fq, g_fk, w_out):
    bn, t, _ = h.shape
    z = rmsnorm(h, ln) @ w_in
    cq, ckv, kr, fq, fk, fv, ff = split_cols(z, ODD_SIZES)
    pad = ATT_BLOCK - N_META
    q = to_heads(rmsnorm(cq, g_qa) @ w_uq, C_HEADS, C_NOPE + C_ROPE)
    kv = to_heads(rmsnorm(ckv, g_kva) @ w_ukv, C_HEADS, C_NOPE + C_V)
    q_nope = rmsnorm(q[..., :C_NOPE], g_qn)
    q_rope = rope(rmsnorm(q[..., C_NOPE:], g_qr), pos_full)
    k_nope = rmsnorm(kv[..., :C_NOPE], g_kn)
    k_rope = rope(rmsnorm(kr, g_kr)[:, None], pos_full)
    q_mla = jnp.concatenate([q_nope, q_rope], axis=-1)
    k_mla = jnp.concatenate([k_nope, jnp.broadcast_to(k_rope, (bn, C_HEADS, t, C_ROPE))], axis=-1)
    v_mla = kv[..., C_NOPE:]
    hc = blocked_causal_attention(pad_time(q_mla, pad), pad_time(k_mla, pad), pad_time(v_mla, pad),
                                  (C_NOPE + C_ROPE) ** -0.5, pad)[:, :, pad:]
    qf = rmsnorm(to_heads(fq, D_HEADS, D_HD), g_fq)
    kf = rmsnorm(to_heads(fk, D_HEADS, D_HD), g_fk)
    vf = to_heads(fv, D_HEADS, D_HD)
    logf = jax.nn.log_sigmoid((ff + b_f).astype(jnp.float32)).transpose(0, 2, 1)
    fcum = jnp.cumsum(pad_time(logf, pad), axis=-1)
    hd = blocked_causal_attention(pad_time(qf, pad), pad_time(kf, pad), pad_time(vf, pad),
                                  D_HD ** -0.5, pad, fcum)[:, :, pad:]
    return jnp.concatenate([from_heads(hc), from_heads(hd)], axis=-1) @ w_out


def sqrelu_mlp(h, ln, w1, w2):
    return jnp.square(jax.nn.relu(rmsnorm(h, ln) @ w1)) @ w2


def setup_inputs(seed: int = 0) -> dict:
    key = jax.random.key(seed)
    ks = iter(jax.random.split(key, 48))
    nrm = lambda shape, s: jax.random.normal(next(ks), shape, jnp.float32) * s
    gain = lambda shape: 1.0 + nrm(shape, 0.02)
    ne, no = N_EVEN, N_ODD
    x = jax.random.normal(next(ks), (BATCH, SEQ, D_MODEL), jnp.float32)
    positions = (jnp.arange(SEQ, dtype=jnp.int32)[None, :]
                 + jax.random.randint(next(ks), (BATCH, 1), 0, 1024, dtype=jnp.int32))
    meta = nrm((N_META, D_MODEL), 1.0)
    ev_ln = gain((ne, D_MODEL))
    ev_w_in = nrm((ne, D_MODEL, P_EVEN), D_MODEL ** -0.5)
    ev_b_if = jnp.concatenate([nrm((ne, A_HEADS), 0.1),
                               3.0 + 3.0 * jax.random.uniform(next(ks), (ne, A_HEADS))], axis=-1)
    ev_a_norm = gain((ne, A_DV))
    ev_conv_w = nrm((ne, CONV_W, B_WIDTH), CONV_W ** -0.5)
    ev_conv_b = nrm((ne, B_WIDTH), 0.01)
    ev_w_ra = nrm((ne, B_BLOCKS, B_BLOCK, B_BLOCK), B_BLOCK ** -0.5)
    ev_b_ra = nrm((ne, B_WIDTH), 0.01)
    ev_w_rx = nrm((ne, B_BLOCKS, B_BLOCK, B_BLOCK), B_BLOCK ** -0.5)
    ev_b_rx = nrm((ne, B_WIDTH), 0.01)
    a0 = jax.random.uniform(next(ks), (ne, B_WIDTH), jnp.float32, 0.9, 0.999)
    s0 = a0 ** (1.0 / LRU_C)
    ev_lam = jnp.log(s0) - jnp.log1p(-s0)
    ev_w_out = nrm((ne, A_HEADS * A_DV + B_WIDTH, D_MODEL), (A_HEADS * A_DV + B_WIDTH) ** -0.5)
    od_ln = gain((no, D_MODEL))
    od_w_in = nrm((no, D_MODEL, P_ODD), D_MODEL ** -0.5)
    od_b_f = 2.0 + nrm((no, D_HEADS), 0.5)
    od_g_qa = gain((no, C_Q_LORA))
    od_g_kva = gain((no, C_KV_LORA))
    od_w_uq = nrm((no, C_Q_LORA, C_HEADS * (C_NOPE + C_ROPE)), C_Q_LORA ** -0.5)
    od_w_ukv = nrm((no, C_KV_LORA, C_HEADS * (C_NOPE + C_V)), C_KV_LORA ** -0.5)
    od_g_qn = gain((no, C_NOPE))
    od_g_qr = gain((no, C_ROPE))
    od_g_kn = gain((no, C_NOPE))
    od_g_kr = gain((no, C_ROPE))
    od_g_fq = gain((no, D_HD))
    od_g_fk = gain((no, D_HD))
    od_w_out = nrm((no, C_HEADS * C_V + D_HEADS * D_HD, D_MODEL), (C_HEADS * C_V + D_HEADS * D_HD) ** -0.5)
    mlp_ln = gain((DEPTH, D_MODEL))
    w_ff1 = nrm((DEPTH, D_MODEL, D_FF), D_MODEL ** -0.5)
    w_ff2 = nrm((DEPTH, D_FF, D_MODEL), D_FF ** -0.5)
    return {"x": x, "positions": positions, "meta": meta,
            "ev_ln": ev_ln, "ev_w_in": ev_w_in, "ev_b_if": ev_b_if, "ev_a_norm": ev_a_norm,
            "ev_conv_w": ev_conv_w, "ev_conv_b": ev_conv_b, "ev_w_ra": ev_w_ra, "ev_b_ra": ev_b_ra,
            "ev_w_rx": ev_w_rx, "ev_b_rx": ev_b_rx, "ev_lam": ev_lam, "ev_w_out": ev_w_out,
            "od_ln": od_ln, "od_w_in": od_w_in, "od_b_f": od_b_f, "od_g_qa": od_g_qa,
            "od_g_kva": od_g_kva, "od_w_uq": od_w_uq, "od_w_ukv": od_w_ukv, "od_g_qn": od_g_qn,
            "od_g_qr": od_g_qr, "od_g_kn": od_g_kn, "od_g_kr": od_g_kr, "od_g_fq": od_g_fq,
            "od_g_fk": od_g_fk, "od_w_out": od_w_out,
            "mlp_ln": mlp_ln, "w_ff1": w_ff1, "w_ff2": w_ff2}


def reference(x, positions, meta,
              ev_ln, ev_w_in, ev_b_if, ev_a_norm, ev_conv_w, ev_conv_b, ev_w_ra, ev_b_ra,
              ev_w_rx, ev_b_rx, ev_lam, ev_w_out,
              od_ln, od_w_in, od_b_f, od_g_qa, od_g_kva, od_w_uq, od_w_ukv, od_g_qn, od_g_qr,
              od_g_kn, od_g_kr, od_g_fq, od_g_fk, od_w_out,
              mlp_ln, w_ff1, w_ff2):
    bn = x.shape[0]
    h = jnp.concatenate([jnp.broadcast_to(meta.astype(x.dtype), (bn, N_META, D_MODEL)), x], axis=1)
    meta_pos = jnp.broadcast_to(jnp.arange(N_META, dtype=jnp.int32)[None, :], (bn, N_META))
    pos_full = jnp.concatenate([meta_pos, positions + N_META], axis=1)
    for layer in range(DEPTH):
        if layer % 2 == 0:
            e = layer // 2
            h = h + even_mixer(h, ev_ln[e], ev_w_in[e], ev_b_if[e], ev_a_norm[e], ev_conv_w[e],
                               ev_conv_b[e], ev_w_ra[e], ev_b_ra[e], ev_w_rx[e], ev_b_rx[e],
                               ev_lam[e], ev_w_out[e])
        else:
            o = layer // 2
            h = h + odd_mixer(h, pos_full, od_ln[o], od_w_in[o], od_b_f[o], od_g_qa[o], od_g_kva[o],
                              od_w_uq[o], od_w_ukv[o], od_g_qn[o], od_g_qr[o], od_g_kn[o],
                              od_g_kr[o], od_g_fq[o], od_g_fk[o], od_w_out[o])
        h = h + sqrelu_mlp(h, mlp_ln[layer], w_ff1[layer], w_ff2[layer])
    return h[:, N_META:]
```

```python
import functools
import math

import numpy as np
import jax
import jax.numpy as jnp
from jax import lax
from jax.experimental import pallas as pl
from jax.experimental.pallas import tpu as pltpu

F32 = jnp.float32
BF16 = jnp.bfloat16

N_META = 16
EPS = 1e-6
A_HEADS = 4
A_DQK = 128
A_DV = 256
B_BLOCKS = 8
B_BLOCK = 128
CONV_W = 4
LRU_C = 8.0
C_HEADS = 8
C_Q_LORA = 384
C_KV_LORA = 256
C_NOPE = 64
C_ROPE = 32
C_V = 64
ROPE_THETA = 10000.0
D_HEADS = 8
D_HD = 64

LANE = 128
SUBLANE = 8
VMEM_LIMIT = 56 * 1024 * 1024

PAD = LANE - N_META
NEG = -1e30

HALF = C_ROPE // 2
ROPE_LO = C_NOPE
ROPE_MID = C_NOPE + HALF
ROPE_HI = C_NOPE + C_ROPE


def _cparams(sem):
    return pltpu.CompilerParams(dimension_semantics=sem, vmem_limit_bytes=VMEM_LIMIT)


def _dot(a, b):
    return jnp.dot(a, b, preferred_element_type=F32)


def _dot_nt(a, b):
    return lax.dot_general(a, b, (((1,), (1,)), ((), ())), preferred_element_type=F32)


def _split2(x):
    hi = x.astype(BF16)
    lo = (x - hi.astype(F32)).astype(BF16)
    return hi, lo


def _split3(x):
    hi = x.astype(BF16)
    r = x - hi.astype(F32)
    mid = r.astype(BF16)
    lo = (r - mid.astype(F32)).astype(BF16)
    return hi, mid, lo


def _dot_sel_r(x, sel, parts):
    pieces = _split3(x) if parts == 3 else _split2(x)
    acc = _dot(pieces[0], sel)
    for p in pieces[1:]:
        acc = acc + _dot(p, sel)
    return acc


def _dot_sel_l(sel, x, parts):
    pieces = _split3(x) if parts == 3 else _split2(x)
    acc = _dot(sel, pieces[0])
    for p in pieces[1:]:
        acc = acc + _dot(sel, p)
    return acc


def _rms_rows(x, g):
    ms = jnp.mean(x * x, axis=-1, keepdims=True)
    return x * lax.rsqrt(ms + EPS) * g


def _log_sigmoid(x):
    return -(jnp.maximum(-x, 0.0) + jnp.log1p(jnp.exp(-jnp.abs(x))))


def _sigmoid(x):
    return 1.0 / (1.0 + jnp.exp(-x))


def _tril_bf16(n):
    r = lax.broadcasted_iota(jnp.int32, (n, n), 0)
    c = lax.broadcasted_iota(jnp.int32, (n, n), 1)
    return jnp.where(r >= c, 1.0, 0.0).astype(BF16)


EV_SLABS = (("q", 512, BF16), ("k", 512, BF16), ("v", 1024, BF16), ("o", 1024, BF16),
            ("g", 128, F32), ("xb", 1024, BF16), ("gb", 1024, BF16))


def _ev_in_kernel(h_ref, ln_ref, w_ref, *out_refs):
    xn = _rms_rows(h_ref[...], ln_ref[...]).astype(BF16)
    off = 0
    for (_, width, dt), ref in zip(EV_SLABS, out_refs):
        ref[...] = _dot(xn, w_ref[:, off:off + width]).astype(dt)
        off += width


def _ev_in(h2, ln, w, tm):
    r, d = h2.shape
    n = w.shape[1]
    outs = [jax.ShapeDtypeStruct((r, width), dt) for (_, width, dt) in EV_SLABS]
    return pl.pallas_call(
        _ev_in_kernel,
        out_shape=outs,
        grid=(r // tm,),
        in_specs=[pl.BlockSpec((tm, d), lambda i: (i, 0)),
                  pl.BlockSpec((1, d), lambda i: (0, 0)),
                  pl.BlockSpec((d, n), lambda i: (0, 0))],
        out_specs=[pl.BlockSpec((tm, width), lambda i: (i, 0)) for (_, width, _) in EV_SLABS],
        compiler_params=_cparams(("parallel",)),
        name="ev_in",
    )(h2, ln, w)


def _mlstm_kernel(q_ref, k_ref, v_ref, o_ref, g_ref, bif_ref, an_ref, out_ref, cn_ref, m_ref):
    c = pl.program_id(1)
    L = q_ref.shape[0]
    dv_ext = A_DV + LANE

    @pl.when(c == 0)
    def _():
        cn_ref[...] = jnp.zeros_like(cn_ref)
        m_ref[...] = jnp.zeros_like(m_ref)

    lane = lax.broadcasted_iota(jnp.int32, (L, LANE), 1)
    row = lax.broadcasted_iota(jnp.int32, (L, LANE), 0)
    valid = (row >= PAD) | (c > 0)
    g = g_ref[...] + bif_ref[...]
    li = jnp.where(valid, g, NEG)
    lf = jnp.where(valid, _log_sigmoid(g), 0.0)
    is_f = (lane >= A_HEADS) & (lane < 2 * A_HEADS)
    lf = jnp.where(is_f, lf, 0.0)
    tril = _tril_bf16(L)
    bcum = _dot_sel_l(tril, lf, 3)
    li_t = li.T
    bcum_t = bcum.T

    rr = lax.broadcasted_iota(jnp.int32, (L, L), 0)
    cc = lax.broadcasted_iota(jnp.int32, (L, L), 1)
    causal = rr >= cc
    e0 = jnp.where(lane == 0, 1.0, 0.0).astype(BF16)
    scale = A_DQK ** -0.5

    for h in range(A_HEADS):
        qs = (q_ref[:, h * A_DQK:(h + 1) * A_DQK].astype(F32) * scale).astype(BF16)
        kk = k_ref[:, h * A_DQK:(h + 1) * A_DQK]
        vext = jnp.concatenate([v_ref[:, h * A_DV:(h + 1) * A_DV], e0], axis=1)
        i_col = li[:, h:h + 1]
        b_col = bcum[:, A_HEADS + h:A_HEADS + h + 1]
        i_row = li_t[h:h + 1, :]
        b_row = bcum_t[A_HEADS + h:A_HEADS + h + 1, :]
        m_state = m_ref[:, h:h + 1]
        cn = cn_ref[h]

        dmat = jnp.where(causal, b_col - b_row + i_row, NEG)
        inter = b_col + m_state
        m_row = jnp.maximum(inter, jnp.max(dmat, axis=1, keepdims=True))
        w_intra = jnp.exp(dmat - m_row)
        w_inter = jnp.exp(inter - m_row)
        s = (_dot_nt(qs, kk) * w_intra).astype(BF16)
        numden = w_inter * _dot(qs, cn.astype(BF16)) + _dot(s, vext)
        num = numden[:, :A_DV]
        den = numden[:, A_DV:A_DV + 1]
        hh = num / jnp.maximum(jnp.abs(den), jnp.exp(-m_row))
        hn = _rms_rows(hh, an_ref[...])
        og = _sigmoid(o_ref[:, h * A_DV:(h + 1) * A_DV].astype(F32))
        out_ref[:, h * A_DV:(h + 1) * A_DV] = (hn * og).astype(out_ref.dtype)

        gsum = b_col[L - 1:L, :]
        dk_col = gsum - b_col + i_col
        dk_row = gsum - b_row + i_row
        m_new = jnp.maximum(gsum + m_state, jnp.max(dk_row, axis=1, keepdims=True))
        wk = jnp.exp(dk_col - m_new)
        sc = jnp.exp(gsum + m_state - m_new)
        kw_t = (kk.astype(F32) * wk).T.astype(BF16)
        cn_ref[h] = sc * cn + _dot(kw_t, vext)
        m_ref[:, h:h + 1] = m_new


def _mlstm(q, k, v, o, g, bif, an, bsz, tp, L):
    nq, nv = A_HEADS * A_DQK, A_HEADS * A_DV
    nc = tp // L
    row_map = lambda b, c: (b * nc + c, 0)
    const = lambda b, c: (0, 0)
    return pl.pallas_call(
        _mlstm_kernel,
        out_shape=jax.ShapeDtypeStruct((bsz * tp, nv), BF16),
        grid=(bsz, nc),
        in_specs=[pl.BlockSpec((L, nq), row_map), pl.BlockSpec((L, nq), row_map),
                  pl.BlockSpec((L, nv), row_map), pl.BlockSpec((L, nv), row_map),
                  pl.BlockSpec((L, LANE), row_map),
                  pl.BlockSpec((1, LANE), const), pl.BlockSpec((1, A_DV), const)],
        out_specs=pl.BlockSpec((L, nv), row_map),
        scratch_shapes=[pltpu.VMEM((A_HEADS, A_DQK, A_DV + LANE), F32),
                        pltpu.VMEM((1, LANE), F32)],
        compiler_params=_cparams(("parallel", "arbitrary")),
        name="mlstm",
    )(q, k, v, o, g, bif, an)


def _gelu_tanh(x):
    c0 = math.sqrt(2.0 / math.pi)
    return 0.5 * x * (1.0 + jnp.tanh(c0 * (x + 0.044715 * (x * x * x))))


def _lru_kernel(xb_ref, gb_ref, cw_ref, cb_ref, wra_ref, bra_ref, wrx_ref, brx_ref, lam_ref,
                out_ref, hist_ref, hc_ref, a_s, u_s):
    t = pl.program_id(1)
    tt, w = xb_ref.shape

    @pl.when(t == 0)
    def _():
        hist_ref[...] = jnp.zeros_like(hist_ref)
        hc_ref[...] = jnp.zeros_like(hc_ref)

    row = lax.broadcasted_iota(jnp.int32, (tt, w), 0)
    valid = (row >= PAD) | (t > 0)
    x = jnp.where(valid, xb_ref[...].astype(F32), 0.0)
    xfull = jnp.concatenate([hist_ref[...], x], axis=0)
    conv = cb_ref[...] + cw_ref[CONV_W - 1:CONV_W, :] * x
    for d in range(1, CONV_W):
        shifted = pltpu.roll(xfull, d, 0)[SUBLANE:, :]
        conv = conv + cw_ref[CONV_W - 1 - d:CONV_W - d, :] * shifted
    hist_ref[...] = x[tt - SUBLANE:, :]

    xcb = conv.astype(BF16)
    r_parts, i_parts = [], []
    for blk in range(B_BLOCKS):
        xs = xcb[:, blk * B_BLOCK:(blk + 1) * B_BLOCK]
        r_parts.append(_dot(xs, wra_ref[blk]))
        i_parts.append(_dot(xs, wrx_ref[blk]))
    r = _sigmoid(jnp.concatenate(r_parts, axis=1) + bra_ref[...])
    ig = _sigmoid(jnp.concatenate(i_parts, axis=1) + brx_ref[...])
    nlam = -lam_ref[...]
    softplus = jnp.maximum(nlam, 0.0) + jnp.log1p(jnp.exp(-jnp.abs(nlam)))
    log_a = (-LRU_C) * r * softplus
    a = jnp.exp(log_a)
    u = jnp.sqrt(1.0 - a * a) * (ig * conv)
    a_s[...] = a
    u_s[...] = jnp.where(valid, u, 0.0)

    srow = lax.broadcasted_iota(jnp.int32, (SUBLANE, w), 0)

    def group(gi, hprev):
        sl = pl.ds(pl.multiple_of(gi * SUBLANE, SUBLANE), SUBLANE)
        ag = a_s[sl, :]
        ug = u_s[sl, :]
        for d in (1, 2, 4):
            ok = srow >= d
            a_sh = jnp.where(ok, pltpu.roll(ag, d, 0), 1.0)
            u_sh = jnp.where(ok, pltpu.roll(ug, d, 0), 0.0)
            ug = ag * u_sh + ug
            ag = ag * a_sh
        hg = ug + ag * hprev
        u_s[sl, :] = hg
        return jnp.broadcast_to(hg[SUBLANE - 1:SUBLANE, :], (SUBLANE, w))

    hc_ref[...] = lax.fori_loop(0, tt // SUBLANE, group, hc_ref[...])
    out_ref[...] = (u_s[...] * _gelu_tanh(gb_ref[...].astype(F32))).astype(out_ref.dtype)


def _lru(xb, gb, cw, cb, wra, bra, wrx, brx, lam, bsz, tp, tt):
    w = xb.shape[1]
    nt = tp // tt
    row_map = lambda b, t: (b * nt + t, 0)
    const2 = lambda b, t: (0, 0)
    const3 = lambda b, t: (0, 0, 0)
    return pl.pallas_call(
        _lru_kernel,
        out_shape=jax.ShapeDtypeStruct(xb.shape, BF16),
        grid=(bsz, nt),
        in_specs=[pl.BlockSpec((tt, w), row_map), pl.BlockSpec((tt, w), row_map),
                  pl.BlockSpec((CONV_W, w), const2), pl.BlockSpec((1, w), const2),
                  pl.BlockSpec((B_BLOCKS, B_BLOCK, B_BLOCK), const3), pl.BlockSpec((1, w), const2),
                  pl.BlockSpec((B_BLOCKS, B_BLOCK, B_BLOCK), const3), pl.BlockSpec((1, w), const2),
                  pl.BlockSpec((1, w), const2)],
        out_specs=pl.BlockSpec((tt, w), row_map),
        scratch_shapes=[pltpu.VMEM((SUBLANE, w), F32), pltpu.VMEM((SUBLANE, w), F32),
                        pltpu.VMEM((tt, w), F32), pltpu.VMEM((tt, w), F32)],
        compiler_params=_cparams(("parallel", "arbitrary")),
        name="conv_rglru",
    )(xb, gb, cw, cb, wra, bra, wrx, brx, lam)


def _out_mlp_kernel(n_mix, h_ref, *refs):
    mix_refs = refs[:n_mix]
    wo_refs = refs[n_mix:2 * n_mix]
    ln_ref, w1_ref, w2_ref, out_ref, res_s, hn_s, acc_s = refs[2 * n_mix:]
    j = pl.program_id(1)

    @pl.when(j == 0)
    def _():
        res = h_ref[...]
        for m_ref, w_ref in zip(mix_refs, wo_refs):
            res = res + _dot(m_ref[...], w_ref[...])
        res_s[...] = res
        hn_s[...] = _rms_rows(res, ln_ref[...]).astype(BF16)
        acc_s[...] = jnp.zeros_like(acc_s)

    a = jnp.maximum(_dot(hn_s[...], w1_ref[...]), 0.0)
    acc_s[...] += _dot((a * a).astype(BF16), w2_ref[...])

    @pl.when(j == pl.num_programs(1) - 1)
    def _():
        out_ref[...] = res_s[...] + acc_s[...]


def _out_mlp(h2, mixes, wouts, ln, w1, w2, tm, tf):
    r, d = h2.shape
    f = w1.shape[1]
    n_mix = len(mixes)
    in_specs = [pl.BlockSpec((tm, d), lambda i, j: (i, 0))]
    in_specs += [pl.BlockSpec((tm, m.shape[1]), lambda i, j: (i, 0)) for m in mixes]
    in_specs += [pl.BlockSpec(w.shape, lambda i, j: (0, 0)) for w in wouts]
    in_specs += [pl.BlockSpec((1, d), lambda i, j: (0, 0)),
                 pl.BlockSpec((d, tf), lambda i, j: (0, j)),
                 pl.BlockSpec((tf, d), lambda i, j: (j, 0))]
    return pl.pallas_call(
        functools.partial(_out_mlp_kernel, n_mix),
        out_shape=jax.ShapeDtypeStruct((r, d), F32),
        grid=(r // tm, f // tf),
        in_specs=in_specs,
        out_specs=pl.BlockSpec((tm, d), lambda i, j: (i, 0)),
        scratch_shapes=[pltpu.VMEM((tm, d), F32), pltpu.VMEM((tm, d), BF16), pltpu.VMEM((tm, d), F32)],
        compiler_params=_cparams(("parallel", "arbitrary")),
        name="out_mlp",
    )(h2, *mixes, *wouts, ln, w1, w2)


OD_COLS = dict(cq=(0, 384), ckv=(384, 640), kr=(640, 768), fq=(768, 1280), fk=(1280, 1792),
               fv=(1792, 2304), ff=(2304, 2432))
OD_N = 2432


def _od_in_kernel(h_ref, ln_ref, w_ref, pos_ref, frq_ref, sgn_ref, gqa_ref, gkva_ref,
                  wq_ref, wkn_ref, wv_ref, gq_ref, gk_ref, gkr_ref, gfq_ref, gfk_ref, bf_ref,
                  gm_ref, g64_ref, ilm_ref, sel_ref,
                  qm_ref, km_ref, vmt_ref, qf_ref, kf_ref, vft_ref, fc_ref):
    t = pl.program_id(1)
    tm = h_ref.shape[1]

    @pl.when(t == 0)
    def _():
        fc_ref[...] = jnp.zeros_like(fc_ref)

    xn = _rms_rows(h_ref[0], ln_ref[...]).astype(BF16)

    def proj(name):
        a, b = OD_COLS[name]
        return _dot(xn, w_ref[:, a:b])

    lane = lax.broadcasted_iota(jnp.int32, (tm, LANE), 1)
    ang = pos_ref[0] * frq_ref[...]
    cs = jnp.cos(ang)
    sn = jnp.sin(ang) * sgn_ref[...]
    first_half = lane < ROPE_MID

    def seg_rms(x, sel, inv_len, gain):
        ss = _dot_sel_r(x * x, sel, 2)
        return x * lax.rsqrt(ss * inv_len + EPS) * gain

    def rope(y):
        sw = jnp.where(first_half, pltpu.roll(y, LANE - HALF, 1), pltpu.roll(y, HALF, 1))
        return y * cs + sw * sn

    gm = gm_ref[...]
    g64 = g64_ref[...]
    ilm = ilm_ref[...]
    inv64 = 1.0 / D_HD

    cqn = _rms_rows(proj("cq"), gqa_ref[...]).astype(BF16)
    qall = _dot(cqn, wq_ref[...])
    scale_m = (C_NOPE + C_ROPE) ** -0.5
    for h in range(C_HEADS):
        y = rope(seg_rms(qall[:, h * LANE:(h + 1) * LANE], gm, ilm, gq_ref[...]))
        qm_ref[0, :, h * LANE:(h + 1) * LANE] = (y * scale_m).astype(BF16)
    ckvn = _rms_rows(proj("ckv"), gkva_ref[...]).astype(BF16)
    knall = _dot(ckvn, wkn_ref[...])
    krt = rope(seg_rms(proj("kr"), gm, ilm, gkr_ref[...]))
    for h in range(C_HEADS):
        y = seg_rms(knall[:, h * LANE:(h + 1) * LANE], gm, ilm, gk_ref[...]) + krt
        km_ref[0, :, h * LANE:(h + 1) * LANE] = y.astype(BF16)
    vmt_ref[0] = _dot(ckvn, wv_ref[...]).T.astype(BF16)

    fq = proj("fq")
    fk = proj("fk")
    scale_f = D_HD ** -0.5
    npair = D_HEADS // 2
    for p in range(npair):
        y = seg_rms(fq[:, p * LANE:(p + 1) * LANE], g64, inv64, gfq_ref[...])
        qf_ref[0, :, p * LANE:(p + 1) * LANE] = (y * scale_f).astype(BF16)
        y = seg_rms(fk[:, p * LANE:(p + 1) * LANE], g64, inv64, gfk_ref[...])
        kf_ref[0, :, 2 * p * LANE:(2 * p + 1) * LANE] = y.astype(BF16)
    vft_ref[0] = proj("fv").T.astype(BF16)

    row = lax.broadcasted_iota(jnp.int32, (tm, LANE), 0)
    valid = ((row >= PAD) | (t > 0)) & (lane < D_HEADS)
    lf = jnp.where(valid, _log_sigmoid(proj("ff") + bf_ref[...]), 0.0)
    fc = _dot_sel_l(_tril_bf16(tm), lf, 3) + fc_ref[...]
    fc_ref[...] = fc[tm - 1:tm, :]
    hi, mid, lo = _split3(-fc)
    bias = _dot(hi, sel_ref[0]) + _dot(mid, sel_ref[1]) + _dot(lo, sel_ref[2])
    for p in range(npair):
        kf_ref[0, :, (2 * p + 1) * LANE:(2 * p + 2) * LANE] = bias[:, p * LANE:(p + 1) * LANE].astype(BF16)


def _od_in(h3, pos_col, consts, tm):
    bsz, tp, d = h3.shape
    nt = tp // tm
    names = ["ln", "w", None, "frq", "sgn", "gqa", "gkva", "wq", "wkn", "wv", "gq", "gk", "gkr",
             "gfq", "gfk", "bf", "gm", "g64", "ilm", "sel"]
    args, in_specs = [h3], [pl.BlockSpec((1, tm, d), lambda b, t: (b, t, 0))]
    for nm in names:
        if nm is None:
            args.append(pos_col)
            in_specs.append(pl.BlockSpec((1, tm, 1), lambda b, t: (b, t, 0)))
            continue
        a = consts[nm]
        args.append(a)
        in_specs.append(pl.BlockSpec(a.shape, (lambda nd: (lambda b, t: (0,) * nd))(a.ndim)))
    hl = C_HEADS * LANE
    pk = D_HEADS * D_HD
    outs = [jax.ShapeDtypeStruct((bsz, tp, hl), BF16), jax.ShapeDtypeStruct((bsz, tp, hl), BF16),
            jax.ShapeDtypeStruct((bsz, pk, tp), BF16), jax.ShapeDtypeStruct((bsz, tp, pk), BF16),
            jax.ShapeDtypeStruct((bsz, tp, 2 * pk), BF16), jax.ShapeDtypeStruct((bsz, pk, tp), BF16)]
    rowspec = lambda n: pl.BlockSpec((1, tm, n), lambda b, t: (b, t, 0))
    tspec = pl.BlockSpec((1, pk, tm), lambda b, t: (b, 0, t))
    return pl.pallas_call(
        _od_in_kernel,
        out_shape=outs,
        grid=(bsz, nt),
        in_specs=in_specs,
        out_specs=[rowspec(hl), rowspec(hl), tspec, rowspec(pk), rowspec(2 * pk), tspec],
        scratch_shapes=[pltpu.VMEM((1, LANE), F32)],
        compiler_params=_cparams(("parallel", "arbitrary")),
        name="od_in",
    )(*args)


def _attn_kernel(fox, q_ref, k_ref, vt_ref, out_ref, st_ref, acc_ref, *maybe_qe):
    i = pl.program_id(2)
    j = pl.program_id(3)
    bq = q_ref.shape[1]
    bk = k_ref.shape[1]
    half = LANE // 2

    @pl.when(j == 0)
    def _():
        st_ref[0:2, :] = jnp.full((2, bq), NEG, F32)
        st_ref[2:4, :] = jnp.zeros((2, bq), F32)
        acc_ref[...] = jnp.zeros_like(acc_ref)
        if fox:
            qe_ref = maybe_qe[0]
            q = q_ref[0]
            lane = lax.broadcasted_iota(jnp.int32, (bq, LANE), 1)
            zero = jnp.zeros_like(q)
            for hd in range(2):
                mine = (lane >= hd * half) & (lane < (hd + 1) * half)
                ones = ((lane >= 3 * hd) & (lane < 3 * hd + 3)).astype(F32).astype(BF16)
                qe_ref[hd] = jnp.concatenate([jnp.where(mine, q, zero), ones], axis=1)

    def body(masked):
        kblk = k_ref[0]
        vt = vt_ref[0]
        if masked:
            kpos = j * bk + lax.broadcasted_iota(jnp.int32, (bk, bq), 0)
            qpos = i * bq + lax.broadcasted_iota(jnp.int32, (bk, bq), 1)
            ok = (kpos <= qpos) & ((kpos >= PAD) | (kpos == qpos))
        for hd in range(2):
            if fox:
                s = _dot_nt(kblk, maybe_qe[0][hd])
            else:
                s = _dot_nt(kblk[:, hd * LANE:(hd + 1) * LANE], q_ref[0, :, hd * LANE:(hd + 1) * LANE])
            if masked:
                s = jnp.where(ok, s, NEG)
            m_old = st_ref[hd:hd + 1, :]
            l_old = st_ref[2 + hd:3 + hd, :]
            m_new = jnp.maximum(m_old, jnp.max(s, axis=0, keepdims=True))
            alpha = jnp.exp(m_old - m_new)
            p = jnp.exp(s - m_new)
            st_ref[hd:hd + 1, :] = m_new
            st_ref[2 + hd:3 + hd, :] = alpha * l_old + jnp.sum(p, axis=0, keepdims=True)
            acc_ref[hd] = acc_ref[hd] * alpha + _dot(vt, p.astype(BF16))

    need_mask = (j == i) | (j == 0)

    @pl.when((j <= i) & need_mask)
    def _():
        body(True)

    @pl.when((j <= i) & jnp.logical_not(need_mask))
    def _():
        body(False)

    @pl.when(j == i)
    def _():
        o_a = acc_ref[0, 0:half, :] / st_ref[2:3, :]
        o_b = acc_ref[1, half:LANE, :] / st_ref[3:4, :]
        out_ref[0] = jnp.concatenate([o_a, o_b], axis=0).T.astype(out_ref.dtype)


def _attention(q, k, vt, fox, blk):
    bsz, tp, _ = q.shape
    npair = vt.shape[1] // LANE
    nb = tp // blk
    qw = LANE if fox else 2 * LANE
    scratch = [pltpu.VMEM((SUBLANE, blk), F32), pltpu.VMEM((2, LANE, blk), F32)]
    if fox:
        scratch.append(pltpu.VMEM((2, blk, 2 * LANE), BF16))
    return pl.pallas_call(
        functools.partial(_attn_kernel, fox),
        out_shape=jax.ShapeDtypeStruct((bsz, tp, npair * LANE), BF16),
        grid=(bsz, npair, nb, nb),
        in_specs=[pl.BlockSpec((1, blk, qw), lambda b, p, i, j: (b, i, p)),
                  pl.BlockSpec((1, blk, 2 * LANE), lambda b, p, i, j: (b, jnp.minimum(j, i), p)),
                  pl.BlockSpec((1, LANE, blk), lambda b, p, i, j: (b, p, jnp.minimum(j, i)))],
        out_specs=pl.BlockSpec((1, blk, LANE), lambda b, p, i, j: (b, i, p)),
        scratch_shapes=scratch,
        compiler_params=_cparams(("parallel", "parallel", "parallel", "arbitrary")),
        name="fox_attn" if fox else "mla_attn",
    )(q, k, vt)


def _place_cols(w, width, offset):
    k, n = w.shape
    return jnp.pad(w, ((0, 0), (offset, width - offset - n)))


def _ev_consts(w_in, b_if):
    nq, nv = A_HEADS * A_DQK, A_HEADS * A_DV
    o = 0
    wq = w_in[:, o:o + nq]; o += nq
    wk = w_in[:, o:o + nq]; o += nq
    wv = w_in[:, o:o + nv]; o += nv
    wo = w_in[:, o:o + nv]; o += nv
    wg = w_in[:, o:o + 2 * A_HEADS]; o += 2 * A_HEADS
    wxb = w_in[:, o:o + 1024]; o += 1024
    wgb = w_in[:, o:o + 1024]
    w = jnp.concatenate([wq, wk, wv, wo, _place_cols(wg, LANE, 0), wxb, wgb], axis=1).astype(BF16)
    bif = _place_cols(b_if[None, :], LANE, 0)
    return w, bif


def _head_tile_cols(w, n_heads, per_head, take, offset):
    k = w.shape[0]
    wh = w.reshape(k, n_heads, per_head)[:, :, take[0]:take[0] + take[1]]
    wh = jnp.pad(wh, ((0, 0), (0, 0), (offset, LANE - offset - take[1])))
    return wh.reshape(k, n_heads * LANE)


def _row128(parts):
    r = jnp.zeros((LANE,), F32)
    for off, v in parts:
        r = r.at[off:off + v.shape[0]].set(v.astype(F32))
    return r[None, :]


def _od_consts(p):
    c = {}
    w_in = p["w_in"]
    o = 0
    wcq = w_in[:, o:o + C_Q_LORA]; o += C_Q_LORA
    wckv = w_in[:, o:o + C_KV_LORA]; o += C_KV_LORA
    wkr = w_in[:, o:o + C_ROPE]; o += C_ROPE
    nf = D_HEADS * D_HD
    wfq = w_in[:, o:o + nf]; o += nf
    wfk = w_in[:, o:o + nf]; o += nf
    wfv = w_in[:, o:o + nf]; o += nf
    wff = w_in[:, o:o + D_HEADS]
    c["w"] = jnp.concatenate([wcq, wckv, _place_cols(wkr, LANE, ROPE_LO), wfq, wfk, wfv,
                              _place_cols(wff, LANE, 0)], axis=1).astype(BF16)
    c["ln"] = p["ln"][None, :]
    freqs = ROPE_THETA ** (-jnp.arange(HALF, dtype=F32) / HALF)
    c["frq"] = _row128([(ROPE_LO, freqs), (ROPE_MID, freqs)])
    c["sgn"] = _row128([(ROPE_LO, -jnp.ones((HALF,), F32)), (ROPE_MID, jnp.ones((HALF,), F32))])
    c["gqa"] = p["g_qa"][None, :]
    c["gkva"] = p["g_kva"][None, :]
    dq = C_NOPE + C_ROPE
    wuq, wukv = p["w_uq"], p["w_ukv"]
    c["wq"] = (_head_tile_cols(wuq, C_HEADS, dq, (0, dq), 0)).astype(BF16)
    c["wkn"] = _head_tile_cols(wukv, C_HEADS, C_NOPE + C_V, (0, C_NOPE), 0).astype(BF16)
    c["wv"] = wukv.reshape(-1, C_HEADS, C_NOPE + C_V)[:, :, C_NOPE:].reshape(-1, C_HEADS * C_V).astype(BF16)
    c["gq"] = _row128([(0, p["g_qn"]), (ROPE_LO, p["g_qr"])])
    c["gk"] = _row128([(0, p["g_kn"])])
    c["gkr"] = _row128([(ROPE_LO, p["g_kr"])])
    c["gfq"] = _row128([(0, p["g_fq"]), (D_HD, p["g_fq"])])
    c["gfk"] = _row128([(0, p["g_fk"]), (D_HD, p["g_fk"])])
    c["bf"] = _row128([(0, p["b_f"])])
    seg = np.where(np.arange(LANE) < C_NOPE, 0, np.where(np.arange(LANE) < ROPE_HI, 1, 2))
    c["gm"] = jnp.asarray(seg[:, None] == seg[None, :], F32).astype(BF16)
    seg64 = np.arange(LANE) // D_HD
    c["g64"] = jnp.asarray(seg64[:, None] == seg64[None, :], F32).astype(BF16)
    c["ilm"] = jnp.asarray(np.where(seg == 0, 1.0 / C_NOPE, 1.0 / C_ROPE)[None, :], F32)
    sel = np.zeros((3, LANE, (D_HEADS // 2) * LANE), np.float32)
    for h in range(D_HEADS):
        for part in range(3):
            sel[part, h, (h // 2) * LANE + 3 * (h % 2) + part] = 1.0
    c["sel"] = jnp.asarray(sel).astype(BF16)
    return c


def _time_tile(tp):
    return 3 * LANE if tp % (3 * LANE) == 0 else LANE


def kernel(x, positions, meta, ev_ln, ev_w_in, ev_b_if, ev_a_norm, ev_conv_w, ev_conv_b, ev_w_ra, ev_b_ra, ev_w_rx, ev_b_rx, ev_lam, ev_w_out, od_ln, od_w_in, od_b_f, od_g_qa, od_g_kva, od_w_uq, od_w_ukv, od_g_qn, od_g_qr, od_g_kn, od_g_kr, od_g_fq, od_g_fk, od_w_out, mlp_ln, w_ff1, w_ff2):
    bsz, seq, d = x.shape
    tp = LANE + seq
    r = bsz * tp
    tm = 512 if r % 512 == 0 else LANE
    tf = 1024
    tt = _time_tile(tp)

    h = jnp.concatenate([jnp.zeros((bsz, PAD, d), x.dtype),
                         jnp.broadcast_to(meta.astype(x.dtype), (bsz, N_META, d)), x], axis=1)
    pos_full = jnp.concatenate([jnp.zeros((bsz, PAD), jnp.int32),
                                jnp.broadcast_to(jnp.arange(N_META, dtype=jnp.int32)[None, :], (bsz, N_META)),
                                positions + N_META], axis=1)
    pos_col = pos_full.astype(F32)[:, :, None]
    h2 = h.reshape(r, d)

    e = 0
    w_a, bif = _ev_consts(ev_w_in[e], ev_b_if[e])
    q, k, v, o, g, xb, gb = _ev_in(h2, ev_ln[e][None, :], w_a, tm)
    ha = _mlstm(q, k, v, o, g, bif, ev_a_norm[e][None, :], bsz, tp, LANE)
    hb = _lru(xb, gb, ev_conv_w[e], ev_conv_b[e][None, :], ev_w_ra[e].astype(BF16), ev_b_ra[e][None, :],
              ev_w_rx[e].astype(BF16), ev_b_rx[e][None, :], ev_lam[e][None, :], bsz, tp, tt)
    nv = A_HEADS * A_DV
    wo = ev_w_out[e].astype(BF16)
    h2 = _out_mlp(h2, [ha, hb], [wo[:nv], wo[nv:]], mlp_ln[0][None, :],
                  w_ff1[0].astype(BF16), w_ff2[0].astype(BF16), tm, tf)

    oi = 0
    consts = _od_consts(dict(ln=od_ln[oi], w_in=od_w_in[oi], b_f=od_b_f[oi], g_qa=od_g_qa[oi],
                             g_kva=od_g_kva[oi], w_uq=od_w_uq[oi], w_ukv=od_w_ukv[oi],
                             g_qn=od_g_qn[oi], g_qr=od_g_qr[oi], g_kn=od_g_kn[oi], g_kr=od_g_kr[oi],
                             g_fq=od_g_fq[oi], g_fk=od_g_fk[oi]))
    qm, km, vmt, qf, kf, vft = _od_in(h2.reshape(bsz, tp, d), pos_col, consts, tt)
    hc = _attention(qm, km, vmt, False, tt)
    hd = _attention(qf, kf, vft, True, tt)
    nc = C_HEADS * C_V
    wo = od_w_out[oi].astype(BF16)
    h2 = _out_mlp(h2, [hc.reshape(r, nc), hd.reshape(r, nc)], [wo[:nc], wo[nc:]], mlp_ln[1][None, :],
                  w_ff1[1].astype(BF16), w_ff2[1].astype(BF16), tm, tf)
    return h2.reshape(bsz, tp, d)[:, LANE:]
```

```python
import functools
import math

import numpy as np
import jax
import jax.numpy as jnp
from jax import lax
from jax.experimental import pallas as pl
from jax.experimental.pallas import tpu as pltpu

F32 = jnp.float32
BF16 = jnp.bfloat16

N_META = 16
EPS = 1e-6
A_HEADS = 4
A_DQK = 128
A_DV = 256
B_BLOCKS = 8
B_BLOCK = 128
CONV_W = 4
LRU_C = 8.0
C_HEADS = 8
C_Q_LORA = 384
C_KV_LORA = 256
C_NOPE = 64
C_ROPE = 32
C_V = 64
ROPE_THETA = 10000.0
D_HEADS = 8
D_HD = 64

LANE = 128
SUBLANE = 8
VMEM_LIMIT = 56 * 1024 * 1024

PAD = LANE - N_META
NEG = -1e30
LOG2E = math.log2(math.e)

HALF = C_ROPE // 2
ROPE_LO = C_NOPE
ROPE_MID = C_NOPE + HALF
ROPE_HI = C_NOPE + C_ROPE
MASK_LANE = ROPE_HI


def _cparams(sem):
    return pltpu.CompilerParams(dimension_semantics=sem, vmem_limit_bytes=VMEM_LIMIT)


def _dot(a, b):
    return jnp.dot(a, b, preferred_element_type=F32)


def _dot_nt(a, b):
    return lax.dot_general(a, b, (((1,), (1,)), ((), ())), preferred_element_type=F32)


def _split2(x):
    hi = x.astype(BF16)
    lo = (x - hi.astype(F32)).astype(BF16)
    return hi, lo


def _split3(x):
    hi = x.astype(BF16)
    r = x - hi.astype(F32)
    mid = r.astype(BF16)
    lo = (r - mid.astype(F32)).astype(BF16)
    return hi, mid, lo


def _dot_sel_r(x, sel, parts):
    pieces = _split3(x) if parts == 3 else _split2(x)
    acc = _dot(pieces[0], sel)
    for p in pieces[1:]:
        acc = acc + _dot(p, sel)
    return acc


def _dot_sel_l(sel, x, parts):
    pieces = _split3(x) if parts == 3 else _split2(x)
    acc = _dot(sel, pieces[0])
    for p in pieces[1:]:
        acc = acc + _dot(sel, p)
    return acc


def _rms_rows(x, g):
    ms = jnp.mean(x * x, axis=-1, keepdims=True)
    return x * lax.rsqrt(ms + EPS) * g


def _log_sigmoid(x):
    return -(jnp.maximum(-x, 0.0) + jnp.log1p(jnp.exp(-jnp.abs(x))))


def _sigmoid(x):
    return 1.0 / (1.0 + jnp.exp(-x))


def _tril_bf16(n):
    r = lax.broadcasted_iota(jnp.int32, (n, n), 0)
    c = lax.broadcasted_iota(jnp.int32, (n, n), 1)
    return jnp.where(r >= c, 1.0, 0.0).astype(BF16)


EV_SLABS = (("q", 512, BF16), ("k", 512, BF16), ("v", 1024, BF16), ("o", 1024, BF16),
            ("g", 128, F32), ("xb", 1024, BF16), ("gb", 1024, BF16))


def _ev_in_kernel(h_ref, ln_ref, w_ref, *out_refs):
    xn = _rms_rows(h_ref[...], ln_ref[...]).astype(BF16)
    off = 0
    for (_, width, dt), ref in zip(EV_SLABS, out_refs):
        ref[...] = _dot(xn, w_ref[:, off:off + width]).astype(dt)
        off += width


def _ev_in(h2, ln, w, tm):
    r, d = h2.shape
    n = w.shape[1]
    outs = [jax.ShapeDtypeStruct((r, width), dt) for (_, width, dt) in EV_SLABS]
    return pl.pallas_call(
        _ev_in_kernel,
        out_shape=outs,
        grid=(r // tm,),
        in_specs=[pl.BlockSpec((tm, d), lambda i: (i, 0)),
                  pl.BlockSpec((1, d), lambda i: (0, 0)),
                  pl.BlockSpec((d, n), lambda i: (0, 0))],
        out_specs=[pl.BlockSpec((tm, width), lambda i: (i, 0)) for (_, width, _) in EV_SLABS],
        compiler_params=_cparams(("parallel",)),
        name="ev_in",
    )(h2, ln, w)


def _mlstm_kernel(q_ref, k_ref, v_ref, o_ref, g_ref, bif_ref, an_ref, out_ref, cn_ref, m_ref):
    c = pl.program_id(1)
    L = q_ref.shape[0]
    dv_ext = A_DV + LANE

    @pl.when(c == 0)
    def _():
        cn_ref[...] = jnp.zeros_like(cn_ref)
        m_ref[...] = jnp.zeros_like(m_ref)

    lane = lax.broadcasted_iota(jnp.int32, (L, LANE), 1)
    row = lax.broadcasted_iota(jnp.int32, (L, LANE), 0)
    valid = (row >= PAD) | (c > 0)
    g = g_ref[...] + bif_ref[...]
    li = jnp.where(valid, g, NEG)
    lf = jnp.where(valid, _log_sigmoid(g), 0.0)
    is_f = (lane >= A_HEADS) & (lane < 2 * A_HEADS)
    lf = jnp.where(is_f, lf, 0.0)
    tril = _tril_bf16(L)
    bcum = _dot_sel_l(tril, lf, 3)
    li_t = li.T
    bcum_t = bcum.T

    rr = lax.broadcasted_iota(jnp.int32, (L, L), 0)
    cc = lax.broadcasted_iota(jnp.int32, (L, L), 1)
    causal = rr >= cc
    e0 = jnp.where(lane == 0, 1.0, 0.0).astype(BF16)
    scale = A_DQK ** -0.5

    for h in range(A_HEADS):
        qs = (q_ref[:, h * A_DQK:(h + 1) * A_DQK].astype(F32) * scale).astype(BF16)
        kk = k_ref[:, h * A_DQK:(h + 1) * A_DQK]
        vext = jnp.concatenate([v_ref[:, h * A_DV:(h + 1) * A_DV], e0], axis=1)
        i_col = li[:, h:h + 1]
        b_col = bcum[:, A_HEADS + h:A_HEADS + h + 1]
        i_row = li_t[h:h + 1, :]
        b_row = bcum_t[A_HEADS + h:A_HEADS + h + 1, :]
        m_state = m_ref[:, h:h + 1]
        cn = cn_ref[h]

        dmat = jnp.where(causal, b_col - b_row + i_row, NEG)
        inter = b_col + m_state
        m_row = jnp.maximum(inter, jnp.max(dmat, axis=1, keepdims=True))
        w_intra = jnp.exp(dmat - m_row)
        w_inter = jnp.exp(inter - m_row)
        s = (_dot_nt(qs, kk) * w_intra).astype(BF16)
        numden = w_inter * _dot(qs, cn.astype(BF16)) + _dot(s, vext)
        num = numden[:, :A_DV]
        den = numden[:, A_DV:A_DV + 1]
        hh = num / jnp.maximum(jnp.abs(den), jnp.exp(-m_row))
        hn = _rms_rows(hh, an_ref[...])
        og = _sigmoid(o_ref[:, h * A_DV:(h + 1) * A_DV].astype(F32))
        out_ref[:, h * A_DV:(h + 1) * A_DV] = (hn * og).astype(out_ref.dtype)

        gsum = b_col[L - 1:L, :]
        dk_col = gsum - b_col + i_col
        dk_row = gsum - b_row + i_row
        m_new = jnp.maximum(gsum + m_state, jnp.max(dk_row, axis=1, keepdims=True))
        wk = jnp.exp(dk_col - m_new)
        sc = jnp.exp(gsum + m_state - m_new)
        kw_t = (kk.astype(F32) * wk).T.astype(BF16)
        cn_ref[h] = sc * cn + _dot(kw_t, vext)
        m_ref[:, h:h + 1] = m_new


def _mlstm(q, k, v, o, g, bif, an, bsz, tp, L):
    nq, nv = A_HEADS * A_DQK, A_HEADS * A_DV
    nc = tp // L
    row_map = lambda b, c: (b * nc + c, 0)
    const = lambda b, c: (0, 0)
    return pl.pallas_call(
        _mlstm_kernel,
        out_shape=jax.ShapeDtypeStruct((bsz * tp, nv), BF16),
        grid=(bsz, nc),
        in_specs=[pl.BlockSpec((L, nq), row_map), pl.BlockSpec((L, nq), row_map),
                  pl.BlockSpec((L, nv), row_map), pl.BlockSpec((L, nv), row_map),
                  pl.BlockSpec((L, LANE), row_map),
                  pl.BlockSpec((1, LANE), const), pl.BlockSpec((1, A_DV), const)],
        out_specs=pl.BlockSpec((L, nv), row_map),
        scratch_shapes=[pltpu.VMEM((A_HEADS, A_DQK, A_DV + LANE), F32),
                        pltpu.VMEM((1, LANE), F32)],
        compiler_params=_cparams(("parallel", "arbitrary")),
        name="mlstm",
    )(q, k, v, o, g, bif, an)


def _gelu_tanh(x):
    c0 = math.sqrt(2.0 / math.pi)
    return 0.5 * x * (1.0 + jnp.tanh(c0 * (x + 0.044715 * (x * x * x))))


def _lru_kernel(xb_ref, gb_ref, cw_ref, cb_ref, wra_ref, bra_ref, wrx_ref, brx_ref, lam_ref,
                out_ref, hist_ref, hc_ref, a_s, u_s):
    t = pl.program_id(1)
    tt, w = xb_ref.shape

    @pl.when(t == 0)
    def _():
        hist_ref[...] = jnp.zeros_like(hist_ref)
        hc_ref[...] = jnp.zeros_like(hc_ref)

    row = lax.broadcasted_iota(jnp.int32, (tt, w), 0)
    valid = (row >= PAD) | (t > 0)
    x = jnp.where(valid, xb_ref[...].astype(F32), 0.0)
    xfull = jnp.concatenate([hist_ref[...], x], axis=0)
    conv = cb_ref[...] + cw_ref[CONV_W - 1:CONV_W, :] * x
    for d in range(1, CONV_W):
        shifted = pltpu.roll(xfull, d, 0)[SUBLANE:, :]
        conv = conv + cw_ref[CONV_W - 1 - d:CONV_W - d, :] * shifted
    hist_ref[...] = x[tt - SUBLANE:, :]

    xcb = conv.astype(BF16)
    r_parts, i_parts = [], []
    for blk in range(B_BLOCKS):
        xs = xcb[:, blk * B_BLOCK:(blk + 1) * B_BLOCK]
        r_parts.append(_dot(xs, wra_ref[blk]))
        i_parts.append(_dot(xs, wrx_ref[blk]))
    r = _sigmoid(jnp.concatenate(r_parts, axis=1) + bra_ref[...])
    ig = _sigmoid(jnp.concatenate(i_parts, axis=1) + brx_ref[...])
    nlam = -lam_ref[...]
    softplus = jnp.maximum(nlam, 0.0) + jnp.log1p(jnp.exp(-jnp.abs(nlam)))
    log_a = (-LRU_C) * r * softplus
    a = jnp.exp(log_a)
    u = jnp.sqrt(1.0 - a * a) * (ig * conv)
    a_s[...] = a
    u_s[...] = jnp.where(valid, u, 0.0)

    srow = lax.broadcasted_iota(jnp.int32, (SUBLANE, w), 0)

    def group(gi, hprev):
        sl = pl.ds(pl.multiple_of(gi * SUBLANE, SUBLANE), SUBLANE)
        ag = a_s[sl, :]
        ug = u_s[sl, :]
        for d in (1, 2, 4):
            ok = srow >= d
            a_sh = jnp.where(ok, pltpu.roll(ag, d, 0), 1.0)
            u_sh = jnp.where(ok, pltpu.roll(ug, d, 0), 0.0)
            ug = ag * u_sh + ug
            ag = ag * a_sh
        hg = ug + ag * hprev
        u_s[sl, :] = hg
        return jnp.broadcast_to(hg[SUBLANE - 1:SUBLANE, :], (SUBLANE, w))

    hc_ref[...] = lax.fori_loop(0, tt // SUBLANE, group, hc_ref[...])
    out_ref[...] = (u_s[...] * _gelu_tanh(gb_ref[...].astype(F32))).astype(out_ref.dtype)


def _lru(xb, gb, cw, cb, wra, bra, wrx, brx, lam, bsz, tp, tt):
    w = xb.shape[1]
    nt = tp // tt
    row_map = lambda b, t: (b * nt + t, 0)
    const2 = lambda b, t: (0, 0)
    const3 = lambda b, t: (0, 0, 0)
    return pl.pallas_call(
        _lru_kernel,
        out_shape=jax.ShapeDtypeStruct(xb.shape, BF16),
        grid=(bsz, nt),
        in_specs=[pl.BlockSpec((tt, w), row_map), pl.BlockSpec((tt, w), row_map),
                  pl.BlockSpec((CONV_W, w), const2), pl.BlockSpec((1, w), const2),
                  pl.BlockSpec((B_BLOCKS, B_BLOCK, B_BLOCK), const3), pl.BlockSpec((1, w), const2),
                  pl.BlockSpec((B_BLOCKS, B_BLOCK, B_BLOCK), const3), pl.BlockSpec((1, w), const2),
                  pl.BlockSpec((1, w), const2)],
        out_specs=pl.BlockSpec((tt, w), row_map),
        scratch_shapes=[pltpu.VMEM((SUBLANE, w), F32), pltpu.VMEM((SUBLANE, w), F32),
                        pltpu.VMEM((tt, w), F32), pltpu.VMEM((tt, w), F32)],
        compiler_params=_cparams(("parallel", "arbitrary")),
        name="conv_rglru",
    )(xb, gb, cw, cb, wra, bra, wrx, brx, lam)


def _out_mlp_kernel(n_mix, h_ref, *refs):
    mix_refs = refs[:n_mix]
    wo_refs = refs[n_mix:2 * n_mix]
    ln_ref, w1_ref, w2_ref, out_ref, res_s, hn_s, acc_s = refs[2 * n_mix:]
    j = pl.program_id(1)

    @pl.when(j == 0)
    def _():
        res = h_ref[...]
        for m_ref, w_ref in zip(mix_refs, wo_refs):
            res = res + _dot(m_ref[...], w_ref[...])
        res_s[...] = res
        hn_s[...] = _rms_rows(res, ln_ref[...]).astype(BF16)
        acc_s[...] = jnp.zeros_like(acc_s)

    a = jnp.maximum(_dot(hn_s[...], w1_ref[...]), 0.0)
    acc_s[...] += _dot((a * a).astype(BF16), w2_ref[...])

    @pl.when(j == pl.num_programs(1) - 1)
    def _():
        out_ref[...] = res_s[...] + acc_s[...]


def _out_mlp(h2, mixes, wouts, ln, w1, w2, tm, tf):
    r, d = h2.shape
    f = w1.shape[1]
    n_mix = len(mixes)
    in_specs = [pl.BlockSpec((tm, d), lambda i, j: (i, 0))]
    in_specs += [pl.BlockSpec((tm, m.shape[1]), lambda i, j: (i, 0)) for m in mixes]
    in_specs += [pl.BlockSpec(w.shape, lambda i, j: (0, 0)) for w in wouts]
    in_specs += [pl.BlockSpec((1, d), lambda i, j: (0, 0)),
                 pl.BlockSpec((d, tf), lambda i, j: (0, j)),
                 pl.BlockSpec((tf, d), lambda i, j: (j, 0))]
    return pl.pallas_call(
        functools.partial(_out_mlp_kernel, n_mix),
        out_shape=jax.ShapeDtypeStruct((r, d), F32),
        grid=(r // tm, f // tf),
        in_specs=in_specs,
        out_specs=pl.BlockSpec((tm, d), lambda i, j: (i, 0)),
        scratch_shapes=[pltpu.VMEM((tm, d), F32), pltpu.VMEM((tm, d), BF16), pltpu.VMEM((tm, d), F32)],
        compiler_params=_cparams(("parallel", "arbitrary")),
        name="out_mlp",
    )(h2, *mixes, *wouts, ln, w1, w2)


OD_COLS = dict(cq=(0, 384), ckv=(384, 640), kr=(640, 768), fq=(768, 1280), fk=(1280, 1792),
               fv=(1792, 2304), ff=(2304, 2432))
OD_N = 2432


def _od_in_kernel(h_ref, ln_ref, w_ref, pos_ref, frq_ref, sgn_ref, gqa_ref, gkva_ref,
                  wq_ref, wkn_ref, wv_ref, gq_ref, gk_ref, gkr_ref, gfq_ref, gfk_ref, bf_ref,
                  gm_ref, g64_ref, ilm_ref, sel_ref,
                  qm_ref, km_ref, vmt_ref, qf_ref, kf_ref, vft_ref, fc_ref):
    t = pl.program_id(1)
    tm = h_ref.shape[1]

    @pl.when(t == 0)
    def _():
        fc_ref[...] = jnp.zeros_like(fc_ref)

    xn = _rms_rows(h_ref[0], ln_ref[...]).astype(BF16)

    def proj(name):
        a, b = OD_COLS[name]
        return _dot(xn, w_ref[:, a:b])

    lane = lax.broadcasted_iota(jnp.int32, (tm, LANE), 1)
    row = lax.broadcasted_iota(jnp.int32, (tm, LANE), 0)
    unused_row = (row < PAD) & (t == 0)
    mask_lane = lane == MASK_LANE
    ang = pos_ref[0] * frq_ref[...]
    cs = jnp.cos(ang)
    sn = jnp.sin(ang) * sgn_ref[...]
    first_half = lane < ROPE_MID

    def seg_rms(x, sel, inv_len, gain):
        ss = _dot_sel_r(x * x, sel, 2)
        return x * lax.rsqrt(ss * inv_len + EPS) * gain

    def rope(y):
        sw = jnp.where(first_half, pltpu.roll(y, LANE - HALF, 1), pltpu.roll(y, HALF, 1))
        return y * cs + sw * sn

    gm = gm_ref[...]
    g64 = g64_ref[...]
    ilm = ilm_ref[...]
    inv64 = 1.0 / D_HD

    cqn = _rms_rows(proj("cq"), gqa_ref[...]).astype(BF16)
    qall = _dot(cqn, wq_ref[...])
    scale_m = (C_NOPE + C_ROPE) ** -0.5 * LOG2E
    for h in range(C_HEADS):
        y = rope(seg_rms(qall[:, h * LANE:(h + 1) * LANE], gm, ilm, gq_ref[...]))
        qm_ref[0, :, h * LANE:(h + 1) * LANE] = jnp.where(mask_lane, 1.0, y * scale_m).astype(BF16)
    ckvn = _rms_rows(proj("ckv"), gkva_ref[...]).astype(BF16)
    knall = _dot(ckvn, wkn_ref[...])
    krt = rope(seg_rms(proj("kr"), gm, ilm, gkr_ref[...]))
    krt = jnp.where(mask_lane & unused_row, NEG, krt)
    for h in range(C_HEADS):
        y = seg_rms(knall[:, h * LANE:(h + 1) * LANE], gm, ilm, gk_ref[...]) + krt
        km_ref[0, :, h * LANE:(h + 1) * LANE] = y.astype(BF16)
    vmt_ref[0] = _dot(ckvn, wv_ref[...]).T.astype(BF16)

    fq = proj("fq")
    fk = proj("fk")
    scale_f = D_HD ** -0.5 * LOG2E
    npair = D_HEADS // 2
    for p in range(npair):
        y = seg_rms(fq[:, p * LANE:(p + 1) * LANE], g64, inv64, gfq_ref[...])
        qf_ref[0, :, p * LANE:(p + 1) * LANE] = (y * scale_f).astype(BF16)
        y = seg_rms(fk[:, p * LANE:(p + 1) * LANE], g64, inv64, gfk_ref[...])
        kf_ref[0, :, 2 * p * LANE:(2 * p + 1) * LANE] = y.astype(BF16)
    vft_ref[0] = proj("fv").T.astype(BF16)

    valid = jnp.logical_not(unused_row) & (lane < D_HEADS)
    lf = jnp.where(valid, _log_sigmoid(proj("ff") + bf_ref[...]), 0.0)
    fc = _dot_sel_l(_tril_bf16(tm), lf, 3) + fc_ref[...]
    fc_ref[...] = fc[tm - 1:tm, :]
    hi, mid, lo = _split3(jnp.where(unused_row, NEG, fc * (-LOG2E)))
    bias = _dot(hi, sel_ref[0]) + _dot(mid, sel_ref[1]) + _dot(lo, sel_ref[2])
    for p in range(npair):
        kf_ref[0, :, (2 * p + 1) * LANE:(2 * p + 2) * LANE] = bias[:, p * LANE:(p + 1) * LANE].astype(BF16)


def _od_in(h3, pos_col, consts, tm):
    bsz, tp, d = h3.shape
    nt = tp // tm
    names = ["ln", "w", None, "frq", "sgn", "gqa", "gkva", "wq", "wkn", "wv", "gq", "gk", "gkr",
             "gfq", "gfk", "bf", "gm", "g64", "ilm", "sel"]
    args, in_specs = [h3], [pl.BlockSpec((1, tm, d), lambda b, t: (b, t, 0))]
    for nm in names:
        if nm is None:
            args.append(pos_col)
            in_specs.append(pl.BlockSpec((1, tm, 1), lambda b, t: (b, t, 0)))
            continue
        a = consts[nm]
        args.append(a)
        in_specs.append(pl.BlockSpec(a.shape, (lambda nd: (lambda b, t: (0,) * nd))(a.ndim)))
    hl = C_HEADS * LANE
    pk = D_HEADS * D_HD
    outs = [jax.ShapeDtypeStruct((bsz, tp, hl), BF16), jax.ShapeDtypeStruct((bsz, tp, hl), BF16),
            jax.ShapeDtypeStruct((bsz, pk, tp), BF16), jax.ShapeDtypeStruct((bsz, tp, pk), BF16),
            jax.ShapeDtypeStruct((bsz, tp, 2 * pk), BF16), jax.ShapeDtypeStruct((bsz, pk, tp), BF16)]
    rowspec = lambda n: pl.BlockSpec((1, tm, n), lambda b, t: (b, t, 0))
    tspec = pl.BlockSpec((1, pk, tm), lambda b, t: (b, 0, t))
    return pl.pallas_call(
        _od_in_kernel,
        out_shape=outs,
        grid=(bsz, nt),
        in_specs=in_specs,
        out_specs=[rowspec(hl), rowspec(hl), tspec, rowspec(pk), rowspec(2 * pk), tspec],
        scratch_shapes=[pltpu.VMEM((1, LANE), F32)],
        compiler_params=_cparams(("parallel", "arbitrary")),
        name="od_in",
    )(*args)


def _attn_kernel(fox, blk, q_ref, k_ref, vt_ref, out_ref, st_ref, acc_ref, *maybe_qe):
    tp = k_ref.shape[1]
    n_blocks = (tp - LANE) // blk
    half = LANE // 2

    def start_qblock(q0, nq):
        st_ref[0:2, 0:nq] = jnp.full((2, nq), NEG, F32)
        st_ref[2:4, 0:nq] = jnp.zeros((2, nq), F32)
        acc_ref[:, :, 0:nq] = jnp.zeros((2, LANE, nq), F32)
        if fox:
            qe_ref = maybe_qe[0]
            q = q_ref[0, pl.ds(q0, nq), :]
            lane = lax.broadcasted_iota(jnp.int32, (nq, LANE), 1)
            zero = jnp.zeros_like(q)
            for hd in range(2):
                mine = (lane >= hd * half) & (lane < (hd + 1) * half)
                ones = ((lane >= 3 * hd) & (lane < 3 * hd + 3)).astype(F32).astype(BF16)
                qe_ref[hd, 0:nq, :] = jnp.concatenate([jnp.where(mine, q, zero), ones], axis=1)

    def tile(q0, nq, k0, nk, diagonal):
        kblk = k_ref[0, pl.ds(k0, nk), :]
        vt = vt_ref[0, :, pl.ds(k0, nk)]
        if diagonal:
            ok = (lax.broadcasted_iota(jnp.int32, (nk, nq), 0)
                  <= lax.broadcasted_iota(jnp.int32, (nk, nq), 1))
        logits = []
        for hd in range(2):
            if fox:
                logits.append(_dot_nt(kblk, maybe_qe[0][hd, 0:nq, :]))
            else:
                logits.append(_dot_nt(kblk[:, hd * LANE:(hd + 1) * LANE],
                                      q_ref[0, pl.ds(q0, nq), hd * LANE:(hd + 1) * LANE]))
        for hd in range(2):
            s = logits[hd]
            if diagonal:
                s = jnp.where(ok, s, NEG)
            m_old = st_ref[hd:hd + 1, 0:nq]
            l_old = st_ref[2 + hd:3 + hd, 0:nq]
            m_new = jnp.maximum(m_old, jnp.max(s, axis=0, keepdims=True))
            alpha = jnp.exp2(m_old - m_new)
            p = jnp.exp2(s - m_new)
            st_ref[hd:hd + 1, 0:nq] = m_new
            st_ref[2 + hd:3 + hd, 0:nq] = alpha * l_old + jnp.sum(p, axis=0, keepdims=True)
            acc_ref[hd, :, 0:nq] = acc_ref[hd, :, 0:nq] * alpha + _dot(vt, p.astype(BF16))

    def finish_qblock(q0, nq):
        o_a = acc_ref[0, 0:half, 0:nq] / st_ref[2:3, 0:nq]
        o_b = acc_ref[1, half:LANE, 0:nq] / st_ref[3:4, 0:nq]
        out_ref[0, pl.ds(q0, nq), :] = jnp.concatenate([o_a, o_b], axis=0).T.astype(out_ref.dtype)

    start_qblock(0, LANE)
    tile(0, LANE, 0, LANE, True)
    finish_qblock(0, LANE)

    def qblock(i, carry):
        q0 = pl.multiple_of(LANE + i * blk, LANE)
        start_qblock(q0, blk)
        tile(q0, blk, 0, LANE, False)

        def kstep(j, c):
            tile(q0, blk, pl.multiple_of(LANE + j * blk, LANE), blk, False)
            return c

        lax.fori_loop(0, i, kstep, 0)
        tile(q0, blk, q0, blk, True)
        finish_qblock(q0, blk)
        return carry

    lax.fori_loop(0, n_blocks, qblock, 0)


def _attention(q, k, vt, fox, blk):
    bsz, tp, _ = q.shape
    npair = vt.shape[1] // LANE
    qw = LANE if fox else 2 * LANE
    scratch = [pltpu.VMEM((SUBLANE, blk), F32), pltpu.VMEM((2, LANE, blk), F32)]
    if fox:
        scratch.append(pltpu.VMEM((2, blk, 2 * LANE), BF16))
    return pl.pallas_call(
        functools.partial(_attn_kernel, fox, blk),
        out_shape=jax.ShapeDtypeStruct((bsz, tp, npair * LANE), BF16),
        grid=(bsz, npair),
        in_specs=[pl.BlockSpec((1, tp, qw), lambda b, p: (b, 0, p)),
                  pl.BlockSpec((1, tp, 2 * LANE), lambda b, p: (b, 0, p)),
                  pl.BlockSpec((1, LANE, tp), lambda b, p: (b, p, 0))],
        out_specs=pl.BlockSpec((1, tp, LANE), lambda b, p: (b, 0, p)),
        scratch_shapes=scratch,
        compiler_params=_cparams(("parallel", "parallel")),
        name="fox_attn" if fox else "mla_attn",
    )(q, k, vt)


def _place_cols(w, width, offset):
    k, n = w.shape
    return jnp.pad(w, ((0, 0), (offset, width - offset - n)))


def _ev_consts(w_in, b_if):
    nq, nv = A_HEADS * A_DQK, A_HEADS * A_DV
    o = 0
    wq = w_in[:, o:o + nq]; o += nq
    wk = w_in[:, o:o + nq]; o += nq
    wv = w_in[:, o:o + nv]; o += nv
    wo = w_in[:, o:o + nv]; o += nv
    wg = w_in[:, o:o + 2 * A_HEADS]; o += 2 * A_HEADS
    wxb = w_in[:, o:o + 1024]; o += 1024
    wgb = w_in[:, o:o + 1024]
    w = jnp.concatenate([wq, wk, wv, wo, _place_cols(wg, LANE, 0), wxb, wgb], axis=1).astype(BF16)
    bif = _place_cols(b_if[None, :], LANE, 0)
    return w, bif


def _head_tile_cols(w, n_heads, per_head, take, offset):
    k = w.shape[0]
    wh = w.reshape(k, n_heads, per_head)[:, :, take[0]:take[0] + take[1]]
    wh = jnp.pad(wh, ((0, 0), (0, 0), (offset, LANE - offset - take[1])))
    return wh.reshape(k, n_heads * LANE)


def _row128(parts):
    r = jnp.zeros((LANE,), F32)
    for off, v in parts:
        r = r.at[off:off + v.shape[0]].set(v.astype(F32))
    return r[None, :]


def _od_consts(p):
    c = {}
    w_in = p["w_in"]
    o = 0
    wcq = w_in[:, o:o + C_Q_LORA]; o += C_Q_LORA
    wckv = w_in[:, o:o + C_KV_LORA]; o += C_KV_LORA
    wkr = w_in[:, o:o + C_ROPE]; o += C_ROPE
    nf = D_HEADS * D_HD
    wfq = w_in[:, o:o + nf]; o += nf
    wfk = w_in[:, o:o + nf]; o += nf
    wfv = w_in[:, o:o + nf]; o += nf
    wff = w_in[:, o:o + D_HEADS]
    c["w"] = jnp.concatenate([wcq, wckv, _place_cols(wkr, LANE, ROPE_LO), wfq, wfk, wfv,
                              _place_cols(wff, LANE, 0)], axis=1).astype(BF16)
    c["ln"] = p["ln"][None, :]
    freqs = ROPE_THETA ** (-jnp.arange(HALF, dtype=F32) / HALF)
    c["frq"] = _row128([(ROPE_LO, freqs), (ROPE_MID, freqs)])
    c["sgn"] = _row128([(ROPE_LO, -jnp.ones((HALF,), F32)), (ROPE_MID, jnp.ones((HALF,), F32))])
    c["gqa"] = p["g_qa"][None, :]
    c["gkva"] = p["g_kva"][None, :]
    dq = C_NOPE + C_ROPE
    wuq, wukv = p["w_uq"], p["w_ukv"]
    c["wq"] = (_head_tile_cols(wuq, C_HEADS, dq, (0, dq), 0)).astype(BF16)
    c["wkn"] = _head_tile_cols(wukv, C_HEADS, C_NOPE + C_V, (0, C_NOPE), 0).astype(BF16)
    c["wv"] = wukv.reshape(-1, C_HEADS, C_NOPE + C_V)[:, :, C_NOPE:].reshape(-1, C_HEADS * C_V).astype(BF16)
    c["gq"] = _row128([(0, p["g_qn"]), (ROPE_LO, p["g_qr"])])
    c["gk"] = _row128([(0, p["g_kn"])])
    c["gkr"] = _row128([(ROPE_LO, p["g_kr"])])
    c["gfq"] = _row128([(0, p["g_fq"]), (D_HD, p["g_fq"])])
    c["gfk"] = _row128([(0, p["g_fk"]), (D_HD, p["g_fk"])])
    c["bf"] = _row128([(0, p["b_f"])])
    seg = np.where(np.arange(LANE) < C_NOPE, 0, np.where(np.arange(LANE) < ROPE_HI, 1, 2))
    c["gm"] = jnp.asarray(seg[:, None] == seg[None, :], F32).astype(BF16)
    seg64 = np.arange(LANE) // D_HD
    c["g64"] = jnp.asarray(seg64[:, None] == seg64[None, :], F32).astype(BF16)
    c["ilm"] = jnp.asarray(np.where(seg == 0, 1.0 / C_NOPE, 1.0 / C_ROPE)[None, :], F32)
    sel = np.zeros((3, LANE, (D_HEADS // 2) * LANE), np.float32)
    for h in range(D_HEADS):
        for part in range(3):
            sel[part, h, (h // 2) * LANE + 3 * (h % 2) + part] = 1.0
    c["sel"] = jnp.asarray(sel).astype(BF16)
    return c


def _time_tile(tp):
    return 3 * LANE if tp % (3 * LANE) == 0 else LANE


def kernel(x, positions, meta, ev_ln, ev_w_in, ev_b_if, ev_a_norm, ev_conv_w, ev_conv_b, ev_w_ra, ev_b_ra, ev_w_rx, ev_b_rx, ev_lam, ev_w_out, od_ln, od_w_in, od_b_f, od_g_qa, od_g_kva, od_w_uq, od_w_ukv, od_g_qn, od_g_qr, od_g_kn, od_g_kr, od_g_fq, od_g_fk, od_w_out, mlp_ln, w_ff1, w_ff2):
    bsz, seq, d = x.shape
    tp = LANE + seq
    r = bsz * tp
    tm = 512 if r % 512 == 0 else LANE
    tf = 1024
    tt = _time_tile(tp)

    h = jnp.concatenate([jnp.zeros((bsz, PAD, d), x.dtype),
                         jnp.broadcast_to(meta.astype(x.dtype), (bsz, N_META, d)), x], axis=1)
    pos_full = jnp.concatenate([jnp.zeros((bsz, PAD), jnp.int32),
                                jnp.broadcast_to(jnp.arange(N_META, dtype=jnp.int32)[None, :], (bsz, N_META)),
                                positions + N_META], axis=1)
    pos_col = pos_full.astype(F32)[:, :, None]
    h2 = h.reshape(r, d)

    e = 0
    w_a, bif = _ev_consts(ev_w_in[e], ev_b_if[e])
    q, k, v, o, g, xb, gb = _ev_in(h2, ev_ln[e][None, :], w_a, tm)
    ha = _mlstm(q, k, v, o, g, bif, ev_a_norm[e][None, :], bsz, tp, LANE)
    hb = _lru(xb, gb, ev_conv_w[e], ev_conv_b[e][None, :], ev_w_ra[e].astype(BF16), ev_b_ra[e][None, :],
              ev_w_rx[e].astype(BF16), ev_b_rx[e][None, :], ev_lam[e][None, :], bsz, tp, tt)
    nv = A_HEADS * A_DV
    wo = ev_w_out[e].astype(BF16)
    h2 = _out_mlp(h2, [ha, hb], [wo[:nv], wo[nv:]], mlp_ln[0][None, :],
                  w_ff1[0].astype(BF16), w_ff2[0].astype(BF16), tm, tf)

    oi = 0
    consts = _od_consts(dict(ln=od_ln[oi], w_in=od_w_in[oi], b_f=od_b_f[oi], g_qa=od_g_qa[oi],
                             g_kva=od_g_kva[oi], w_uq=od_w_uq[oi], w_ukv=od_w_ukv[oi],
                             g_qn=od_g_qn[oi], g_qr=od_g_qr[oi], g_kn=od_g_kn[oi], g_kr=od_g_kr[oi],
                             g_fq=od_g_fq[oi], g_fk=od_g_fk[oi]))
    qm, km, vmt, qf, kf, vft = _od_in(h2.reshape(bsz, tp, d), pos_col, consts, tt)
    blk = 512 if seq % 512 == 0 else LANE
    hc = _attention(qm, km, vmt, False, blk)
    hd = _attention(qf, kf, vft, True, blk)
    nc = C_HEADS * C_V
    wo = od_w_out[oi].astype(BF16)
    h2 = _out_mlp(h2, [hc.reshape(r, nc), hd.reshape(r, nc)], [wo[:nc], wo[nc:]], mlp_ln[1][None, :],
                  w_ff1[1].astype(BF16), w_ff2[1].astype(BF16), tm, tf)
    return h2.reshape(bsz, tp, d)[:, LANE:]
```

```python
import functools
import math

import numpy as np
import jax
import jax.numpy as jnp
from jax import lax
from jax.experimental import pallas as pl
from jax.experimental.pallas import tpu as pltpu

F32 = jnp.float32
BF16 = jnp.bfloat16

N_META = 16
EPS = 1e-6
A_HEADS = 4
A_DQK = 128
A_DV = 256
B_BLOCKS = 8
B_BLOCK = 128
CONV_W = 4
LRU_C = 8.0
C_HEADS = 8
C_Q_LORA = 384
C_KV_LORA = 256
C_NOPE = 64
C_ROPE = 32
C_V = 64
ROPE_THETA = 10000.0
D_HEADS = 8
D_HD = 64

LANE = 128
SUBLANE = 8
VMEM_LIMIT = 56 * 1024 * 1024

PAD = LANE - N_META
NEG = -1e30
LOG2E = math.log2(math.e)

HALF = C_ROPE // 2
ROPE_LO = C_NOPE
ROPE_MID = C_NOPE + HALF
ROPE_HI = C_NOPE + C_ROPE
MASK_LANE = ROPE_HI


def _cparams(sem):
    return pltpu.CompilerParams(dimension_semantics=sem, vmem_limit_bytes=VMEM_LIMIT)


def _dot(a, b):
    return jnp.dot(a, b, preferred_element_type=F32)


def _dot_nt(a, b):
    return lax.dot_general(a, b, (((1,), (1,)), ((), ())), preferred_element_type=F32)


def _split2(x):
    hi = x.astype(BF16)
    lo = (x - hi.astype(F32)).astype(BF16)
    return hi, lo


def _split3(x):
    hi = x.astype(BF16)
    r = x - hi.astype(F32)
    mid = r.astype(BF16)
    lo = (r - mid.astype(F32)).astype(BF16)
    return hi, mid, lo


def _dot_sel_r(x, sel, parts):
    pieces = _split3(x) if parts == 3 else _split2(x)
    acc = _dot(pieces[0], sel)
    for p in pieces[1:]:
        acc = acc + _dot(p, sel)
    return acc


def _dot_sel_l(sel, x, parts):
    pieces = _split3(x) if parts == 3 else _split2(x)
    acc = _dot(sel, pieces[0])
    for p in pieces[1:]:
        acc = acc + _dot(sel, p)
    return acc


def _rms_rows(x, g):
    ms = jnp.mean(x * x, axis=-1, keepdims=True)
    return x * lax.rsqrt(ms + EPS) * g


def _log_sigmoid(x):
    return -(jnp.maximum(-x, 0.0) + jnp.log1p(jnp.exp(-jnp.abs(x))))


def _sigmoid(x):
    return 0.5 * jnp.tanh(0.5 * x) + 0.5


def _tril_bf16(n):
    r = lax.broadcasted_iota(jnp.int32, (n, n), 0)
    c = lax.broadcasted_iota(jnp.int32, (n, n), 1)
    return jnp.where(r >= c, 1.0, 0.0).astype(BF16)


EV_SLABS = (("q", 512, BF16), ("k", 512, BF16), ("v", 1024, BF16), ("o", 1024, BF16),
            ("g", 128, F32), ("xb", 1024, BF16), ("gb", 1024, BF16))


def _ev_in_kernel(h_ref, ln_ref, w_ref, *out_refs):
    xn = _rms_rows(h_ref[...], ln_ref[...]).astype(BF16)
    off = 0
    for (_, width, dt), ref in zip(EV_SLABS, out_refs):
        ref[...] = _dot(xn, w_ref[:, off:off + width]).astype(dt)
        off += width


def _ev_in(h2, ln, w, tm):
    r, d = h2.shape
    n = w.shape[1]
    outs = [jax.ShapeDtypeStruct((r, width), dt) for (_, width, dt) in EV_SLABS]
    return pl.pallas_call(
        _ev_in_kernel,
        out_shape=outs,
        grid=(r // tm,),
        in_specs=[pl.BlockSpec((tm, d), lambda i: (i, 0)),
                  pl.BlockSpec((1, d), lambda i: (0, 0)),
                  pl.BlockSpec((d, n), lambda i: (0, 0))],
        out_specs=[pl.BlockSpec((tm, width), lambda i: (i, 0)) for (_, width, _) in EV_SLABS],
        compiler_params=_cparams(("parallel",)),
        name="ev_in",
    )(h2, ln, w)


def _mlstm_kernel(q_ref, k_ref, v_ref, o_ref, g_ref, bif_ref, an_ref, out_ref, cn_ref, m_ref):
    c = pl.program_id(1)
    L = q_ref.shape[0]
    dv_ext = A_DV + LANE

    @pl.when(c == 0)
    def _():
        cn_ref[...] = jnp.zeros_like(cn_ref)
        m_ref[...] = jnp.zeros_like(m_ref)

    lane = lax.broadcasted_iota(jnp.int32, (L, LANE), 1)
    row = lax.broadcasted_iota(jnp.int32, (L, LANE), 0)
    valid = (row >= PAD) | (c > 0)
    g = g_ref[...] + bif_ref[...]
    li = jnp.where(valid, g, NEG)
    lf = jnp.where(valid, _log_sigmoid(g), 0.0)
    is_f = (lane >= A_HEADS) & (lane < 2 * A_HEADS)
    lf = jnp.where(is_f, lf, 0.0)
    tril = _tril_bf16(L)
    bcum = _dot_sel_l(tril, lf, 3)
    li_t = li.T
    bcum_t = bcum.T

    rr = lax.broadcasted_iota(jnp.int32, (L, L), 0)
    cc = lax.broadcasted_iota(jnp.int32, (L, L), 1)
    causal = rr >= cc
    e0 = jnp.where(lane == 0, 1.0, 0.0).astype(BF16)
    scale = A_DQK ** -0.5

    for h in range(A_HEADS):
        qs = (q_ref[:, h * A_DQK:(h + 1) * A_DQK].astype(F32) * scale).astype(BF16)
        kk = k_ref[:, h * A_DQK:(h + 1) * A_DQK]
        vext = jnp.concatenate([v_ref[:, h * A_DV:(h + 1) * A_DV], e0], axis=1)
        i_col = li[:, h:h + 1]
        b_col = bcum[:, A_HEADS + h:A_HEADS + h + 1]
        i_row = li_t[h:h + 1, :]
        b_row = bcum_t[A_HEADS + h:A_HEADS + h + 1, :]
        m_state = m_ref[:, h:h + 1]
        cn = cn_ref[h]

        dmat = jnp.where(causal, b_col - b_row + i_row, NEG)
        inter = b_col + m_state
        m_row = jnp.maximum(inter, jnp.max(dmat, axis=1, keepdims=True))
        w_intra = jnp.exp(dmat - m_row)
        w_inter = jnp.exp(inter - m_row)
        s = (_dot_nt(qs, kk) * w_intra).astype(BF16)
        numden = w_inter * _dot(qs, cn.astype(BF16)) + _dot(s, vext)
        num = numden[:, :A_DV]
        den = numden[:, A_DV:A_DV + 1]
        hh = num / jnp.maximum(jnp.abs(den), jnp.exp(-m_row))
        hn = _rms_rows(hh, an_ref[...])
        og = _sigmoid(o_ref[:, h * A_DV:(h + 1) * A_DV].astype(F32))
        out_ref[:, h * A_DV:(h + 1) * A_DV] = (hn * og).astype(out_ref.dtype)

        gsum = b_col[L - 1:L, :]
        dk_col = gsum - b_col + i_col
        dk_row = gsum - b_row + i_row
        m_new = jnp.maximum(gsum + m_state, jnp.max(dk_row, axis=1, keepdims=True))
        wk = jnp.exp(dk_col - m_new)
        sc = jnp.exp(gsum + m_state - m_new)
        kw_t = (kk.astype(F32) * wk).T.astype(BF16)
        cn_ref[h] = sc * cn + _dot(kw_t, vext)
        m_ref[:, h:h + 1] = m_new


def _mlstm(q, k, v, o, g, bif, an, bsz, tp, L):
    nq, nv = A_HEADS * A_DQK, A_HEADS * A_DV
    nc = tp // L
    row_map = lambda b, c: (b * nc + c, 0)
    const = lambda b, c: (0, 0)
    return pl.pallas_call(
        _mlstm_kernel,
        out_shape=jax.ShapeDtypeStruct((bsz * tp, nv), BF16),
        grid=(bsz, nc),
        in_specs=[pl.BlockSpec((L, nq), row_map), pl.BlockSpec((L, nq), row_map),
                  pl.BlockSpec((L, nv), row_map), pl.BlockSpec((L, nv), row_map),
                  pl.BlockSpec((L, LANE), row_map),
                  pl.BlockSpec((1, LANE), const), pl.BlockSpec((1, A_DV), const)],
        out_specs=pl.BlockSpec((L, nv), row_map),
        scratch_shapes=[pltpu.VMEM((A_HEADS, A_DQK, A_DV + LANE), F32),
                        pltpu.VMEM((1, LANE), F32)],
        compiler_params=_cparams(("parallel", "arbitrary")),
        name="mlstm",
    )(q, k, v, o, g, bif, an)


def _gelu_tanh(x):
    c0 = math.sqrt(2.0 / math.pi)
    return x * (0.5 * jnp.tanh(x * (c0 + (c0 * 0.044715) * (x * x))) + 0.5)


def _lru_kernel(xb_ref, gb_ref, cw_ref, cb_ref, wra_ref, bra_ref, wrx_ref, brx_ref, lam_ref,
                out_ref, hist_ref, hc_ref, a_s, u_s):
    t = pl.program_id(1)
    tt, w = xb_ref.shape

    @pl.when(t == 0)
    def _():
        hist_ref[...] = jnp.zeros_like(hist_ref)
        hc_ref[...] = jnp.zeros_like(hc_ref)

    row = lax.broadcasted_iota(jnp.int32, (tt, w), 0)
    valid = (row >= PAD) | (t > 0)
    x = jnp.where(valid, xb_ref[...].astype(F32), 0.0)
    xfull = jnp.concatenate([hist_ref[...], x], axis=0)
    conv = cb_ref[...] + cw_ref[CONV_W - 1:CONV_W, :] * x
    for d in range(1, CONV_W):
        shifted = pltpu.roll(xfull, d, 0)[SUBLANE:, :]
        conv = conv + cw_ref[CONV_W - 1 - d:CONV_W - d, :] * shifted
    hist_ref[...] = x[tt - SUBLANE:, :]

    xcb = conv.astype(BF16)
    r_parts, i_parts = [], []
    for blk in range(B_BLOCKS):
        xs = xcb[:, blk * B_BLOCK:(blk + 1) * B_BLOCK]
        r_parts.append(_dot(xs, wra_ref[blk]))
        i_parts.append(_dot(xs, wrx_ref[blk]))
    r = _sigmoid(jnp.concatenate(r_parts, axis=1) + bra_ref[...])
    ig = _sigmoid(jnp.concatenate(i_parts, axis=1) + brx_ref[...])
    nlam = -lam_ref[...]
    softplus = jnp.maximum(nlam, 0.0) + jnp.log1p(jnp.exp(-jnp.abs(nlam)))
    a = jnp.exp2(((-LRU_C * LOG2E) * softplus) * r)
    u = jnp.sqrt(1.0 - a * a) * (ig * conv)
    a_s[...] = a
    u_s[...] = jnp.where(valid, u, 0.0)

    srow = lax.broadcasted_iota(jnp.int32, (SUBLANE, w), 0)

    def group(gi, hprev):
        sl = pl.ds(pl.multiple_of(gi * SUBLANE, SUBLANE), SUBLANE)
        ag = a_s[sl, :]
        ug = u_s[sl, :]
        for d in (1, 2, 4):
            ok = srow >= d
            a_sh = jnp.where(ok, pltpu.roll(ag, d, 0), 1.0)
            u_sh = jnp.where(ok, pltpu.roll(ug, d, 0), 0.0)
            ug = ag * u_sh + ug
            ag = ag * a_sh
        hg = ug + ag * hprev
        u_s[sl, :] = hg
        return jnp.broadcast_to(hg[SUBLANE - 1:SUBLANE, :], (SUBLANE, w))

    hc_ref[...] = lax.fori_loop(0, tt // SUBLANE, group, hc_ref[...])
    out_ref[...] = (u_s[...] * _gelu_tanh(gb_ref[...].astype(F32))).astype(out_ref.dtype)


def _lru(xb, gb, cw, cb, wra, bra, wrx, brx, lam, bsz, tp, tt):
    w = xb.shape[1]
    nt = tp // tt
    row_map = lambda b, t: (b * nt + t, 0)
    const2 = lambda b, t: (0, 0)
    const3 = lambda b, t: (0, 0, 0)
    return pl.pallas_call(
        _lru_kernel,
        out_shape=jax.ShapeDtypeStruct(xb.shape, BF16),
        grid=(bsz, nt),
        in_specs=[pl.BlockSpec((tt, w), row_map), pl.BlockSpec((tt, w), row_map),
                  pl.BlockSpec((CONV_W, w), const2), pl.BlockSpec((1, w), const2),
                  pl.BlockSpec((B_BLOCKS, B_BLOCK, B_BLOCK), const3), pl.BlockSpec((1, w), const2),
                  pl.BlockSpec((B_BLOCKS, B_BLOCK, B_BLOCK), const3), pl.BlockSpec((1, w), const2),
                  pl.BlockSpec((1, w), const2)],
        out_specs=pl.BlockSpec((tt, w), row_map),
        scratch_shapes=[pltpu.VMEM((SUBLANE, w), F32), pltpu.VMEM((SUBLANE, w), F32),
                        pltpu.VMEM((tt, w), F32), pltpu.VMEM((tt, w), F32)],
        compiler_params=_cparams(("parallel", "arbitrary")),
        name="conv_rglru",
    )(xb, gb, cw, cb, wra, bra, wrx, brx, lam)


def _out_mlp_kernel(n_mix, h_ref, *refs):
    mix_refs = refs[:n_mix]
    wo_refs = refs[n_mix:2 * n_mix]
    ln_ref, w1_ref, w2_ref, out_ref, res_s, hn_s, acc_s = refs[2 * n_mix:]
    j = pl.program_id(1)

    @pl.when(j == 0)
    def _():
        res = h_ref[...]
        for m_ref, w_ref in zip(mix_refs, wo_refs):
            res = res + _dot(m_ref[...], w_ref[...])
        res_s[...] = res
        hn_s[...] = _rms_rows(res, ln_ref[...]).astype(BF16)
        acc_s[...] = jnp.zeros_like(acc_s)

    a = jnp.maximum(_dot(hn_s[...], w1_ref[...]), 0.0)
    acc_s[...] += _dot((a * a).astype(BF16), w2_ref[...])

    @pl.when(j == pl.num_programs(1) - 1)
    def _():
        out_ref[...] = res_s[...] + acc_s[...]


def _out_mlp(h2, mixes, wouts, ln, w1, w2, tm, tf):
    r, d = h2.shape
    f = w1.shape[1]
    n_mix = len(mixes)
    in_specs = [pl.BlockSpec((tm, d), lambda i, j: (i, 0))]
    in_specs += [pl.BlockSpec((tm, m.shape[1]), lambda i, j: (i, 0)) for m in mixes]
    in_specs += [pl.BlockSpec(w.shape, lambda i, j: (0, 0)) for w in wouts]
    in_specs += [pl.BlockSpec((1, d), lambda i, j: (0, 0)),
                 pl.BlockSpec((d, tf), lambda i, j: (0, j)),
                 pl.BlockSpec((tf, d), lambda i, j: (j, 0))]
    return pl.pallas_call(
        functools.partial(_out_mlp_kernel, n_mix),
        out_shape=jax.ShapeDtypeStruct((r, d), F32),
        grid=(r // tm, f // tf),
        in_specs=in_specs,
        out_specs=pl.BlockSpec((tm, d), lambda i, j: (i, 0)),
        scratch_shapes=[pltpu.VMEM((tm, d), F32), pltpu.VMEM((tm, d), BF16), pltpu.VMEM((tm, d), F32)],
        compiler_params=_cparams(("parallel", "arbitrary")),
        name="out_mlp",
    )(h2, *mixes, *wouts, ln, w1, w2)


OD_COLS = dict(cq=(0, 384), ckv=(384, 640), kr=(640, 768), fq=(768, 1280), fk=(1280, 1792),
               fv=(1792, 2304), ff=(2304, 2432))
OD_N = 2432


def _od_in_kernel(h_ref, ln_ref, w_ref, pos_ref, frq_ref, sgn_ref, gqa_ref, gkva_ref,
                  wq_ref, wkn_ref, wv_ref, gq_ref, gk_ref, gkr_ref, gfq_ref, gfk_ref, bf_ref,
                  gm_ref, g64_ref, ilm_ref, sel_ref,
                  qm_ref, km_ref, vmt_ref, qf_ref, kf_ref, vft_ref, fc_ref):
    t = pl.program_id(1)
    tm = h_ref.shape[1]

    @pl.when(t == 0)
    def _():
        fc_ref[...] = jnp.zeros_like(fc_ref)

    xn = _rms_rows(h_ref[0], ln_ref[...]).astype(BF16)

    def proj(name):
        a, b = OD_COLS[name]
        return _dot(xn, w_ref[:, a:b])

    wide = 2 * LANE
    lane2 = lax.broadcasted_iota(jnp.int32, (tm, wide), 1) % LANE
    row2 = lax.broadcasted_iota(jnp.int32, (tm, wide), 0)
    unused_row2 = (row2 < PAD) & (t == 0)
    mask_lane = lane2 == MASK_LANE
    ang = pos_ref[0] * frq_ref[...]
    cs = jnp.cos(ang)
    sn = jnp.sin(ang) * sgn_ref[...]
    cs = jnp.concatenate([cs, cs], axis=1)
    sn = jnp.concatenate([sn, sn], axis=1)
    first_half = lane2 < ROPE_MID

    def seg_rms(x, sel, inv_len, gain):
        ss = _dot((x * x).astype(BF16), sel)
        return x * lax.rsqrt(ss * inv_len + EPS) * gain

    def rope(y):
        sw = jnp.where(first_half, pltpu.roll(y, wide - HALF, 1), pltpu.roll(y, HALF, 1))
        return y * cs + sw * sn

    gm = gm_ref[...]
    g64 = g64_ref[...]
    ilm = ilm_ref[...]
    inv64 = 1.0 / D_HD

    cqn = _rms_rows(proj("cq"), gqa_ref[...]).astype(BF16)
    qall = _dot(cqn, wq_ref[...])
    scale_m = (C_NOPE + C_ROPE) ** -0.5 * LOG2E
    for h2 in range(C_HEADS // 2):
        cols = slice(h2 * wide, (h2 + 1) * wide)
        y = rope(seg_rms(qall[:, cols], gm, ilm, gq_ref[...]))
        qm_ref[0, :, cols] = jnp.where(mask_lane, 1.0, y * scale_m).astype(BF16)
    ckvn = _rms_rows(proj("ckv"), gkva_ref[...]).astype(BF16)
    knall = _dot(ckvn, wkn_ref[...])
    kr = proj("kr")
    krt = rope(seg_rms(jnp.concatenate([kr, kr], axis=1), gm, ilm, gkr_ref[...]))
    krt = jnp.where(mask_lane & unused_row2, NEG, krt)
    for h2 in range(C_HEADS // 2):
        cols = slice(h2 * wide, (h2 + 1) * wide)
        y = seg_rms(knall[:, cols], gm, ilm, gk_ref[...]) + krt
        km_ref[0, :, cols] = y.astype(BF16)
    vmt_ref[0] = _dot(ckvn, wv_ref[...]).T.astype(BF16)

    fq = proj("fq")
    fk = proj("fk")
    scale_f = D_HD ** -0.5 * LOG2E
    npair = D_HEADS // 2
    for p2 in range(npair // 2):
        cols = slice(p2 * wide, (p2 + 1) * wide)
        y = seg_rms(fq[:, cols], g64, inv64, gfq_ref[...])
        qf_ref[0, :, cols] = (y * scale_f).astype(BF16)
        y = seg_rms(fk[:, cols], g64, inv64, gfk_ref[...]).astype(BF16)
        for e in range(2):
            p = 2 * p2 + e
            kf_ref[0, :, 2 * p * LANE:(2 * p + 1) * LANE] = y[:, e * LANE:(e + 1) * LANE]
    vft_ref[0] = proj("fv").T.astype(BF16)

    lane = lax.broadcasted_iota(jnp.int32, (tm, LANE), 1)
    unused_row = (lax.broadcasted_iota(jnp.int32, (tm, LANE), 0) < PAD) & (t == 0)
    valid = jnp.logical_not(unused_row) & (lane < D_HEADS)
    lf = jnp.where(valid, _log_sigmoid(proj("ff") + bf_ref[...]), 0.0)
    fc = _dot_sel_l(_tril_bf16(tm), lf, 3) + fc_ref[...]
    fc_ref[...] = fc[tm - 1:tm, :]
    hi, mid, lo = _split3(jnp.where(unused_row, NEG, fc * (-LOG2E)))
    bias = _dot(hi, sel_ref[0]) + _dot(mid, sel_ref[1]) + _dot(lo, sel_ref[2])
    for p in range(npair):
        kf_ref[0, :, (2 * p + 1) * LANE:(2 * p + 2) * LANE] = bias[:, p * LANE:(p + 1) * LANE].astype(BF16)


def _od_in(h3, pos_col, consts, tm):
    bsz, tp, d = h3.shape
    nt = tp // tm
    names = ["ln", "w", None, "frq", "sgn", "gqa", "gkva", "wq", "wkn", "wv", "gq", "gk", "gkr",
             "gfq", "gfk", "bf", "gm", "g64", "ilm", "sel"]
    args, in_specs = [h3], [pl.BlockSpec((1, tm, d), lambda b, t: (b, t, 0))]
    for nm in names:
        if nm is None:
            args.append(pos_col)
            in_specs.append(pl.BlockSpec((1, tm, 1), lambda b, t: (b, t, 0)))
            continue
        a = consts[nm]
        args.append(a)
        in_specs.append(pl.BlockSpec(a.shape, (lambda nd: (lambda b, t: (0,) * nd))(a.ndim)))
    hl = C_HEADS * LANE
    pk = D_HEADS * D_HD
    outs = [jax.ShapeDtypeStruct((bsz, tp, hl), BF16), jax.ShapeDtypeStruct((bsz, tp, hl), BF16),
            jax.ShapeDtypeStruct((bsz, pk, tp), BF16), jax.ShapeDtypeStruct((bsz, tp, pk), BF16),
            jax.ShapeDtypeStruct((bsz, tp, 2 * pk), BF16), jax.ShapeDtypeStruct((bsz, pk, tp), BF16)]
    rowspec = lambda n: pl.BlockSpec((1, tm, n), lambda b, t: (b, t, 0))
    tspec = pl.BlockSpec((1, pk, tm), lambda b, t: (b, 0, t))
    return pl.pallas_call(
        _od_in_kernel,
        out_shape=outs,
        grid=(bsz, nt),
        in_specs=in_specs,
        out_specs=[rowspec(hl), rowspec(hl), tspec, rowspec(pk), rowspec(2 * pk), tspec],
        scratch_shapes=[pltpu.VMEM((1, LANE), F32)],
        compiler_params=_cparams(("parallel", "arbitrary")),
        name="od_in",
    )(*args)


def _attn_kernel(fox, blk, q_ref, k_ref, vt_ref, out_ref, st_ref, acc_ref, buf0, buf1, bufm,
                 mx0, mx1, mxm, *maybe_qe):
    tp = k_ref.shape[1]
    n_blocks = (tp - LANE) // blk
    half = LANE // 2

    def start_qblock(q0, nq):
        st_ref[0:2, 0:nq] = jnp.full((2, nq), NEG, F32)
        st_ref[2:4, 0:nq] = jnp.zeros((2, nq), F32)
        acc_ref[:, :, 0:nq] = jnp.zeros((2, LANE, nq), F32)
        if fox:
            qe_ref = maybe_qe[0]
            q = q_ref[0, pl.ds(q0, nq), :]
            lane = lax.broadcasted_iota(jnp.int32, (nq, LANE), 1)
            zero = jnp.zeros_like(q)
            for hd in range(2):
                mine = (lane >= hd * half) & (lane < (hd + 1) * half)
                ones = ((lane >= 3 * hd) & (lane < 3 * hd + 3)).astype(F32).astype(BF16)
                qe_ref[hd, 0:nq, :] = jnp.concatenate([jnp.where(mine, q, zero), ones], axis=1)

    def logits_to(buf, mx, q0, nq, k0, nk, diagonal):
        kblk = k_ref[0, pl.ds(k0, nk), :]
        if diagonal:
            ok = (lax.broadcasted_iota(jnp.int32, (nk, nq), 0)
                  <= lax.broadcasted_iota(jnp.int32, (nk, nq), 1))
        for hd in range(2):
            if fox:
                s = _dot_nt(kblk, maybe_qe[0][hd, 0:nq, :])
            else:
                s = _dot_nt(kblk[:, hd * LANE:(hd + 1) * LANE],
                            q_ref[0, pl.ds(q0, nq), hd * LANE:(hd + 1) * LANE])
            if diagonal:
                s = jnp.where(ok, s, NEG)
            buf[hd, 0:nk, 0:nq] = s
            mx[hd:hd + 1, 0:nq] = jnp.max(s, axis=0, keepdims=True)

    def consume(buf, mx, nq, k0, nk):
        vt = vt_ref[0, :, pl.ds(k0, nk)]
        for hd in range(2):
            m_old = st_ref[hd:hd + 1, 0:nq]
            l_old = st_ref[2 + hd:3 + hd, 0:nq]
            m_new = jnp.maximum(m_old, mx[hd:hd + 1, 0:nq])
            alpha = jnp.exp2(m_old - m_new)
            p = jnp.exp2(buf[hd, 0:nk, 0:nq] - m_new)
            st_ref[hd:hd + 1, 0:nq] = m_new
            st_ref[2 + hd:3 + hd, 0:nq] = alpha * l_old + jnp.sum(p, axis=0, keepdims=True)
            acc_ref[hd, :, 0:nq] = acc_ref[hd, :, 0:nq] * alpha + _dot(vt, p.astype(BF16))

    def finish_qblock(q0, nq):
        o_a = acc_ref[0, 0:half, 0:nq] / st_ref[2:3, 0:nq]
        o_b = acc_ref[1, half:LANE, 0:nq] / st_ref[3:4, 0:nq]
        out_ref[0, pl.ds(q0, nq), :] = jnp.concatenate([o_a, o_b], axis=0).T.astype(out_ref.dtype)

    start_qblock(0, LANE)
    logits_to(bufm, mxm, 0, LANE, 0, LANE, True)
    consume(bufm, mxm, LANE, 0, LANE)
    finish_qblock(0, LANE)

    def qblock(i, carry):
        q0 = pl.multiple_of(LANE + i * blk, LANE)

        def k_of(n):
            return pl.multiple_of(LANE + n * blk, LANE)

        start_qblock(q0, blk)
        logits_to(bufm, mxm, q0, blk, 0, LANE, False)

        @pl.when(i == 0)
        def _():
            logits_to(buf1, mx1, q0, blk, q0, blk, True)
            consume(bufm, mxm, blk, 0, LANE)
            consume(buf1, mx1, blk, q0, blk)

        @pl.when(i > 0)
        def _():
            logits_to(buf0, mx0, q0, blk, k_of(0), blk, False)
            consume(bufm, mxm, blk, 0, LANE)

            def pair(m, c):
                logits_to(buf1, mx1, q0, blk, k_of(2 * m + 1), blk, False)
                consume(buf0, mx0, blk, k_of(2 * m), blk)
                logits_to(buf0, mx0, q0, blk, k_of(2 * m + 2), blk, False)
                consume(buf1, mx1, blk, k_of(2 * m + 1), blk)
                return c

            n_pairs = (i - 1) // 2
            lax.fori_loop(0, n_pairs, pair, 0)
            held = 2 * n_pairs

            @pl.when(i % 2 == 1)
            def _():
                logits_to(buf1, mx1, q0, blk, q0, blk, True)
                consume(buf0, mx0, blk, k_of(held), blk)
                consume(buf1, mx1, blk, q0, blk)

            @pl.when(i % 2 == 0)
            def _():
                logits_to(buf1, mx1, q0, blk, k_of(held + 1), blk, False)
                consume(buf0, mx0, blk, k_of(held), blk)
                logits_to(buf0, mx0, q0, blk, q0, blk, True)
                consume(buf1, mx1, blk, k_of(held + 1), blk)
                consume(buf0, mx0, blk, q0, blk)

        finish_qblock(q0, blk)
        return carry

    lax.fori_loop(0, n_blocks, qblock, 0)


def _attention(q, k, vt, fox, blk):
    bsz, tp, _ = q.shape
    npair = vt.shape[1] // LANE
    qw = LANE if fox else 2 * LANE
    scratch = [pltpu.VMEM((SUBLANE, blk), F32), pltpu.VMEM((2, LANE, blk), F32),
               pltpu.VMEM((2, blk, blk), F32), pltpu.VMEM((2, blk, blk), F32),
               pltpu.VMEM((2, LANE, blk), F32),
               pltpu.VMEM((SUBLANE, blk), F32), pltpu.VMEM((SUBLANE, blk), F32),
               pltpu.VMEM((SUBLANE, blk), F32)]
    if fox:
        scratch.append(pltpu.VMEM((2, blk, 2 * LANE), BF16))
    return pl.pallas_call(
        functools.partial(_attn_kernel, fox, blk),
        out_shape=jax.ShapeDtypeStruct((bsz, tp, npair * LANE), BF16),
        grid=(bsz, npair),
        in_specs=[pl.BlockSpec((1, tp, qw), lambda b, p: (b, 0, p)),
                  pl.BlockSpec((1, tp, 2 * LANE), lambda b, p: (b, 0, p)),
                  pl.BlockSpec((1, LANE, tp), lambda b, p: (b, p, 0))],
        out_specs=pl.BlockSpec((1, tp, LANE), lambda b, p: (b, 0, p)),
        scratch_shapes=scratch,
        compiler_params=_cparams(("parallel", "parallel")),
        name="fox_attn" if fox else "mla_attn",
    )(q, k, vt)


def _place_cols(w, width, offset):
    k, n = w.shape
    return jnp.pad(w, ((0, 0), (offset, width - offset - n)))


def _ev_consts(w_in, b_if):
    nq, nv = A_HEADS * A_DQK, A_HEADS * A_DV
    o = 0
    wq = w_in[:, o:o + nq]; o += nq
    wk = w_in[:, o:o + nq]; o += nq
    wv = w_in[:, o:o + nv]; o += nv
    wo = w_in[:, o:o + nv]; o += nv
    wg = w_in[:, o:o + 2 * A_HEADS]; o += 2 * A_HEADS
    wxb = w_in[:, o:o + 1024]; o += 1024
    wgb = w_in[:, o:o + 1024]
    w = jnp.concatenate([wq, wk, wv, wo, _place_cols(wg, LANE, 0), wxb, wgb], axis=1).astype(BF16)
    bif = _place_cols(b_if[None, :], LANE, 0)
    return w, bif


def _head_tile_cols(w, n_heads, per_head, take, offset):
    k = w.shape[0]
    wh = w.reshape(k, n_heads, per_head)[:, :, take[0]:take[0] + take[1]]
    wh = jnp.pad(wh, ((0, 0), (0, 0), (offset, LANE - offset - take[1])))
    return wh.reshape(k, n_heads * LANE)


def _row128(parts):
    r = jnp.zeros((LANE,), F32)
    for off, v in parts:
        r = r.at[off:off + v.shape[0]].set(v.astype(F32))
    return r[None, :]


def _od_consts(p):
    c = {}
    w_in = p["w_in"]
    o = 0
    wcq = w_in[:, o:o + C_Q_LORA]; o += C_Q_LORA
    wckv = w_in[:, o:o + C_KV_LORA]; o += C_KV_LORA
    wkr = w_in[:, o:o + C_ROPE]; o += C_ROPE
    nf = D_HEADS * D_HD
    wfq = w_in[:, o:o + nf]; o += nf
    wfk = w_in[:, o:o + nf]; o += nf
    wfv = w_in[:, o:o + nf]; o += nf
    wff = w_in[:, o:o + D_HEADS]
    c["w"] = jnp.concatenate([wcq, wckv, _place_cols(wkr, LANE, ROPE_LO), wfq, wfk, wfv,
                              _place_cols(wff, LANE, 0)], axis=1).astype(BF16)
    c["ln"] = p["ln"][None, :]
    freqs = ROPE_THETA ** (-jnp.arange(HALF, dtype=F32) / HALF)
    c["frq"] = _row128([(ROPE_LO, freqs), (ROPE_MID, freqs)])
    c["sgn"] = _row128([(ROPE_LO, -jnp.ones((HALF,), F32)), (ROPE_MID, jnp.ones((HALF,), F32))])
    c["gqa"] = p["g_qa"][None, :]
    c["gkva"] = p["g_kva"][None, :]
    dq = C_NOPE + C_ROPE
    wuq, wukv = p["w_uq"], p["w_ukv"]
    c["wq"] = (_head_tile_cols(wuq, C_HEADS, dq, (0, dq), 0)).astype(BF16)
    c["wkn"] = _head_tile_cols(wukv, C_HEADS, C_NOPE + C_V, (0, C_NOPE), 0).astype(BF16)
    c["wv"] = wukv.reshape(-1, C_HEADS, C_NOPE + C_V)[:, :, C_NOPE:].reshape(-1, C_HEADS * C_V).astype(BF16)
    c["gq"] = _row128([(0, p["g_qn"]), (ROPE_LO, p["g_qr"])])
    c["gk"] = _row128([(0, p["g_kn"])])
    c["gkr"] = _row128([(ROPE_LO, p["g_kr"])])
    c["gfq"] = _row128([(0, p["g_fq"]), (D_HD, p["g_fq"])])
    c["gfk"] = _row128([(0, p["g_fk"]), (D_HD, p["g_fk"])])
    c["bf"] = _row128([(0, p["b_f"])])
    seg = np.where(np.arange(LANE) < C_NOPE, 0, np.where(np.arange(LANE) < ROPE_HI, 1, 2))
    c["gm"] = jnp.asarray(seg[:, None] == seg[None, :], F32).astype(BF16)
    seg64 = np.arange(LANE) // D_HD
    c["g64"] = jnp.asarray(seg64[:, None] == seg64[None, :], F32).astype(BF16)
    c["ilm"] = jnp.asarray(np.where(seg == 0, 1.0 / C_NOPE, 1.0 / C_ROPE)[None, :], F32)
    sel = np.zeros((3, LANE, (D_HEADS // 2) * LANE), np.float32)
    for h in range(D_HEADS):
        for part in range(3):
            sel[part, h, (h // 2) * LANE + 3 * (h % 2) + part] = 1.0
    c["sel"] = jnp.asarray(sel).astype(BF16)
    for nm in ("gq", "gk", "gkr", "gfq", "gfk", "ilm"):
        c[nm] = jnp.tile(c[nm], (1, 2))
    for nm in ("gm", "g64"):
        z = jnp.zeros_like(c[nm])
        c[nm] = jnp.block([[c[nm], z], [z, c[nm]]])
    return c


def _time_tile(tp):
    return 3 * LANE if tp % (3 * LANE) == 0 else LANE


def kernel(x, positions, meta, ev_ln, ev_w_in, ev_b_if, ev_a_norm, ev_conv_w, ev_conv_b, ev_w_ra, ev_b_ra, ev_w_rx, ev_b_rx, ev_lam, ev_w_out, od_ln, od_w_in, od_b_f, od_g_qa, od_g_kva, od_w_uq, od_w_ukv, od_g_qn, od_g_qr, od_g_kn, od_g_kr, od_g_fq, od_g_fk, od_w_out, mlp_ln, w_ff1, w_ff2):
    bsz, seq, d = x.shape
    tp = LANE + seq
    r = bsz * tp
    tm = 512 if r % 512 == 0 else LANE
    tf = 1024
    tt = _time_tile(tp)

    h = jnp.concatenate([jnp.zeros((bsz, PAD, d), x.dtype),
                         jnp.broadcast_to(meta.astype(x.dtype), (bsz, N_META, d)), x], axis=1)
    pos_full = jnp.concatenate([jnp.zeros((bsz, PAD), jnp.int32),
                                jnp.broadcast_to(jnp.arange(N_META, dtype=jnp.int32)[None, :], (bsz, N_META)),
                                positions + N_META], axis=1)
    pos_col = pos_full.astype(F32)[:, :, None]
    h2 = h.reshape(r, d)

    e = 0
    w_a, bif = _ev_consts(ev_w_in[e], ev_b_if[e])
    q, k, v, o, g, xb, gb = _ev_in(h2, ev_ln[e][None, :], w_a, tm)
    ha = _mlstm(q, k, v, o, g, bif, ev_a_norm[e][None, :], bsz, tp, LANE)
    hb = _lru(xb, gb, ev_conv_w[e], ev_conv_b[e][None, :], ev_w_ra[e].astype(BF16), ev_b_ra[e][None, :],
              ev_w_rx[e].astype(BF16), ev_b_rx[e][None, :], ev_lam[e][None, :], bsz, tp, tt)
    nv = A_HEADS * A_DV
    wo = ev_w_out[e].astype(BF16)
    h2 = _out_mlp(h2, [ha, hb], [wo[:nv], wo[nv:]], mlp_ln[0][None, :],
                  w_ff1[0].astype(BF16), w_ff2[0].astype(BF16), tm, tf)

    oi = 0
    consts = _od_consts(dict(ln=od_ln[oi], w_in=od_w_in[oi], b_f=od_b_f[oi], g_qa=od_g_qa[oi],
                             g_kva=od_g_kva[oi], w_uq=od_w_uq[oi], w_ukv=od_w_ukv[oi],
                             g_qn=od_g_qn[oi], g_qr=od_g_qr[oi], g_kn=od_g_kn[oi], g_kr=od_g_kr[oi],
                             g_fq=od_g_fq[oi], g_fk=od_g_fk[oi]))
    qm, km, vmt, qf, kf, vft = _od_in(h2.reshape(bsz, tp, d), pos_col, consts, tt)
    blk = 512 if seq % 512 == 0 else LANE
    hc = _attention(qm, km, vmt, False, blk)
    hd = _attention(qf, kf, vft, True, blk)
    nc = C_HEADS * C_V
    wo = od_w_out[oi].astype(BF16)
    h2 = _out_mlp(h2, [hc.reshape(r, nc), hd.reshape(r, nc)], [wo[:nc], wo[nc:]], mlp_ln[1][None, :],
                  w_ff1[1].astype(BF16), w_ff2[1].astype(BF16), tm, tf)
    return h2.reshape(bsz, tp, d)[:, LANE:]
```

```python
import functools
import math

import numpy as np
import jax
import jax.numpy as jnp
from jax import lax
from jax.experimental import pallas as pl
from jax.experimental.pallas import tpu as pltpu

F32 = jnp.float32
BF16 = jnp.bfloat16

N_META = 16
EPS = 1e-6
A_HEADS = 4
A_DQK = 128
A_DV = 256
B_BLOCKS = 8
B_BLOCK = 128
CONV_W = 4
LRU_C = 8.0
C_HEADS = 8
C_Q_LORA = 384
C_KV_LORA = 256
C_NOPE = 64
C_ROPE = 32
C_V = 64
ROPE_THETA = 10000.0
D_HEADS = 8
D_HD = 64

LANE = 128
SUBLANE = 8
VMEM_LIMIT = 56 * 1024 * 1024

PAD = LANE - N_META
NEG = -1e30
LOG2E = math.log2(math.e)

HALF = C_ROPE // 2
ROPE_LO = C_NOPE
ROPE_MID = C_NOPE + HALF
ROPE_HI = C_NOPE + C_ROPE
MASK_LANE = ROPE_HI


def _cparams(sem):
    return pltpu.CompilerParams(dimension_semantics=sem, vmem_limit_bytes=VMEM_LIMIT)


def _dot(a, b):
    return jnp.dot(a, b, preferred_element_type=F32)


def _dot_nt(a, b):
    return lax.dot_general(a, b, (((1,), (1,)), ((), ())), preferred_element_type=F32)


def _split2(x):
    hi = x.astype(BF16)
    lo = (x - hi.astype(F32)).astype(BF16)
    return hi, lo


def _split3(x):
    hi = x.astype(BF16)
    r = x - hi.astype(F32)
    mid = r.astype(BF16)
    lo = (r - mid.astype(F32)).astype(BF16)
    return hi, mid, lo


def _dot_sel_r(x, sel, parts):
    pieces = _split3(x) if parts == 3 else _split2(x)
    acc = _dot(pieces[0], sel)
    for p in pieces[1:]:
        acc = acc + _dot(p, sel)
    return acc


def _dot_sel_l(sel, x, parts):
    pieces = _split3(x) if parts == 3 else _split2(x)
    acc = _dot(sel, pieces[0])
    for p in pieces[1:]:
        acc = acc + _dot(sel, p)
    return acc


def _rms_rows(x, g):
    ms = jnp.mean(x * x, axis=-1, keepdims=True)
    return x * lax.rsqrt(ms + EPS) * g


def _log_sigmoid(x):
    return -(jnp.maximum(-x, 0.0) + jnp.log1p(jnp.exp(-jnp.abs(x))))


def _sigmoid(x):
    return 0.5 * jnp.tanh(0.5 * x) + 0.5


def _tril_bf16(n):
    r = lax.broadcasted_iota(jnp.int32, (n, n), 0)
    c = lax.broadcasted_iota(jnp.int32, (n, n), 1)
    return jnp.where(r >= c, 1.0, 0.0).astype(BF16)


EV_SLABS = (("q", 512, BF16), ("k", 512, BF16), ("v", 1024, BF16), ("o", 1024, BF16),
            ("g", 128, F32), ("xb", 1024, BF16), ("gb", 1024, BF16))


def _ev_in_kernel(h_ref, ln_ref, w_ref, *out_refs):
    xn = _rms_rows(h_ref[...], ln_ref[...]).astype(BF16)
    off = 0
    for (_, width, dt), ref in zip(EV_SLABS, out_refs):
        ref[...] = _dot(xn, w_ref[:, off:off + width]).astype(dt)
        off += width


def _ev_in(h2, ln, w, tm):
    r, d = h2.shape
    n = w.shape[1]
    outs = [jax.ShapeDtypeStruct((r, width), dt) for (_, width, dt) in EV_SLABS]
    return pl.pallas_call(
        _ev_in_kernel,
        out_shape=outs,
        grid=(r // tm,),
        in_specs=[pl.BlockSpec((tm, d), lambda i: (i, 0)),
                  pl.BlockSpec((1, d), lambda i: (0, 0)),
                  pl.BlockSpec((d, n), lambda i: (0, 0))],
        out_specs=[pl.BlockSpec((tm, width), lambda i: (i, 0)) for (_, width, _) in EV_SLABS],
        compiler_params=_cparams(("parallel",)),
        name="ev_in",
    )(h2, ln, w)


def _mlstm_kernel(q_ref, k_ref, v_ref, o_ref, g_ref, bif_ref, an_ref, out_ref, cn_ref, m_ref):
    c = pl.program_id(1)
    L = q_ref.shape[0]
    dv_ext = A_DV + LANE

    @pl.when(c == 0)
    def _():
        cn_ref[...] = jnp.zeros_like(cn_ref)
        m_ref[...] = jnp.zeros_like(m_ref)

    lane = lax.broadcasted_iota(jnp.int32, (L, LANE), 1)
    row = lax.broadcasted_iota(jnp.int32, (L, LANE), 0)
    valid = (row >= PAD) | (c > 0)
    g = g_ref[...] + bif_ref[...]
    li = jnp.where(valid, g, NEG)
    lf = jnp.where(valid, _log_sigmoid(g), 0.0)
    is_f = (lane >= A_HEADS) & (lane < 2 * A_HEADS)
    lf = jnp.where(is_f, lf, 0.0)
    tril = _tril_bf16(L)
    bcum = _dot_sel_l(tril, lf, 3)
    li_t = li.T
    bcum_t = bcum.T

    rr = lax.broadcasted_iota(jnp.int32, (L, L), 0)
    cc = lax.broadcasted_iota(jnp.int32, (L, L), 1)
    causal = rr >= cc
    e0 = jnp.where(lane == 0, 1.0, 0.0).astype(BF16)
    scale = A_DQK ** -0.5

    for h in range(A_HEADS):
        qs = (q_ref[:, h * A_DQK:(h + 1) * A_DQK].astype(F32) * scale).astype(BF16)
        kk = k_ref[:, h * A_DQK:(h + 1) * A_DQK]
        vext = jnp.concatenate([v_ref[:, h * A_DV:(h + 1) * A_DV], e0], axis=1)
        i_col = li[:, h:h + 1]
        b_col = bcum[:, A_HEADS + h:A_HEADS + h + 1]
        i_row = li_t[h:h + 1, :]
        b_row = bcum_t[A_HEADS + h:A_HEADS + h + 1, :]
        m_state = m_ref[:, h:h + 1]
        cn = cn_ref[h]

        dmat = jnp.where(causal, b_col - b_row + i_row, NEG)
        inter = b_col + m_state
        m_row = jnp.maximum(inter, jnp.max(dmat, axis=1, keepdims=True))
        w_intra = jnp.exp(dmat - m_row)
        w_inter = jnp.exp(inter - m_row)
        s = (_dot_nt(qs, kk) * w_intra).astype(BF16)
        numden = w_inter * _dot(qs, cn.astype(BF16)) + _dot(s, vext)
        num = numden[:, :A_DV]
        den = numden[:, A_DV:A_DV + 1]
        hh = num / jnp.maximum(jnp.abs(den), jnp.exp(-m_row))
        hn = _rms_rows(hh, an_ref[...])
        og = _sigmoid(o_ref[:, h * A_DV:(h + 1) * A_DV].astype(F32))
        out_ref[:, h * A_DV:(h + 1) * A_DV] = (hn * og).astype(out_ref.dtype)

        gsum = b_col[L - 1:L, :]
        dk_col = gsum - b_col + i_col
        dk_row = gsum - b_row + i_row
        m_new = jnp.maximum(gsum + m_state, jnp.max(dk_row, axis=1, keepdims=True))
        wk = jnp.exp(dk_col - m_new)
        sc = jnp.exp(gsum + m_state - m_new)
        kw_t = (kk.astype(F32) * wk).T.astype(BF16)
        cn_ref[h] = sc * cn + _dot(kw_t, vext)
        m_ref[:, h:h + 1] = m_new


def _mlstm(q, k, v, o, g, bif, an, bsz, tp, L):
    nq, nv = A_HEADS * A_DQK, A_HEADS * A_DV
    nc = tp // L
    row_map = lambda b, c: (b * nc + c, 0)
    const = lambda b, c: (0, 0)
    return pl.pallas_call(
        _mlstm_kernel,
        out_shape=jax.ShapeDtypeStruct((bsz * tp, nv), BF16),
        grid=(bsz, nc),
        in_specs=[pl.BlockSpec((L, nq), row_map), pl.BlockSpec((L, nq), row_map),
                  pl.BlockSpec((L, nv), row_map), pl.BlockSpec((L, nv), row_map),
                  pl.BlockSpec((L, LANE), row_map),
                  pl.BlockSpec((1, LANE), const), pl.BlockSpec((1, A_DV), const)],
        out_specs=pl.BlockSpec((L, nv), row_map),
        scratch_shapes=[pltpu.VMEM((A_HEADS, A_DQK, A_DV + LANE), F32),
                        pltpu.VMEM((1, LANE), F32)],
        compiler_params=_cparams(("parallel", "arbitrary")),
        name="mlstm",
    )(q, k, v, o, g, bif, an)


def _gelu_tanh(x):
    c0 = math.sqrt(2.0 / math.pi)
    return x * (0.5 * jnp.tanh(x * (c0 + (c0 * 0.044715) * (x * x))) + 0.5)


def _lru_kernel(xb_ref, gb_ref, cw_ref, cb_ref, wra_ref, bra_ref, wrx_ref, brx_ref, lam_ref,
                out_ref, hist_ref, hc_ref, a_s, u_s):
    t = pl.program_id(1)
    tt, w = xb_ref.shape

    @pl.when(t == 0)
    def _():
        hist_ref[...] = jnp.zeros_like(hist_ref)
        hc_ref[...] = jnp.zeros_like(hc_ref)

    row = lax.broadcasted_iota(jnp.int32, (tt, w), 0)
    valid = (row >= PAD) | (t > 0)
    x = jnp.where(valid, xb_ref[...].astype(F32), 0.0)
    xfull = jnp.concatenate([hist_ref[...], x], axis=0)
    conv = cb_ref[...] + cw_ref[CONV_W - 1:CONV_W, :] * x
    for d in range(1, CONV_W):
        shifted = pltpu.roll(xfull, d, 0)[SUBLANE:, :]
        conv = conv + cw_ref[CONV_W - 1 - d:CONV_W - d, :] * shifted
    hist_ref[...] = x[tt - SUBLANE:, :]

    xcb = conv.astype(BF16)
    r_parts, i_parts = [], []
    for blk in range(B_BLOCKS):
        xs = xcb[:, blk * B_BLOCK:(blk + 1) * B_BLOCK]
        r_parts.append(_dot(xs, wra_ref[blk]))
        i_parts.append(_dot(xs, wrx_ref[blk]))
    r = _sigmoid(jnp.concatenate(r_parts, axis=1) + bra_ref[...])
    ig = _sigmoid(jnp.concatenate(i_parts, axis=1) + brx_ref[...])
    nlam = -lam_ref[...]
    softplus = jnp.maximum(nlam, 0.0) + jnp.log1p(jnp.exp(-jnp.abs(nlam)))
    a = jnp.exp2(((-LRU_C * LOG2E) * softplus) * r)
    u = jnp.sqrt(1.0 - a * a) * (ig * conv)
    a_s[...] = a
    u_s[...] = jnp.where(valid, u, 0.0)

    srow = lax.broadcasted_iota(jnp.int32, (SUBLANE, w), 0)

    def group(gi, hprev):
        sl = pl.ds(pl.multiple_of(gi * SUBLANE, SUBLANE), SUBLANE)
        ag = a_s[sl, :]
        ug = u_s[sl, :]
        for d in (1, 2, 4):
            ok = srow >= d
            a_sh = jnp.where(ok, pltpu.roll(ag, d, 0), 1.0)
            u_sh = jnp.where(ok, pltpu.roll(ug, d, 0), 0.0)
            ug = ag * u_sh + ug
            ag = ag * a_sh
        hg = ug + ag * hprev
        u_s[sl, :] = hg
        return jnp.broadcast_to(hg[SUBLANE - 1:SUBLANE, :], (SUBLANE, w))

    hc_ref[...] = lax.fori_loop(0, tt // SUBLANE, group, hc_ref[...])
    out_ref[...] = (u_s[...] * _gelu_tanh(gb_ref[...].astype(F32))).astype(out_ref.dtype)


def _lru(xb, gb, cw, cb, wra, bra, wrx, brx, lam, bsz, tp, tt):
    w = xb.shape[1]
    nt = tp // tt
    row_map = lambda b, t: (b * nt + t, 0)
    const2 = lambda b, t: (0, 0)
    const3 = lambda b, t: (0, 0, 0)
    return pl.pallas_call(
        _lru_kernel,
        out_shape=jax.ShapeDtypeStruct(xb.shape, BF16),
        grid=(bsz, nt),
        in_specs=[pl.BlockSpec((tt, w), row_map), pl.BlockSpec((tt, w), row_map),
                  pl.BlockSpec((CONV_W, w), const2), pl.BlockSpec((1, w), const2),
                  pl.BlockSpec((B_BLOCKS, B_BLOCK, B_BLOCK), const3), pl.BlockSpec((1, w), const2),
                  pl.BlockSpec((B_BLOCKS, B_BLOCK, B_BLOCK), const3), pl.BlockSpec((1, w), const2),
                  pl.BlockSpec((1, w), const2)],
        out_specs=pl.BlockSpec((tt, w), row_map),
        scratch_shapes=[pltpu.VMEM((SUBLANE, w), F32), pltpu.VMEM((SUBLANE, w), F32),
                        pltpu.VMEM((tt, w), F32), pltpu.VMEM((tt, w), F32)],
        compiler_params=_cparams(("parallel", "arbitrary")),
        name="conv_rglru",
    )(xb, gb, cw, cb, wra, bra, wrx, brx, lam)


def _out_mlp_kernel(n_mix, tf, h_ref, *refs):
    mix_refs = refs[:n_mix]
    ln_ref = refs[n_mix]
    n_w = n_mix + 2
    w_hbm = refs[n_mix + 1:n_mix + 1 + n_w]
    out_ref = refs[n_mix + 1 + n_w]
    w_vmem = refs[n_mix + 2 + n_w:n_mix + 2 + 2 * n_w]
    sem = refs[n_mix + 2 + 2 * n_w]

    @pl.when((pl.program_id(0) == 0) & (pl.program_id(1) == 0))
    def _():
        copies = [pltpu.make_async_copy(src, dst, sem.at[i])
                  for i, (src, dst) in enumerate(zip(w_hbm, w_vmem))]
        for cp in copies:
            cp.start()
        for cp in copies:
            cp.wait()

    wo_refs, w1_ref, w2_ref = w_vmem[:n_mix], w_vmem[n_mix], w_vmem[n_mix + 1]
    res = h_ref[...]
    for m_ref, w_ref in zip(mix_refs, wo_refs):
        res = res + _dot(m_ref[...], w_ref[...])
    hn = _rms_rows(res, ln_ref[...]).astype(BF16)
    acc = res
    for c in range(w1_ref.shape[1] // tf):
        a = jnp.maximum(_dot(hn, w1_ref[:, c * tf:(c + 1) * tf]), 0.0)
        acc = acc + _dot((a * a).astype(BF16), w2_ref[c * tf:(c + 1) * tf, :])
    out_ref[...] = acc


def _out_mlp(h2, mixes, wouts, ln, w1, w2, tm, tf, tokens_only=None):
    r, d = h2.shape
    n_mix = len(mixes)
    if tokens_only is None:
        grid, out_rows = (r // tm, 1), r
        row_spec = lambda n: pl.BlockSpec((tm, n), lambda i, j: (i, 0))
        out_spec = pl.BlockSpec((tm, d), lambda i, j: (i, 0))
    else:
        tp, seq = tokens_only
        per_seq = seq // tm
        grid, out_rows = (r // tp, per_seq), (r // tp) * seq
        row_spec = lambda n: pl.BlockSpec(
            (pl.Element(tm), pl.Element(n)),
            lambda b, i: (pl.multiple_of(b * tp + (tp - seq) + i * tm, LANE), 0))
        out_spec = pl.BlockSpec((tm, d), lambda b, i: (b * per_seq + i, 0))
    weights = [*wouts, w1, w2]
    in_specs = [row_spec(d)] + [row_spec(m.shape[1]) for m in mixes]
    in_specs += [pl.BlockSpec((1, d), lambda i, j: (0, 0))]
    in_specs += [pl.BlockSpec(memory_space=pl.ANY) for _ in weights]
    scratch = [pltpu.VMEM(w.shape, w.dtype) for w in weights]
    scratch.append(pltpu.SemaphoreType.DMA((len(weights),)))
    return pl.pallas_call(
        functools.partial(_out_mlp_kernel, n_mix, tf),
        out_shape=jax.ShapeDtypeStruct((out_rows, d), F32),
        grid=grid,
        in_specs=in_specs,
        out_specs=out_spec,
        scratch_shapes=scratch,
        compiler_params=_cparams(("arbitrary", "arbitrary")),
        name="out_mlp",
    )(h2, *mixes, ln, *weights)


OD_COLS = dict(cq=(0, 384), ckv=(384, 640), kr=(640, 768), fq=(768, 1280), fk=(1280, 1792),
               fv=(1792, 2304), ff=(2304, 2432))
OD_N = 2432


def _od_in_kernel(h_ref, ln_ref, w_ref, pos_ref, frq_ref, sgn_ref, gqa_ref, gkva_ref,
                  wq_ref, wkn_ref, wv_ref, gq_ref, gk_ref, gkr_ref, gfq_ref, gfk_ref, bf_ref,
                  gm_ref, g64_ref, ilm_ref, sel_ref,
                  qm_ref, km_ref, vmt_ref, qf_ref, kf_ref, vft_ref, fc_ref):
    t = pl.program_id(1)
    tm = h_ref.shape[1]

    @pl.when(t == 0)
    def _():
        fc_ref[...] = jnp.zeros_like(fc_ref)

    xn = _rms_rows(h_ref[0], ln_ref[...]).astype(BF16)

    def proj(name):
        a, b = OD_COLS[name]
        return _dot(xn, w_ref[:, a:b])

    wide = 2 * LANE
    lane2 = lax.broadcasted_iota(jnp.int32, (tm, wide), 1) % LANE
    row2 = lax.broadcasted_iota(jnp.int32, (tm, wide), 0)
    unused_row2 = (row2 < PAD) & (t == 0)
    mask_lane = lane2 == MASK_LANE
    ang = pos_ref[0] * frq_ref[...]
    cs = jnp.cos(ang)
    sn = jnp.sin(ang) * sgn_ref[...]
    cs = jnp.concatenate([cs, cs], axis=1)
    sn = jnp.concatenate([sn, sn], axis=1)
    first_half = lane2 < ROPE_MID

    def seg_rms(x, sel, inv_len, gain):
        ss = _dot((x * x).astype(BF16), sel)
        return x * lax.rsqrt(ss * inv_len + EPS) * gain

    def rope(y):
        sw = jnp.where(first_half, pltpu.roll(y, wide - HALF, 1), pltpu.roll(y, HALF, 1))
        return y * cs + sw * sn

    gm = gm_ref[...]
    g64 = g64_ref[...]
    ilm = ilm_ref[...]
    inv64 = 1.0 / D_HD

    cqn = _rms_rows(proj("cq"), gqa_ref[...]).astype(BF16)
    qall = _dot(cqn, wq_ref[...])
    scale_m = (C_NOPE + C_ROPE) ** -0.5 * LOG2E
    for h2 in range(C_HEADS // 2):
        cols = slice(h2 * wide, (h2 + 1) * wide)
        y = rope(seg_rms(qall[:, cols], gm, ilm, gq_ref[...]))
        qm_ref[0, :, cols] = jnp.where(mask_lane, 1.0, y * scale_m).astype(BF16)
    ckvn = _rms_rows(proj("ckv"), gkva_ref[...]).astype(BF16)
    knall = _dot(ckvn, wkn_ref[...])
    kr = proj("kr")
    krt = rope(seg_rms(jnp.concatenate([kr, kr], axis=1), gm, ilm, gkr_ref[...]))
    krt = jnp.where(mask_lane & unused_row2, NEG, krt)
    for h2 in range(C_HEADS // 2):
        cols = slice(h2 * wide, (h2 + 1) * wide)
        y = seg_rms(knall[:, cols], gm, ilm, gk_ref[...]) + krt
        km_ref[0, :, cols] = y.astype(BF16)
    vmt_ref[0] = _dot(ckvn, wv_ref[...]).T.astype(BF16)

    fq = proj("fq")
    fk = proj("fk")
    scale_f = D_HD ** -0.5 * LOG2E
    npair = D_HEADS // 2
    for p2 in range(npair // 2):
        cols = slice(p2 * wide, (p2 + 1) * wide)
        y = seg_rms(fq[:, cols], g64, inv64, gfq_ref[...])
        qf_ref[0, :, cols] = (y * scale_f).astype(BF16)
        y = seg_rms(fk[:, cols], g64, inv64, gfk_ref[...]).astype(BF16)
        for e in range(2):
            p = 2 * p2 + e
            kf_ref[0, :, 2 * p * LANE:(2 * p + 1) * LANE] = y[:, e * LANE:(e + 1) * LANE]
    vft_ref[0] = proj("fv").T.astype(BF16)

    lane = lax.broadcasted_iota(jnp.int32, (tm, LANE), 1)
    unused_row = (lax.broadcasted_iota(jnp.int32, (tm, LANE), 0) < PAD) & (t == 0)
    valid = jnp.logical_not(unused_row) & (lane < D_HEADS)
    lf = jnp.where(valid, _log_sigmoid(proj("ff") + bf_ref[...]), 0.0)
    fc = _dot_sel_l(_tril_bf16(tm), lf, 3) + fc_ref[...]
    fc_ref[...] = fc[tm - 1:tm, :]
    hi, mid, lo = _split3(jnp.where(unused_row, NEG, fc * (-LOG2E)))
    bias = _dot(hi, sel_ref[0]) + _dot(mid, sel_ref[1]) + _dot(lo, sel_ref[2])
    for p in range(npair):
        kf_ref[0, :, (2 * p + 1) * LANE:(2 * p + 2) * LANE] = bias[:, p * LANE:(p + 1) * LANE].astype(BF16)


def _od_in(h3, pos_col, consts, tm):
    bsz, tp, d = h3.shape
    nt = tp // tm
    names = ["ln", "w", None, "frq", "sgn", "gqa", "gkva", "wq", "wkn", "wv", "gq", "gk", "gkr",
             "gfq", "gfk", "bf", "gm", "g64", "ilm", "sel"]
    args, in_specs = [h3], [pl.BlockSpec((1, tm, d), lambda b, t: (b, t, 0))]
    for nm in names:
        if nm is None:
            args.append(pos_col)
            in_specs.append(pl.BlockSpec((1, tm, 1), lambda b, t: (b, t, 0)))
            continue
        a = consts[nm]
        args.append(a)
        in_specs.append(pl.BlockSpec(a.shape, (lambda nd: (lambda b, t: (0,) * nd))(a.ndim)))
    hl = C_HEADS * LANE
    pk = D_HEADS * D_HD
    outs = [jax.ShapeDtypeStruct((bsz, tp, hl), BF16), jax.ShapeDtypeStruct((bsz, tp, hl), BF16),
            jax.ShapeDtypeStruct((bsz, pk, tp), BF16), jax.ShapeDtypeStruct((bsz, tp, pk), BF16),
            jax.ShapeDtypeStruct((bsz, tp, 2 * pk), BF16), jax.ShapeDtypeStruct((bsz, pk, tp), BF16)]
    rowspec = lambda n: pl.BlockSpec((1, tm, n), lambda b, t: (b, t, 0))
    tspec = pl.BlockSpec((1, pk, tm), lambda b, t: (b, 0, t))
    return pl.pallas_call(
        _od_in_kernel,
        out_shape=outs,
        grid=(bsz, nt),
        in_specs=in_specs,
        out_specs=[rowspec(hl), rowspec(hl), tspec, rowspec(pk), rowspec(2 * pk), tspec],
        scratch_shapes=[pltpu.VMEM((1, LANE), F32)],
        compiler_params=_cparams(("parallel", "arbitrary")),
        name="od_in",
    )(*args)


def _attn_kernel(fox, blk, q_ref, k_ref, vt_ref, out_ref, st_ref, acc_ref, buf0, buf1, bufm,
                 mx0, mx1, mxm, *maybe_qe):
    tp = k_ref.shape[1]
    n_blocks = (tp - LANE) // blk
    half = LANE // 2

    def start_qblock(q0, nq):
        st_ref[0:2, 0:nq] = jnp.full((2, nq), NEG, F32)
        st_ref[2:4, 0:nq] = jnp.zeros((2, nq), F32)
        acc_ref[:, :, 0:nq] = jnp.zeros((2, LANE, nq), F32)
        if fox:
            qe_ref = maybe_qe[0]
            q = q_ref[0, pl.ds(q0, nq), :]
            lane = lax.broadcasted_iota(jnp.int32, (nq, LANE), 1)
            zero = jnp.zeros_like(q)
            for hd in range(2):
                mine = (lane >= hd * half) & (lane < (hd + 1) * half)
                ones = ((lane >= 3 * hd) & (lane < 3 * hd + 3)).astype(F32).astype(BF16)
                qe_ref[hd, 0:nq, :] = jnp.concatenate([jnp.where(mine, q, zero), ones], axis=1)

    def logits_to(buf, mx, q0, nq, k0, nk, diagonal):
        kblk = k_ref[0, pl.ds(k0, nk), :]
        if diagonal:
            ok = (lax.broadcasted_iota(jnp.int32, (nk, nq), 0)
                  <= lax.broadcasted_iota(jnp.int32, (nk, nq), 1))
        for hd in range(2):
            if fox:
                s = _dot_nt(kblk, maybe_qe[0][hd, 0:nq, :])
            else:
                s = _dot_nt(kblk[:, hd * LANE:(hd + 1) * LANE],
                            q_ref[0, pl.ds(q0, nq), hd * LANE:(hd + 1) * LANE])
            if diagonal:
                s = jnp.where(ok, s, NEG)
            buf[hd, 0:nk, 0:nq] = s
            mx[hd:hd + 1, 0:nq] = jnp.max(s, axis=0, keepdims=True)

    def consume(buf, mx, nq, k0, nk):
        vt = vt_ref[0, :, pl.ds(k0, nk)]
        for hd in range(2):
            m_old = st_ref[hd:hd + 1, 0:nq]
            l_old = st_ref[2 + hd:3 + hd, 0:nq]
            m_new = jnp.maximum(m_old, mx[hd:hd + 1, 0:nq])
            alpha = jnp.exp2(m_old - m_new)
            p = jnp.exp2(buf[hd, 0:nk, 0:nq] - m_new)
            st_ref[hd:hd + 1, 0:nq] = m_new
            st_ref[2 + hd:3 + hd, 0:nq] = alpha * l_old + jnp.sum(p, axis=0, keepdims=True)
            acc_ref[hd, :, 0:nq] = acc_ref[hd, :, 0:nq] * alpha + _dot(vt, p.astype(BF16))

    def finish_qblock(q0, nq):
        o_a = acc_ref[0, 0:half, 0:nq] / st_ref[2:3, 0:nq]
        o_b = acc_ref[1, half:LANE, 0:nq] / st_ref[3:4, 0:nq]
        out_ref[0, pl.ds(q0, nq), :] = jnp.concatenate([o_a, o_b], axis=0).T.astype(out_ref.dtype)

    start_qblock(0, LANE)
    logits_to(bufm, mxm, 0, LANE, 0, LANE, True)
    consume(bufm, mxm, LANE, 0, LANE)
    finish_qblock(0, LANE)

    def qblock(i, carry):
        q0 = pl.multiple_of(LANE + i * blk, LANE)

        def k_of(n):
            return pl.multiple_of(LANE + n * blk, LANE)

        start_qblock(q0, blk)
        logits_to(bufm, mxm, q0, blk, 0, LANE, False)

        @pl.when(i == 0)
        def _():
            logits_to(buf1, mx1, q0, blk, q0, blk, True)
            consume(bufm, mxm, blk, 0, LANE)
            consume(buf1, mx1, blk, q0, blk)

        @pl.when(i > 0)
        def _():
            logits_to(buf0, mx0, q0, blk, k_of(0), blk, False)
            consume(bufm, mxm, blk, 0, LANE)

            def pair(m, c):
                logits_to(buf1, mx1, q0, blk, k_of(2 * m + 1), blk, False)
                consume(buf0, mx0, blk, k_of(2 * m), blk)
                logits_to(buf0, mx0, q0, blk, k_of(2 * m + 2), blk, False)
                consume(buf1, mx1, blk, k_of(2 * m + 1), blk)
                return c

            n_pairs = (i - 1) // 2
            lax.fori_loop(0, n_pairs, pair, 0)
            held = 2 * n_pairs

            @pl.when(i % 2 == 1)
            def _():
                logits_to(buf1, mx1, q0, blk, q0, blk, True)
                consume(buf0, mx0, blk, k_of(held), blk)
                consume(buf1, mx1, blk, q0, blk)

            @pl.when(i % 2 == 0)
            def _():
                logits_to(buf1, mx1, q0, blk, k_of(held + 1), blk, False)
                consume(buf0, mx0, blk, k_of(held), blk)
                logits_to(buf0, mx0, q0, blk, q0, blk, True)
                consume(buf1, mx1, blk, k_of(held + 1), blk)
                consume(buf0, mx0, blk, q0, blk)

        finish_qblock(q0, blk)
        return carry

    lax.fori_loop(0, n_blocks, qblock, 0)


def _attention(q, k, vt, fox, blk):
    bsz, tp, _ = q.shape
    npair = vt.shape[1] // LANE
    qw = LANE if fox else 2 * LANE
    scratch = [pltpu.VMEM((SUBLANE, blk), F32), pltpu.VMEM((2, LANE, blk), F32),
               pltpu.VMEM((2, blk, blk), F32), pltpu.VMEM((2, blk, blk), F32),
               pltpu.VMEM((2, LANE, blk), F32),
               pltpu.VMEM((SUBLANE, blk), F32), pltpu.VMEM((SUBLANE, blk), F32),
               pltpu.VMEM((SUBLANE, blk), F32)]
    if fox:
        scratch.append(pltpu.VMEM((2, blk, 2 * LANE), BF16))
    return pl.pallas_call(
        functools.partial(_attn_kernel, fox, blk),
        out_shape=jax.ShapeDtypeStruct((bsz, tp, npair * LANE), BF16),
        grid=(bsz, npair),
        in_specs=[pl.BlockSpec((1, tp, qw), lambda b, p: (b, 0, p)),
                  pl.BlockSpec((1, tp, 2 * LANE), lambda b, p: (b, 0, p)),
                  pl.BlockSpec((1, LANE, tp), lambda b, p: (b, p, 0))],
        out_specs=pl.BlockSpec((1, tp, LANE), lambda b, p: (b, 0, p)),
        scratch_shapes=scratch,
        compiler_params=_cparams(("parallel", "parallel")),
        name="fox_attn" if fox else "mla_attn",
    )(q, k, vt)


def _place_cols(w, width, offset):
    k, n = w.shape
    return jnp.pad(w, ((0, 0), (offset, width - offset - n)))


def _ev_consts(w_in, b_if):
    nq, nv = A_HEADS * A_DQK, A_HEADS * A_DV
    o = 0
    wq = w_in[:, o:o + nq]; o += nq
    wk = w_in[:, o:o + nq]; o += nq
    wv = w_in[:, o:o + nv]; o += nv
    wo = w_in[:, o:o + nv]; o += nv
    wg = w_in[:, o:o + 2 * A_HEADS]; o += 2 * A_HEADS
    wxb = w_in[:, o:o + 1024]; o += 1024
    wgb = w_in[:, o:o + 1024]
    w = jnp.concatenate([wq, wk, wv, wo, _place_cols(wg, LANE, 0), wxb, wgb], axis=1).astype(BF16)
    bif = _place_cols(b_if[None, :], LANE, 0)
    return w, bif


def _head_tile_cols(w, n_heads, per_head, take, offset):
    k = w.shape[0]
    wh = w.reshape(k, n_heads, per_head)[:, :, take[0]:take[0] + take[1]]
    wh = jnp.pad(wh, ((0, 0), (0, 0), (offset, LANE - offset - take[1])))
    return wh.reshape(k, n_heads * LANE)


def _row128(parts):
    r = jnp.zeros((LANE,), F32)
    for off, v in parts:
        r = r.at[off:off + v.shape[0]].set(v.astype(F32))
    return r[None, :]


def _od_consts(p):
    c = {}
    w_in = p["w_in"]
    o = 0
    wcq = w_in[:, o:o + C_Q_LORA]; o += C_Q_LORA
    wckv = w_in[:, o:o + C_KV_LORA]; o += C_KV_LORA
    wkr = w_in[:, o:o + C_ROPE]; o += C_ROPE
    nf = D_HEADS * D_HD
    wfq = w_in[:, o:o + nf]; o += nf
    wfk = w_in[:, o:o + nf]; o += nf
    wfv = w_in[:, o:o + nf]; o += nf
    wff = w_in[:, o:o + D_HEADS]
    c["w"] = jnp.concatenate([wcq, wckv, _place_cols(wkr, LANE, ROPE_LO), wfq, wfk, wfv,
                              _place_cols(wff, LANE, 0)], axis=1).astype(BF16)
    c["ln"] = p["ln"][None, :]
    freqs = ROPE_THETA ** (-jnp.arange(HALF, dtype=F32) / HALF)
    c["frq"] = _row128([(ROPE_LO, freqs), (ROPE_MID, freqs)])
    c["sgn"] = _row128([(ROPE_LO, -jnp.ones((HALF,), F32)), (ROPE_MID, jnp.ones((HALF,), F32))])
    c["gqa"] = p["g_qa"][None, :]
    c["gkva"] = p["g_kva"][None, :]
    dq = C_NOPE + C_ROPE
    wuq, wukv = p["w_uq"], p["w_ukv"]
    c["wq"] = (_head_tile_cols(wuq, C_HEADS, dq, (0, dq), 0)).astype(BF16)
    c["wkn"] = _head_tile_cols(wukv, C_HEADS, C_NOPE + C_V, (0, C_NOPE), 0).astype(BF16)
    c["wv"] = wukv.reshape(-1, C_HEADS, C_NOPE + C_V)[:, :, C_NOPE:].reshape(-1, C_HEADS * C_V).astype(BF16)
    c["gq"] = _row128([(0, p["g_qn"]), (ROPE_LO, p["g_qr"])])
    c["gk"] = _row128([(0, p["g_kn"])])
    c["gkr"] = _row128([(ROPE_LO, p["g_kr"])])
    c["gfq"] = _row128([(0, p["g_fq"]), (D_HD, p["g_fq"])])
    c["gfk"] = _row128([(0, p["g_fk"]), (D_HD, p["g_fk"])])
    c["bf"] = _row128([(0, p["b_f"])])
    seg = np.where(np.arange(LANE) < C_NOPE, 0, np.where(np.arange(LANE) < ROPE_HI, 1, 2))
    c["gm"] = jnp.asarray(seg[:, None] == seg[None, :], F32).astype(BF16)
    seg64 = np.arange(LANE) // D_HD
    c["g64"] = jnp.asarray(seg64[:, None] == seg64[None, :], F32).astype(BF16)
    c["ilm"] = jnp.asarray(np.where(seg == 0, 1.0 / C_NOPE, 1.0 / C_ROPE)[None, :], F32)
    sel = np.zeros((3, LANE, (D_HEADS // 2) * LANE), np.float32)
    for h in range(D_HEADS):
        for part in range(3):
            sel[part, h, (h // 2) * LANE + 3 * (h % 2) + part] = 1.0
    c["sel"] = jnp.asarray(sel).astype(BF16)
    for nm in ("gq", "gk", "gkr", "gfq", "gfk", "ilm"):
        c[nm] = jnp.tile(c[nm], (1, 2))
    for nm in ("gm", "g64"):
        z = jnp.zeros_like(c[nm])
        c[nm] = jnp.block([[c[nm], z], [z, c[nm]]])
    return c


def _time_tile(tp):
    return 3 * LANE if tp % (3 * LANE) == 0 else LANE


def kernel(x, positions, meta, ev_ln, ev_w_in, ev_b_if, ev_a_norm, ev_conv_w, ev_conv_b, ev_w_ra, ev_b_ra, ev_w_rx, ev_b_rx, ev_lam, ev_w_out, od_ln, od_w_in, od_b_f, od_g_qa, od_g_kva, od_w_uq, od_w_ukv, od_g_qn, od_g_qr, od_g_kn, od_g_kr, od_g_fq, od_g_fk, od_w_out, mlp_ln, w_ff1, w_ff2):
    bsz, seq, d = x.shape
    tp = LANE + seq
    r = bsz * tp
    tm = 512 if r % 512 == 0 else LANE
    tf = w_ff1.shape[-1]
    tt = _time_tile(tp)

    h = jnp.concatenate([jnp.zeros((bsz, PAD, d), x.dtype),
                         jnp.broadcast_to(meta.astype(x.dtype), (bsz, N_META, d)), x], axis=1)
    pos_full = jnp.concatenate([jnp.zeros((bsz, PAD), jnp.int32),
                                jnp.broadcast_to(jnp.arange(N_META, dtype=jnp.int32)[None, :], (bsz, N_META)),
                                positions + N_META], axis=1)
    pos_col = pos_full.astype(F32)[:, :, None]
    h2 = h.reshape(r, d)

    e = 0
    w_a, bif = _ev_consts(ev_w_in[e], ev_b_if[e])
    q, k, v, o, g, xb, gb = _ev_in(h2, ev_ln[e][None, :], w_a, tm)
    ha = _mlstm(q, k, v, o, g, bif, ev_a_norm[e][None, :], bsz, tp, tt)
    hb = _lru(xb, gb, ev_conv_w[e], ev_conv_b[e][None, :], ev_w_ra[e].astype(BF16), ev_b_ra[e][None, :],
              ev_w_rx[e].astype(BF16), ev_b_rx[e][None, :], ev_lam[e][None, :], bsz, tp, tt)
    nv = A_HEADS * A_DV
    wo = ev_w_out[e].astype(BF16)
    h2 = _out_mlp(h2, [ha, hb], [wo[:nv], wo[nv:]], mlp_ln[0][None, :],
                  w_ff1[0].astype(BF16), w_ff2[0].astype(BF16), tm, tf)

    oi = 0
    consts = _od_consts(dict(ln=od_ln[oi], w_in=od_w_in[oi], b_f=od_b_f[oi], g_qa=od_g_qa[oi],
                             g_kva=od_g_kva[oi], w_uq=od_w_uq[oi], w_ukv=od_w_ukv[oi],
                             g_qn=od_g_qn[oi], g_qr=od_g_qr[oi], g_kn=od_g_kn[oi], g_kr=od_g_kr[oi],
                             g_fq=od_g_fq[oi], g_fk=od_g_fk[oi]))
    qm, km, vmt, qf, kf, vft = _od_in(h2.reshape(bsz, tp, d), pos_col, consts, tt)
    blk = 512 if seq % 512 == 0 else LANE
    hc = _attention(qm, km, vmt, False, blk)
    hd = _attention(qf, kf, vft, True, blk)
    nc = C_HEADS * C_V
    wo = od_w_out[oi].astype(BF16)
    out = _out_mlp(h2, [hc.reshape(r, nc), hd.reshape(r, nc)], [wo[:nc], wo[nc:]], mlp_ln[1][None, :],
                  w_ff1[1].astype(BF16), w_ff2[1].astype(BF16), tm if seq % tm == 0 else LANE, tf,
                  tokens_only=(tp, seq))
    return out.reshape(bsz, seq, d)
```

```python
import functools
import math

import numpy as np
import jax
import jax.numpy as jnp
from jax import lax
from jax.experimental import pallas as pl
from jax.experimental.pallas import tpu as pltpu

F32 = jnp.float32
BF16 = jnp.bfloat16

N_META = 16
EPS = 1e-6
A_HEADS = 4
A_DQK = 128
A_DV = 256
B_BLOCKS = 8
B_BLOCK = 128
CONV_W = 4
LRU_C = 8.0
C_HEADS = 8
C_Q_LORA = 384
C_KV_LORA = 256
C_NOPE = 64
C_ROPE = 32
C_V = 64
ROPE_THETA = 10000.0
D_HEADS = 8
D_HD = 64

LANE = 128
SUBLANE = 8
VMEM_LIMIT = 56 * 1024 * 1024

PAD = LANE - N_META
NEG = -1e30
LOG2E = math.log2(math.e)

HALF = C_ROPE // 2
ROPE_LO = C_NOPE
ROPE_MID = C_NOPE + HALF
ROPE_HI = C_NOPE + C_ROPE
MASK_LANE = ROPE_HI


def _cparams(sem):
    return pltpu.CompilerParams(dimension_semantics=sem, vmem_limit_bytes=VMEM_LIMIT)


def _dot(a, b):
    return jnp.dot(a, b, preferred_element_type=F32)


def _dot_nt(a, b):
    return lax.dot_general(a, b, (((1,), (1,)), ((), ())), preferred_element_type=F32)


def _split2(x):
    hi = x.astype(BF16)
    lo = (x - hi.astype(F32)).astype(BF16)
    return hi, lo


def _split3(x):
    hi = x.astype(BF16)
    r = x - hi.astype(F32)
    mid = r.astype(BF16)
    lo = (r - mid.astype(F32)).astype(BF16)
    return hi, mid, lo


def _dot_sel_r(x, sel, parts):
    pieces = _split3(x) if parts == 3 else _split2(x)
    acc = _dot(pieces[0], sel)
    for p in pieces[1:]:
        acc = acc + _dot(p, sel)
    return acc


def _dot_sel_l(sel, x, parts):
    pieces = _split3(x) if parts == 3 else _split2(x)
    acc = _dot(sel, pieces[0])
    for p in pieces[1:]:
        acc = acc + _dot(sel, p)
    return acc


def _rms_rows(x, g):
    ms = jnp.mean(x * x, axis=-1, keepdims=True)
    return x * lax.rsqrt(ms + EPS) * g


def _log_sigmoid(x):
    return -(jnp.maximum(-x, 0.0) + jnp.log1p(jnp.exp(-jnp.abs(x))))


def _sigmoid(x):
    return 0.5 * jnp.tanh(0.5 * x) + 0.5


def _tril_bf16(n):
    r = lax.broadcasted_iota(jnp.int32, (n, n), 0)
    c = lax.broadcasted_iota(jnp.int32, (n, n), 1)
    return jnp.where(r >= c, 1.0, 0.0).astype(BF16)


EV_SLABS = (("q", 512, BF16), ("k", 512, BF16), ("v", 1024, BF16), ("o", 1024, BF16),
            ("g", 128, F32), ("xb", 1024, BF16), ("gb", 1024, BF16))


def _input_tile(x_ref, mb_ref):
    xt = x_ref[...]
    first = jnp.concatenate([mb_ref[...], xt[:xt.shape[0] - LANE]], axis=0)
    return jnp.where(pl.program_id(1) == 0, first, xt)


def _input_specs(seq, tt, d):
    x_spec = pl.BlockSpec(
        (pl.Element(tt), pl.Element(d)),
        lambda b, t: (pl.multiple_of(b * seq + jnp.maximum(t * tt - LANE, 0), LANE), 0))
    return [x_spec, pl.BlockSpec((LANE, d), lambda b, t: (0, 0))]


def _ev_in_kernel(x_ref, mb_ref, ln_ref, w_ref, *out_refs):
    xn = _rms_rows(_input_tile(x_ref, mb_ref), ln_ref[...]).astype(BF16)
    off = 0
    for (_, width, dt), ref in zip(EV_SLABS, out_refs):
        ref[...] = _dot(xn, w_ref[:, off:off + width]).astype(dt)
        off += width


def _ev_in(x2, mb, ln, w, bsz, tp, tt):
    d = x2.shape[1]
    n = w.shape[1]
    nt = tp // tt
    outs = [jax.ShapeDtypeStruct((bsz * tp, width), dt) for (_, width, dt) in EV_SLABS]
    return pl.pallas_call(
        _ev_in_kernel,
        out_shape=outs,
        grid=(bsz, nt),
        in_specs=_input_specs(x2.shape[0] // bsz, tt, d) + [
            pl.BlockSpec((1, d), lambda b, t: (0, 0)),
            pl.BlockSpec((d, n), lambda b, t: (0, 0))],
        out_specs=[pl.BlockSpec((tt, width), lambda b, t: (b * nt + t, 0)) for (_, width, _) in EV_SLABS],
        compiler_params=_cparams(("parallel", "parallel")),
        name="ev_in",
    )(x2, mb, ln, w)


def _mlstm_kernel(q_ref, k_ref, v_ref, o_ref, g_ref, bif_ref, an_ref, out_ref, cn_ref, m_ref):
    c = pl.program_id(1)
    L = q_ref.shape[0]
    dv_ext = A_DV + LANE

    @pl.when(c == 0)
    def _():
        cn_ref[...] = jnp.zeros_like(cn_ref)
        m_ref[...] = jnp.zeros_like(m_ref)

    lane = lax.broadcasted_iota(jnp.int32, (L, LANE), 1)
    row = lax.broadcasted_iota(jnp.int32, (L, LANE), 0)
    valid = (row >= PAD) | (c > 0)
    g = g_ref[...] + bif_ref[...]
    li = jnp.where(valid, g, NEG)
    lf = jnp.where(valid, _log_sigmoid(g), 0.0)
    is_f = (lane >= A_HEADS) & (lane < 2 * A_HEADS)
    lf = jnp.where(is_f, lf, 0.0)
    tril = _tril_bf16(L)
    bcum = _dot_sel_l(tril, lf, 3)
    li_t = li.T
    bcum_t = bcum.T

    rr = lax.broadcasted_iota(jnp.int32, (L, L), 0)
    cc = lax.broadcasted_iota(jnp.int32, (L, L), 1)
    causal = rr >= cc
    e0 = jnp.where(lane == 0, 1.0, 0.0).astype(BF16)
    scale = A_DQK ** -0.5

    for h in range(A_HEADS):
        qs = (q_ref[:, h * A_DQK:(h + 1) * A_DQK].astype(F32) * scale).astype(BF16)
        kk = k_ref[:, h * A_DQK:(h + 1) * A_DQK]
        vext = jnp.concatenate([v_ref[:, h * A_DV:(h + 1) * A_DV], e0], axis=1)
        i_col = li[:, h:h + 1]
        b_col = bcum[:, A_HEADS + h:A_HEADS + h + 1]
        i_row = li_t[h:h + 1, :]
        b_row = bcum_t[A_HEADS + h:A_HEADS + h + 1, :]
        m_state = m_ref[:, h:h + 1]
        cn = cn_ref[h]

        dmat = jnp.where(causal, b_col - b_row + i_row, NEG)
        inter = b_col + m_state
        m_row = jnp.maximum(inter, jnp.max(dmat, axis=1, keepdims=True))
        w_intra = jnp.exp(dmat - m_row)
        w_inter = jnp.exp(inter - m_row)
        s = (_dot_nt(qs, kk) * w_intra).astype(BF16)
        numden = w_inter * _dot(qs, cn.astype(BF16)) + _dot(s, vext)
        num = numden[:, :A_DV]
        den = numden[:, A_DV:A_DV + 1]
        hh = num / jnp.maximum(jnp.abs(den), jnp.exp(-m_row))
        hn = _rms_rows(hh, an_ref[...])
        og = _sigmoid(o_ref[:, h * A_DV:(h + 1) * A_DV].astype(F32))
        out_ref[:, h * A_DV:(h + 1) * A_DV] = (hn * og).astype(out_ref.dtype)

        gsum = b_col[L - 1:L, :]
        dk_col = gsum - b_col + i_col
        dk_row = gsum - b_row + i_row
        m_new = jnp.maximum(gsum + m_state, jnp.max(dk_row, axis=1, keepdims=True))
        wk = jnp.exp(dk_col - m_new)
        sc = jnp.exp(gsum + m_state - m_new)
        kw_t = (kk.astype(F32) * wk).T.astype(BF16)
        cn_ref[h] = sc * cn + _dot(kw_t, vext)
        m_ref[:, h:h + 1] = m_new


def _mlstm(q, k, v, o, g, bif, an, bsz, tp, L):
    nq, nv = A_HEADS * A_DQK, A_HEADS * A_DV
    nc = tp // L
    row_map = lambda b, c: (b * nc + c, 0)
    const = lambda b, c: (0, 0)
    return pl.pallas_call(
        _mlstm_kernel,
        out_shape=jax.ShapeDtypeStruct((bsz * tp, nv), BF16),
        grid=(bsz, nc),
        in_specs=[pl.BlockSpec((L, nq), row_map), pl.BlockSpec((L, nq), row_map),
                  pl.BlockSpec((L, nv), row_map), pl.BlockSpec((L, nv), row_map),
                  pl.BlockSpec((L, LANE), row_map),
                  pl.BlockSpec((1, LANE), const), pl.BlockSpec((1, A_DV), const)],
        out_specs=pl.BlockSpec((L, nv), row_map),
        scratch_shapes=[pltpu.VMEM((A_HEADS, A_DQK, A_DV + LANE), F32),
                        pltpu.VMEM((1, LANE), F32)],
        compiler_params=_cparams(("parallel", "arbitrary")),
        name="mlstm",
    )(q, k, v, o, g, bif, an)


def _gelu_tanh(x):
    c0 = math.sqrt(2.0 / math.pi)
    return x * (0.5 * jnp.tanh(x * (c0 + (c0 * 0.044715) * (x * x))) + 0.5)


def _lru_kernel(xb_ref, gb_ref, cw_ref, cb_ref, wra_ref, bra_ref, wrx_ref, brx_ref, lam_ref,
                out_ref, hist_ref, hc_ref, a_s, u_s):
    t = pl.program_id(1)
    tt, w = xb_ref.shape

    @pl.when(t == 0)
    def _():
        hist_ref[...] = jnp.zeros_like(hist_ref)
        hc_ref[...] = jnp.zeros_like(hc_ref)

    row = lax.broadcasted_iota(jnp.int32, (tt, w), 0)
    valid = (row >= PAD) | (t > 0)
    x = jnp.where(valid, xb_ref[...].astype(F32), 0.0)
    xfull = jnp.concatenate([hist_ref[...], x], axis=0)
    conv = cb_ref[...] + cw_ref[CONV_W - 1:CONV_W, :] * x
    for d in range(1, CONV_W):
        shifted = pltpu.roll(xfull, d, 0)[SUBLANE:, :]
        conv = conv + cw_ref[CONV_W - 1 - d:CONV_W - d, :] * shifted
    hist_ref[...] = x[tt - SUBLANE:, :]

    xcb = conv.astype(BF16)
    r_parts, i_parts = [], []
    for blk in range(B_BLOCKS):
        xs = xcb[:, blk * B_BLOCK:(blk + 1) * B_BLOCK]
        r_parts.append(_dot(xs, wra_ref[blk]))
        i_parts.append(_dot(xs, wrx_ref[blk]))
    r = _sigmoid(jnp.concatenate(r_parts, axis=1) + bra_ref[...])
    ig = _sigmoid(jnp.concatenate(i_parts, axis=1) + brx_ref[...])
    nlam = -lam_ref[...]
    softplus = jnp.maximum(nlam, 0.0) + jnp.log1p(jnp.exp(-jnp.abs(nlam)))
    a = jnp.exp2(((-LRU_C * LOG2E) * softplus) * r)
    y = 1.0 - a * a
    u = jnp.where(y > 0.0, y * lax.rsqrt(y), 0.0) * (ig * conv)
    a_s[...] = a
    u_s[...] = jnp.where(valid, u, 0.0)

    srow = lax.broadcasted_iota(jnp.int32, (SUBLANE, w), 0)

    def group(gi, hprev):
        sl = pl.ds(pl.multiple_of(gi * SUBLANE, SUBLANE), SUBLANE)
        ag = a_s[sl, :]
        ug = u_s[sl, :]
        for d in (1, 2, 4):
            ok = srow >= d
            a_sh = jnp.where(ok, pltpu.roll(ag, d, 0), 1.0)
            u_sh = jnp.where(ok, pltpu.roll(ug, d, 0), 0.0)
            ug = ag * u_sh + ug
            ag = ag * a_sh
        hg = ug + ag * hprev
        u_s[sl, :] = hg
        return jnp.broadcast_to(hg[SUBLANE - 1:SUBLANE, :], (SUBLANE, w))

    hc_ref[...] = lax.fori_loop(0, tt // SUBLANE, group, hc_ref[...])
    out_ref[...] = (u_s[...] * _gelu_tanh(gb_ref[...].astype(F32))).astype(out_ref.dtype)


def _lru(xb, gb, cw, cb, wra, bra, wrx, brx, lam, bsz, tp, tt):
    w = xb.shape[1]
    nt = tp // tt
    row_map = lambda b, t: (b * nt + t, 0)
    const2 = lambda b, t: (0, 0)
    const3 = lambda b, t: (0, 0, 0)
    return pl.pallas_call(
        _lru_kernel,
        out_shape=jax.ShapeDtypeStruct(xb.shape, BF16),
        grid=(bsz, nt),
        in_specs=[pl.BlockSpec((tt, w), row_map), pl.BlockSpec((tt, w), row_map),
                  pl.BlockSpec((CONV_W, w), const2), pl.BlockSpec((1, w), const2),
                  pl.BlockSpec((B_BLOCKS, B_BLOCK, B_BLOCK), const3), pl.BlockSpec((1, w), const2),
                  pl.BlockSpec((B_BLOCKS, B_BLOCK, B_BLOCK), const3), pl.BlockSpec((1, w), const2),
                  pl.BlockSpec((1, w), const2)],
        out_specs=pl.BlockSpec((tt, w), row_map),
        scratch_shapes=[pltpu.VMEM((SUBLANE, w), F32), pltpu.VMEM((SUBLANE, w), F32),
                        pltpu.VMEM((tt, w), F32), pltpu.VMEM((tt, w), F32)],
        compiler_params=_cparams(("parallel", "arbitrary")),
        name="conv_rglru",
    )(xb, gb, cw, cb, wra, bra, wrx, brx, lam)


def _out_mlp_kernel(n_mix, from_input, *refs):
    n_h = 2 if from_input else 1
    h_refs, refs = refs[:n_h], refs[n_h:]
    mix_refs = refs[:n_mix]
    ln_ref = refs[n_mix]
    n_w = n_mix + 2
    w_hbm = refs[n_mix + 1:n_mix + 1 + n_w]
    out_ref = refs[n_mix + 1 + n_w]
    w_vmem = refs[n_mix + 2 + n_w:n_mix + 2 + 2 * n_w]
    sem = refs[n_mix + 2 + 2 * n_w]

    @pl.when((pl.program_id(0) == 0) & (pl.program_id(1) == 0))
    def _():
        copies = [pltpu.make_async_copy(src, dst, sem.at[i])
                  for i, (src, dst) in enumerate(zip(w_hbm, w_vmem))]
        for cp in copies:
            cp.start()
        for cp in copies:
            cp.wait()

    wo_refs, w1_ref, w2_ref = w_vmem[:n_mix], w_vmem[n_mix], w_vmem[n_mix + 1]
    res = _input_tile(*h_refs) if from_input else h_refs[0][...]
    for m_ref, w_ref in zip(mix_refs, wo_refs):
        res = res + _dot(m_ref[...], w_ref[...])
    hn = _rms_rows(res, ln_ref[...]).astype(BF16)
    a = jnp.maximum(_dot(hn, w1_ref[...]), 0.0)
    out_ref[...] = res + _dot((a * a).astype(BF16), w2_ref[...])


def _out_mlp(h_in, mixes, wouts, ln, w1, w2, tm, rows):
    d = w2.shape[1]
    n_mix = len(mixes)
    kind = rows[0]
    if kind == "all":
        r = rows[1]
        grid, out_rows = (r // tm, 1), r
        mix_spec = lambda n: pl.BlockSpec((tm, n), lambda i, j: (i, 0))
        h_specs, h_args = [mix_spec(d)], [h_in]
        out_spec = pl.BlockSpec((tm, d), lambda i, j: (i, 0))
    elif kind == "tokens":
        _, bsz, tp, seq = rows
        per_seq = seq // tm
        grid, out_rows = (bsz, per_seq), bsz * seq
        mix_spec = lambda n: pl.BlockSpec(
            (pl.Element(tm), pl.Element(n)),
            lambda b, i: (pl.multiple_of(b * tp + (tp - seq) + i * tm, LANE), 0))
        h_specs, h_args = [mix_spec(d)], [h_in]
        out_spec = pl.BlockSpec((tm, d), lambda b, i: (b * per_seq + i, 0))
    else:
        _, bsz, tp, seq = rows
        nt = tp // tm
        grid, out_rows = (bsz, nt), bsz * tp
        mix_spec = lambda n: pl.BlockSpec((tm, n), lambda b, t: (b * nt + t, 0))
        h_specs, h_args = _input_specs(seq, tm, d), list(h_in)
        out_spec = mix_spec(d)
    weights = [*wouts, w1, w2]
    in_specs = h_specs + [mix_spec(m.shape[1]) for m in mixes]
    in_specs += [pl.BlockSpec((1, d), lambda i, j: (0, 0))]
    in_specs += [pl.BlockSpec(memory_space=pl.ANY) for _ in weights]
    scratch = [pltpu.VMEM(w.shape, w.dtype) for w in weights]
    scratch.append(pltpu.SemaphoreType.DMA((len(weights),)))
    return pl.pallas_call(
        functools.partial(_out_mlp_kernel, n_mix, kind == "input"),
        out_shape=jax.ShapeDtypeStruct((out_rows, d), F32),
        grid=grid,
        in_specs=in_specs,
        out_specs=out_spec,
        scratch_shapes=scratch,
        compiler_params=_cparams(("arbitrary", "arbitrary")),
        name="out_mlp",
    )(*h_args, *mixes, ln, *weights)


OD_COLS = dict(cq=(0, 384), ckv=(384, 640), kr=(640, 768), fq=(768, 1280), fk=(1280, 1792),
               fv=(1792, 2304), ff=(2304, 2432))
OD_N = 2432


def _od_in_kernel(h_ref, ln_ref, w_ref, pos_ref, frq_ref, sgn_ref, gqa_ref, gkva_ref,
                  wq_ref, wkn_ref, wv_ref, gq_ref, gk_ref, gkr_ref, gfq_ref, gfk_ref, bf_ref,
                  gm_ref, g64_ref, sel_ref,
                  qm_ref, km_ref, vmt_ref, qf_ref, kf_ref, vft_ref, fc_ref):
    t = pl.program_id(1)
    tm = h_ref.shape[1]

    @pl.when(t == 0)
    def _():
        fc_ref[...] = jnp.zeros_like(fc_ref)

    xn = _rms_rows(h_ref[0], ln_ref[...]).astype(BF16)

    def proj(name):
        a, b = OD_COLS[name]
        return _dot(xn, w_ref[:, a:b])

    wide = 2 * LANE
    lane2 = lax.broadcasted_iota(jnp.int32, (tm, wide), 1) % LANE
    row2 = lax.broadcasted_iota(jnp.int32, (tm, wide), 0)
    unused_row2 = (row2 < PAD) & (t == 0)
    mask_lane = lane2 == MASK_LANE
    ang = pos_ref[0] * frq_ref[...]
    cs = jnp.cos(ang)
    sn = jnp.sin(ang) * sgn_ref[...]
    cs = jnp.concatenate([cs, cs], axis=1)
    sn = jnp.concatenate([sn, sn], axis=1)
    first_half = lane2 < ROPE_MID

    def seg_rms(x, sel, gain):
        ms = _dot((x * x).astype(BF16), sel)
        return x * lax.rsqrt(ms + EPS) * gain

    def rope(y):
        sw = jnp.where(first_half, pltpu.roll(y, wide - HALF, 1), pltpu.roll(y, HALF, 1))
        return y * cs + sw * sn

    gm = gm_ref[...]
    g64 = g64_ref[...]

    cqn = _rms_rows(proj("cq"), gqa_ref[...]).astype(BF16)
    qall = _dot(cqn, wq_ref[...])
    scale_m = (C_NOPE + C_ROPE) ** -0.5 * LOG2E
    for h2 in range(C_HEADS // 2):
        cols = slice(h2 * wide, (h2 + 1) * wide)
        y = rope(seg_rms(qall[:, cols], gm,gq_ref[...]))
        qm_ref[0, :, cols] = jnp.where(mask_lane, 1.0, y * scale_m).astype(BF16)
    ckvn = _rms_rows(proj("ckv"), gkva_ref[...]).astype(BF16)
    knall = _dot(ckvn, wkn_ref[...])
    kr = proj("kr")
    krt = rope(seg_rms(jnp.concatenate([kr, kr], axis=1), gm,gkr_ref[...]))
    krt = jnp.where(mask_lane & unused_row2, NEG, krt)
    for h2 in range(C_HEADS // 2):
        cols = slice(h2 * wide, (h2 + 1) * wide)
        y = seg_rms(knall[:, cols], gm,gk_ref[...]) + krt
        km_ref[0, :, cols] = y.astype(BF16)
    vmt_ref[0] = _dot(ckvn, wv_ref[...]).T.astype(BF16)

    fq = proj("fq")
    fk = proj("fk")
    scale_f = D_HD ** -0.5 * LOG2E
    npair = D_HEADS // 2
    for p2 in range(npair // 2):
        cols = slice(p2 * wide, (p2 + 1) * wide)
        y = seg_rms(fq[:, cols], g64,gfq_ref[...])
        qf_ref[0, :, cols] = (y * scale_f).astype(BF16)
        y = seg_rms(fk[:, cols], g64,gfk_ref[...]).astype(BF16)
        for e in range(2):
            p = 2 * p2 + e
            kf_ref[0, :, 2 * p * LANE:(2 * p + 1) * LANE] = y[:, e * LANE:(e + 1) * LANE]
    vft_ref[0] = proj("fv").T.astype(BF16)

    lane = lax.broadcasted_iota(jnp.int32, (tm, LANE), 1)
    unused_row = (lax.broadcasted_iota(jnp.int32, (tm, LANE), 0) < PAD) & (t == 0)
    valid = jnp.logical_not(unused_row) & (lane < D_HEADS)
    lf = jnp.where(valid, _log_sigmoid(proj("ff") + bf_ref[...]), 0.0)
    fc = _dot_sel_l(_tril_bf16(tm), lf, 3) + fc_ref[...]
    fc_ref[...] = fc[tm - 1:tm, :]
    hi, mid, lo = _split3(jnp.where(unused_row, NEG, fc * (-LOG2E)))
    bias = _dot(hi, sel_ref[0]) + _dot(mid, sel_ref[1]) + _dot(lo, sel_ref[2])
    for p in range(npair):
        kf_ref[0, :, (2 * p + 1) * LANE:(2 * p + 2) * LANE] = bias[:, p * LANE:(p + 1) * LANE].astype(BF16)


def _od_in(h3, pos_col, consts, tm):
    bsz, tp, d = h3.shape
    nt = tp // tm
    names = ["ln", "w", None, "frq", "sgn", "gqa", "gkva", "wq", "wkn", "wv", "gq", "gk", "gkr",
             "gfq", "gfk", "bf", "gm", "g64", "sel"]
    args, in_specs = [h3], [pl.BlockSpec((1, tm, d), lambda b, t: (b, t, 0))]
    for nm in names:
        if nm is None:
            args.append(pos_col)
            in_specs.append(pl.BlockSpec((1, tm, 1), lambda b, t: (b, t, 0)))
            continue
        a = consts[nm]
        args.append(a)
        in_specs.append(pl.BlockSpec(a.shape, (lambda nd: (lambda b, t: (0,) * nd))(a.ndim)))
    hl = C_HEADS * LANE
    pk = D_HEADS * D_HD
    outs = [jax.ShapeDtypeStruct((bsz, tp, hl), BF16), jax.ShapeDtypeStruct((bsz, tp, hl), BF16),
            jax.ShapeDtypeStruct((bsz, pk, tp), BF16), jax.ShapeDtypeStruct((bsz, tp, pk), BF16),
            jax.ShapeDtypeStruct((bsz, tp, 2 * pk), BF16), jax.ShapeDtypeStruct((bsz, pk, tp), BF16)]
    rowspec = lambda n: pl.BlockSpec((1, tm, n), lambda b, t: (b, t, 0))
    tspec = pl.BlockSpec((1, pk, tm), lambda b, t: (b, 0, t))
    return pl.pallas_call(
        _od_in_kernel,
        out_shape=outs,
        grid=(bsz, nt),
        in_specs=in_specs,
        out_specs=[rowspec(hl), rowspec(hl), tspec, rowspec(pk), rowspec(2 * pk), tspec],
        scratch_shapes=[pltpu.VMEM((1, LANE), F32)],
        compiler_params=_cparams(("parallel", "arbitrary")),
        name="od_in",
    )(*args)


def _attn_kernel(fox, blk, q_ref, k_ref, vt_ref, out_ref, st_ref, acc_ref, buf0, buf1, bufm,
                 mx0, mx1, mxm, *maybe_qe):
    tp = k_ref.shape[1]
    n_blocks = (tp - LANE) // blk
    half = LANE // 2

    def start_qblock(q0, nq):
        st_ref[0:2, 0:nq] = jnp.full((2, nq), NEG, F32)
        st_ref[2:4, 0:nq] = jnp.zeros((2, nq), F32)
        acc_ref[:, :, 0:nq] = jnp.zeros((2, LANE, nq), F32)
        if fox:
            qe_ref = maybe_qe[0]
            q = q_ref[0, pl.ds(q0, nq), :]
            lane = lax.broadcasted_iota(jnp.int32, (nq, LANE), 1)
            zero = jnp.zeros_like(q)
            for hd in range(2):
                mine = (lane >= hd * half) & (lane < (hd + 1) * half)
                ones = ((lane >= 3 * hd) & (lane < 3 * hd + 3)).astype(F32).astype(BF16)
                qe_ref[hd, 0:nq, :] = jnp.concatenate([jnp.where(mine, q, zero), ones], axis=1)

    def logits_to(buf, mx, q0, nq, k0, nk, diagonal):
        kblk = k_ref[0, pl.ds(k0, nk), :]
        if diagonal:
            ok = (lax.broadcasted_iota(jnp.int32, (nk, nq), 0)
                  <= lax.broadcasted_iota(jnp.int32, (nk, nq), 1))
        for hd in range(2):
            if fox:
                s = _dot_nt(kblk, maybe_qe[0][hd, 0:nq, :])
            else:
                s = _dot_nt(kblk[:, hd * LANE:(hd + 1) * LANE],
                            q_ref[0, pl.ds(q0, nq), hd * LANE:(hd + 1) * LANE])
            if diagonal:
                s = jnp.where(ok, s, NEG)
            buf[hd, 0:nk, 0:nq] = s
            mx[hd:hd + 1, 0:nq] = jnp.max(s, axis=0, keepdims=True)

    def consume(buf, mx, nq, k0, nk):
        vt = vt_ref[0, :, pl.ds(k0, nk)]
        for hd in range(2):
            m_old = st_ref[hd:hd + 1, 0:nq]
            l_old = st_ref[2 + hd:3 + hd, 0:nq]
            m_new = jnp.maximum(m_old, mx[hd:hd + 1, 0:nq])
            alpha = jnp.exp2(m_old - m_new)
            p = jnp.exp2(buf[hd, 0:nk, 0:nq] - m_new)
            st_ref[hd:hd + 1, 0:nq] = m_new
            st_ref[2 + hd:3 + hd, 0:nq] = alpha * l_old + jnp.sum(p, axis=0, keepdims=True)
            acc_ref[hd, :, 0:nq] = acc_ref[hd, :, 0:nq] * alpha + _dot(vt, p.astype(BF16))

    def finish_qblock(q0, nq):
        o_a = acc_ref[0, 0:half, 0:nq] / st_ref[2:3, 0:nq]
        o_b = acc_ref[1, half:LANE, 0:nq] / st_ref[3:4, 0:nq]
        out_ref[0, pl.ds(q0, nq), :] = jnp.concatenate([o_a, o_b], axis=0).T.astype(out_ref.dtype)

    start_qblock(0, LANE)
    logits_to(bufm, mxm, 0, LANE, 0, LANE, True)
    consume(bufm, mxm, LANE, 0, LANE)
    finish_qblock(0, LANE)

    def qblock(i, carry):
        q0 = pl.multiple_of(LANE + i * blk, LANE)

        def k_of(n):
            return pl.multiple_of(LANE + n * blk, LANE)

        start_qblock(q0, blk)
        logits_to(bufm, mxm, q0, blk, 0, LANE, False)

        @pl.when(i == 0)
        def _():
            logits_to(buf1, mx1, q0, blk, q0, blk, True)
            consume(bufm, mxm, blk, 0, LANE)
            consume(buf1, mx1, blk, q0, blk)

        @pl.when(i > 0)
        def _():
            logits_to(buf0, mx0, q0, blk, k_of(0), blk, False)
            consume(bufm, mxm, blk, 0, LANE)

            def pair(m, c):
                logits_to(buf1, mx1, q0, blk, k_of(2 * m + 1), blk, False)
                consume(buf0, mx0, blk, k_of(2 * m), blk)
                logits_to(buf0, mx0, q0, blk, k_of(2 * m + 2), blk, False)
                consume(buf1, mx1, blk, k_of(2 * m + 1), blk)
                return c

            n_pairs = (i - 1) // 2
            lax.fori_loop(0, n_pairs, pair, 0)
            held = 2 * n_pairs

            @pl.when(i % 2 == 1)
            def _():
                logits_to(buf1, mx1, q0, blk, q0, blk, True)
                consume(buf0, mx0, blk, k_of(held), blk)
                consume(buf1, mx1, blk, q0, blk)

            @pl.when(i % 2 == 0)
            def _():
                logits_to(buf1, mx1, q0, blk, k_of(held + 1), blk, False)
                consume(buf0, mx0, blk, k_of(held), blk)
                logits_to(buf0, mx0, q0, blk, q0, blk, True)
                consume(buf1, mx1, blk, k_of(held + 1), blk)
                consume(buf0, mx0, blk, q0, blk)

        finish_qblock(q0, blk)
        return carry

    lax.fori_loop(0, n_blocks, qblock, 0)


def _attention(q, k, vt, fox, blk):
    bsz, tp, _ = q.shape
    npair = vt.shape[1] // LANE
    qw = LANE if fox else 2 * LANE
    scratch = [pltpu.VMEM((SUBLANE, blk), F32), pltpu.VMEM((2, LANE, blk), F32),
               pltpu.VMEM((2, blk, blk), F32), pltpu.VMEM((2, blk, blk), F32),
               pltpu.VMEM((2, LANE, blk), F32),
               pltpu.VMEM((SUBLANE, blk), F32), pltpu.VMEM((SUBLANE, blk), F32),
               pltpu.VMEM((SUBLANE, blk), F32)]
    if fox:
        scratch.append(pltpu.VMEM((2, blk, 2 * LANE), BF16))
    return pl.pallas_call(
        functools.partial(_attn_kernel, fox, blk),
        out_shape=jax.ShapeDtypeStruct((bsz, tp, npair * LANE), BF16),
        grid=(bsz, npair),
        in_specs=[pl.BlockSpec((1, tp, qw), lambda b, p: (b, 0, p)),
                  pl.BlockSpec((1, tp, 2 * LANE), lambda b, p: (b, 0, p)),
                  pl.BlockSpec((1, LANE, tp), lambda b, p: (b, p, 0))],
        out_specs=pl.BlockSpec((1, tp, LANE), lambda b, p: (b, 0, p)),
        scratch_shapes=scratch,
        compiler_params=_cparams(("parallel", "parallel")),
        name="fox_attn" if fox else "mla_attn",
    )(q, k, vt)


def _place_cols(w, width, offset):
    k, n = w.shape
    return jnp.pad(w, ((0, 0), (offset, width - offset - n)))


def _ev_consts(w_in, b_if):
    nq, nv = A_HEADS * A_DQK, A_HEADS * A_DV
    o = 0
    wq = w_in[:, o:o + nq]; o += nq
    wk = w_in[:, o:o + nq]; o += nq
    wv = w_in[:, o:o + nv]; o += nv
    wo = w_in[:, o:o + nv]; o += nv
    wg = w_in[:, o:o + 2 * A_HEADS]; o += 2 * A_HEADS
    wxb = w_in[:, o:o + 1024]; o += 1024
    wgb = w_in[:, o:o + 1024]
    w = jnp.concatenate([wq, wk, wv, wo, _place_cols(wg, LANE, 0), wxb, wgb], axis=1).astype(BF16)
    bif = _place_cols(b_if[None, :], LANE, 0)
    return w, bif


def _head_tile_cols(w, n_heads, per_head, take, offset):
    k = w.shape[0]
    wh = w.reshape(k, n_heads, per_head)[:, :, take[0]:take[0] + take[1]]
    wh = jnp.pad(wh, ((0, 0), (0, 0), (offset, LANE - offset - take[1])))
    return wh.reshape(k, n_heads * LANE)


def _row128(parts):
    r = jnp.zeros((LANE,), F32)
    for off, v in parts:
        r = r.at[off:off + v.shape[0]].set(v.astype(F32))
    return r[None, :]


def _od_consts(p):
    c = {}
    w_in = p["w_in"]
    o = 0
    wcq = w_in[:, o:o + C_Q_LORA]; o += C_Q_LORA
    wckv = w_in[:, o:o + C_KV_LORA]; o += C_KV_LORA
    wkr = w_in[:, o:o + C_ROPE]; o += C_ROPE
    nf = D_HEADS * D_HD
    wfq = w_in[:, o:o + nf]; o += nf
    wfk = w_in[:, o:o + nf]; o += nf
    wfv = w_in[:, o:o + nf]; o += nf
    wff = w_in[:, o:o + D_HEADS]
    c["w"] = jnp.concatenate([wcq, wckv, _place_cols(wkr, LANE, ROPE_LO), wfq, wfk, wfv,
                              _place_cols(wff, LANE, 0)], axis=1).astype(BF16)
    c["ln"] = p["ln"][None, :]
    freqs = ROPE_THETA ** (-jnp.arange(HALF, dtype=F32) / HALF)
    c["frq"] = _row128([(ROPE_LO, freqs), (ROPE_MID, freqs)])
    c["sgn"] = _row128([(ROPE_LO, -jnp.ones((HALF,), F32)), (ROPE_MID, jnp.ones((HALF,), F32))])
    c["gqa"] = p["g_qa"][None, :]
    c["gkva"] = p["g_kva"][None, :]
    dq = C_NOPE + C_ROPE
    wuq, wukv = p["w_uq"], p["w_ukv"]
    c["wq"] = (_head_tile_cols(wuq, C_HEADS, dq, (0, dq), 0)).astype(BF16)
    c["wkn"] = _head_tile_cols(wukv, C_HEADS, C_NOPE + C_V, (0, C_NOPE), 0).astype(BF16)
    c["wv"] = wukv.reshape(-1, C_HEADS, C_NOPE + C_V)[:, :, C_NOPE:].reshape(-1, C_HEADS * C_V).astype(BF16)
    c["gq"] = _row128([(0, p["g_qn"]), (ROPE_LO, p["g_qr"])])
    c["gk"] = _row128([(0, p["g_kn"])])
    c["gkr"] = _row128([(ROPE_LO, p["g_kr"])])
    c["gfq"] = _row128([(0, p["g_fq"]), (D_HD, p["g_fq"])])
    c["gfk"] = _row128([(0, p["g_fk"]), (D_HD, p["g_fk"])])
    c["bf"] = _row128([(0, p["b_f"])])
    seg = np.where(np.arange(LANE) < C_NOPE, 0, np.where(np.arange(LANE) < ROPE_HI, 1, 2))
    inv_len = np.where(seg == 0, 1.0 / C_NOPE, 1.0 / C_ROPE)[None, :]
    c["gm"] = jnp.asarray((seg[:, None] == seg[None, :]) * inv_len, F32).astype(BF16)
    seg64 = np.arange(LANE) // D_HD
    c["g64"] = jnp.asarray((seg64[:, None] == seg64[None, :]) * (1.0 / D_HD), F32).astype(BF16)
    sel = np.zeros((3, LANE, (D_HEADS // 2) * LANE), np.float32)
    for h in range(D_HEADS):
        for part in range(3):
            sel[part, h, (h // 2) * LANE + 3 * (h % 2) + part] = 1.0
    c["sel"] = jnp.asarray(sel).astype(BF16)
    for nm in ("gq", "gk", "gkr", "gfq", "gfk"):
        c[nm] = jnp.tile(c[nm], (1, 2))
    for nm in ("gm", "g64"):
        z = jnp.zeros_like(c[nm])
        c[nm] = jnp.block([[c[nm], z], [z, c[nm]]])
    return c


def _time_tile(tp):
    return 3 * LANE if tp % (3 * LANE) == 0 else LANE


def kernel(x, positions, meta, ev_ln, ev_w_in, ev_b_if, ev_a_norm, ev_conv_w, ev_conv_b, ev_w_ra, ev_b_ra, ev_w_rx, ev_b_rx, ev_lam, ev_w_out, od_ln, od_w_in, od_b_f, od_g_qa, od_g_kva, od_w_uq, od_w_ukv, od_g_qn, od_g_qr, od_g_kn, od_g_kr, od_g_fq, od_g_fk, od_w_out, mlp_ln, w_ff1, w_ff2):
    bsz, seq, d = x.shape
    tp = LANE + seq
    r = bsz * tp
    tm = 512 if seq % 512 == 0 else LANE
    tt = _time_tile(tp)

    x2 = x.reshape(bsz * seq, d)
    mb = jnp.concatenate([jnp.zeros((PAD, d), x.dtype), meta.astype(x.dtype)], axis=0)
    pos_full = jnp.concatenate([jnp.zeros((bsz, PAD), jnp.int32),
                                jnp.broadcast_to(jnp.arange(N_META, dtype=jnp.int32)[None, :], (bsz, N_META)),
                                positions + N_META], axis=1)
    pos_col = pos_full.astype(F32)[:, :, None]

    e = 0
    w_a, bif = _ev_consts(ev_w_in[e], ev_b_if[e])
    q, k, v, o, g, xb, gb = _ev_in(x2, mb, ev_ln[e][None, :], w_a, bsz, tp, tt)
    ha = _mlstm(q, k, v, o, g, bif, ev_a_norm[e][None, :], bsz, tp, tt)
    hb = _lru(xb, gb, ev_conv_w[e], ev_conv_b[e][None, :], ev_w_ra[e].astype(BF16), ev_b_ra[e][None, :],
              ev_w_rx[e].astype(BF16), ev_b_rx[e][None, :], ev_lam[e][None, :], bsz, tp, tt)
    nv = A_HEADS * A_DV
    wo = ev_w_out[e].astype(BF16)
    h2 = _out_mlp((x2, mb), [ha, hb], [wo[:nv], wo[nv:]], mlp_ln[0][None, :],
                  w_ff1[0].astype(BF16), w_ff2[0].astype(BF16), tt, ("input", bsz, tp, seq))

    oi = 0
    consts = _od_consts(dict(ln=od_ln[oi], w_in=od_w_in[oi], b_f=od_b_f[oi], g_qa=od_g_qa[oi],
                             g_kva=od_g_kva[oi], w_uq=od_w_uq[oi], w_ukv=od_w_ukv[oi],
                             g_qn=od_g_qn[oi], g_qr=od_g_qr[oi], g_kn=od_g_kn[oi], g_kr=od_g_kr[oi],
                             g_fq=od_g_fq[oi], g_fk=od_g_fk[oi]))
    qm, km, vmt, qf, kf, vft = _od_in(h2.reshape(bsz, tp, d), pos_col, consts, tt)
    blk = 512 if seq % 512 == 0 else LANE
    hc = _attention(qm, km, vmt, False, blk)
    hd = _attention(qf, kf, vft, True, blk)
    nc = C_HEADS * C_V
    wo = od_w_out[oi].astype(BF16)
    out = _out_mlp(h2, [hc.reshape(r, nc), hd.reshape(r, nc)], [wo[:nc], wo[nc:]], mlp_ln[1][None, :],
                   w_ff1[1].astype(BF16), w_ff2[1].astype(BF16), tm, ("tokens", bsz, tp, seq))
    return out.reshape(bsz, seq, d)
```

```python
import functools
import math

import numpy as np
import jax
import jax.numpy as jnp
from jax import lax
from jax.experimental import pallas as pl
from jax.experimental.pallas import tpu as pltpu

F32 = jnp.float32
BF16 = jnp.bfloat16

N_META = 16
EPS = 1e-6
A_HEADS = 4
A_DQK = 128
A_DV = 256
B_BLOCKS = 8
B_BLOCK = 128
CONV_W = 4
LRU_C = 8.0
C_HEADS = 8
C_Q_LORA = 384
C_KV_LORA = 256
C_NOPE = 64
C_ROPE = 32
C_V = 64
ROPE_THETA = 10000.0
D_HEADS = 8
D_HD = 64

LANE = 128
SUBLANE = 8
VMEM_LIMIT = 56 * 1024 * 1024

PAD = LANE - N_META
NEG = -1e30
LOG2E = math.log2(math.e)

HALF = C_ROPE // 2
ROPE_LO = C_NOPE
ROPE_MID = C_NOPE + HALF
ROPE_HI = C_NOPE + C_ROPE
MASK_LANE = ROPE_HI


def _cparams(sem):
    return pltpu.CompilerParams(dimension_semantics=sem, vmem_limit_bytes=VMEM_LIMIT)


def _dot(a, b):
    return jnp.dot(a, b, preferred_element_type=F32)


def _dot_nt(a, b):
    return lax.dot_general(a, b, (((1,), (1,)), ((), ())), preferred_element_type=F32)


def _split2(x):
    hi = x.astype(BF16)
    lo = (x - hi.astype(F32)).astype(BF16)
    return hi, lo


def _split3(x):
    hi = x.astype(BF16)
    r = x - hi.astype(F32)
    mid = r.astype(BF16)
    lo = (r - mid.astype(F32)).astype(BF16)
    return hi, mid, lo


def _dot_sel_r(x, sel, parts):
    pieces = _split3(x) if parts == 3 else _split2(x)
    acc = _dot(pieces[0], sel)
    for p in pieces[1:]:
        acc = acc + _dot(p, sel)
    return acc


def _dot_sel_l(sel, x, parts):
    pieces = _split3(x) if parts == 3 else _split2(x)
    acc = _dot(sel, pieces[0])
    for p in pieces[1:]:
        acc = acc + _dot(sel, p)
    return acc


def _rms_rows(x, g):
    ms = jnp.mean(x * x, axis=-1, keepdims=True)
    return x * lax.rsqrt(ms + EPS) * g


def _log_sigmoid(x):
    return -(jnp.maximum(-x, 0.0) + jnp.log1p(jnp.exp(-jnp.abs(x))))


def _sigmoid(x):
    return 0.5 * jnp.tanh(0.5 * x) + 0.5


def _tril_bf16(n):
    r = lax.broadcasted_iota(jnp.int32, (n, n), 0)
    c = lax.broadcasted_iota(jnp.int32, (n, n), 1)
    return jnp.where(r >= c, 1.0, 0.0).astype(BF16)


EV_SLABS = (("q", 512, BF16), ("k", 512, BF16), ("v", 1024, BF16), ("o", 1024, BF16),
            ("g", 128, F32), ("xb", 1024, BF16), ("gb", 1024, BF16))


def _input_tile(x_ref, mb_ref):
    xt = x_ref[...]
    first = jnp.concatenate([mb_ref[...], xt[:xt.shape[0] - LANE]], axis=0)
    return jnp.where(pl.program_id(1) == 0, first, xt)


def _input_specs(seq, tt, d):
    x_spec = pl.BlockSpec(
        (pl.Element(tt), pl.Element(d)),
        lambda b, t: (pl.multiple_of(b * seq + jnp.maximum(t * tt - LANE, 0), LANE), 0))
    return [x_spec, pl.BlockSpec((LANE, d), lambda b, t: (0, 0))]


def _ev_in_kernel(x_ref, mb_ref, ln_ref, w_ref, *out_refs):
    xn = _rms_rows(_input_tile(x_ref, mb_ref), ln_ref[...]).astype(BF16)
    off = 0
    for (_, width, dt), ref in zip(EV_SLABS, out_refs):
        ref[...] = _dot(xn, w_ref[:, off:off + width]).astype(dt)
        off += width


def _ev_in(x2, mb, ln, w, bsz, tp, tt):
    d = x2.shape[1]
    n = w.shape[1]
    nt = tp // tt
    outs = [jax.ShapeDtypeStruct((bsz * tp, width), dt) for (_, width, dt) in EV_SLABS]
    return pl.pallas_call(
        _ev_in_kernel,
        out_shape=outs,
        grid=(bsz, nt),
        in_specs=_input_specs(x2.shape[0] // bsz, tt, d) + [
            pl.BlockSpec((1, d), lambda b, t: (0, 0)),
            pl.BlockSpec((d, n), lambda b, t: (0, 0))],
        out_specs=[pl.BlockSpec((tt, width), lambda b, t: (b * nt + t, 0)) for (_, width, _) in EV_SLABS],
        compiler_params=_cparams(("parallel", "parallel")),
        name="ev_in",
    )(x2, mb, ln, w)


def _mlstm_kernel(q_ref, k_ref, v_ref, o_ref, g_ref, bif_ref, an_ref, out_ref, cn_ref, m_ref):
    c = pl.program_id(1)
    L = q_ref.shape[0]
    dv_ext = A_DV + LANE

    @pl.when(c == 0)
    def _():
        cn_ref[...] = jnp.zeros_like(cn_ref)
        m_ref[...] = jnp.zeros_like(m_ref)

    lane = lax.broadcasted_iota(jnp.int32, (L, LANE), 1)
    row = lax.broadcasted_iota(jnp.int32, (L, LANE), 0)
    valid = (row >= PAD) | (c > 0)
    g = g_ref[...] + bif_ref[...]
    li = jnp.where(valid, g, NEG)
    lf = jnp.where(valid, _log_sigmoid(g), 0.0)
    is_f = (lane >= A_HEADS) & (lane < 2 * A_HEADS)
    lf = jnp.where(is_f, lf, 0.0)
    tril = _tril_bf16(L)
    bcum = _dot_sel_l(tril, lf, 3)
    li_t = li.T
    bcum_t = bcum.T

    rr = lax.broadcasted_iota(jnp.int32, (L, L), 0)
    cc = lax.broadcasted_iota(jnp.int32, (L, L), 1)
    causal = rr >= cc
    e0 = jnp.where(lane == 0, 1.0, 0.0).astype(BF16)
    scale = A_DQK ** -0.5

    heads = []
    for h in range(A_HEADS):
        qs = (q_ref[:, h * A_DQK:(h + 1) * A_DQK].astype(F32) * scale).astype(BF16)
        kk = k_ref[:, h * A_DQK:(h + 1) * A_DQK]
        vext = jnp.concatenate([v_ref[:, h * A_DV:(h + 1) * A_DV], e0], axis=1)
        i_col = li[:, h:h + 1]
        b_col = bcum[:, A_HEADS + h:A_HEADS + h + 1]
        i_row = li_t[h:h + 1, :]
        b_row = bcum_t[A_HEADS + h:A_HEADS + h + 1, :]
        m_state = m_ref[:, h:h + 1]
        cn = cn_ref[h]
        qk = _dot_nt(qs, kk)
        qc = _dot(qs, cn.astype(BF16))

        gsum = b_col[L - 1:L, :]
        dk_col = gsum - b_col + i_col
        dk_row = gsum - b_row + i_row
        m_new = jnp.maximum(gsum + m_state, jnp.max(dk_row, axis=1, keepdims=True))
        wk = jnp.exp(dk_col - m_new)
        sc = jnp.exp(gsum + m_state - m_new)
        kw_t = (kk.astype(F32) * wk).T.astype(BF16)
        cn_ref[h] = sc * cn + _dot(kw_t, vext)
        m_ref[:, h:h + 1] = m_new
        heads.append((qk, qc, vext, b_col, b_row, i_row, m_state))

    weighted = []
    for qk, qc, vext, b_col, b_row, i_row, m_state in heads:
        dmat = jnp.where(causal, b_col - b_row + i_row, NEG)
        inter = b_col + m_state
        m_row = jnp.maximum(inter, jnp.max(dmat, axis=1, keepdims=True))
        w_intra = jnp.exp(dmat - m_row)
        w_inter = jnp.exp(inter - m_row)
        s = (qk * w_intra).astype(BF16)
        weighted.append((w_inter * qc + _dot(s, vext), m_row))

    for h, (numden, m_row) in enumerate(weighted):
        num = numden[:, :A_DV]
        den = numden[:, A_DV:A_DV + 1]
        hh = num / jnp.maximum(jnp.abs(den), jnp.exp(-m_row))
        hn = _rms_rows(hh, an_ref[...])
        og = _sigmoid(o_ref[:, h * A_DV:(h + 1) * A_DV].astype(F32))
        out_ref[:, h * A_DV:(h + 1) * A_DV] = (hn * og).astype(out_ref.dtype)


def _mlstm(q, k, v, o, g, bif, an, bsz, tp, L):
    nq, nv = A_HEADS * A_DQK, A_HEADS * A_DV
    nc = tp // L
    row_map = lambda b, c: (b * nc + c, 0)
    const = lambda b, c: (0, 0)
    return pl.pallas_call(
        _mlstm_kernel,
        out_shape=jax.ShapeDtypeStruct((bsz * tp, nv), BF16),
        grid=(bsz, nc),
        in_specs=[pl.BlockSpec((L, nq), row_map), pl.BlockSpec((L, nq), row_map),
                  pl.BlockSpec((L, nv), row_map), pl.BlockSpec((L, nv), row_map),
                  pl.BlockSpec((L, LANE), row_map),
                  pl.BlockSpec((1, LANE), const), pl.BlockSpec((1, A_DV), const)],
        out_specs=pl.BlockSpec((L, nv), row_map),
        scratch_shapes=[pltpu.VMEM((A_HEADS, A_DQK, A_DV + LANE), F32),
                        pltpu.VMEM((1, LANE), F32)],
        compiler_params=_cparams(("parallel", "arbitrary")),
        name="mlstm",
    )(q, k, v, o, g, bif, an)


def _gelu_tanh(x):
    c0 = math.sqrt(2.0 / math.pi)
    return x * (0.5 * jnp.tanh(x * (c0 + (c0 * 0.044715) * (x * x))) + 0.5)


def _lru_kernel(xb_ref, gb_ref, cw_ref, cb_ref, wra_ref, bra_ref, wrx_ref, brx_ref, lam_ref,
                out_ref, hist_ref, hc_ref, a_s, u_s):
    t = pl.program_id(1)
    tt, w = xb_ref.shape

    @pl.when(t == 0)
    def _():
        hist_ref[...] = jnp.zeros_like(hist_ref)
        hc_ref[...] = jnp.zeros_like(hc_ref)

    row = lax.broadcasted_iota(jnp.int32, (tt, w), 0)
    valid = (row >= PAD) | (t > 0)
    x = jnp.where(valid, xb_ref[...].astype(F32), 0.0)
    xfull = jnp.concatenate([hist_ref[...], x], axis=0)
    conv = cb_ref[...] + cw_ref[CONV_W - 1:CONV_W, :] * x
    for d in range(1, CONV_W):
        shifted = pltpu.roll(xfull, d, 0)[SUBLANE:, :]
        conv = conv + cw_ref[CONV_W - 1 - d:CONV_W - d, :] * shifted
    hist_ref[...] = x[tt - SUBLANE:, :]

    xcb = conv.astype(BF16)
    r_parts, i_parts = [], []
    for blk in range(B_BLOCKS):
        xs = xcb[:, blk * B_BLOCK:(blk + 1) * B_BLOCK]
        r_parts.append(_dot(xs, wra_ref[blk]))
        i_parts.append(_dot(xs, wrx_ref[blk]))
    r = _sigmoid(jnp.concatenate(r_parts, axis=1) + bra_ref[...])
    ig = _sigmoid(jnp.concatenate(i_parts, axis=1) + brx_ref[...])
    nlam = -lam_ref[...]
    softplus = jnp.maximum(nlam, 0.0) + jnp.log1p(jnp.exp(-jnp.abs(nlam)))
    a = jnp.exp2(((-LRU_C * LOG2E) * softplus) * r)
    y = 1.0 - a * a
    u = jnp.where(y > 0.0, y * lax.rsqrt(y), 0.0) * (ig * conv)
    a_s[...] = a
    u_s[...] = jnp.where(valid, u, 0.0)

    srow = lax.broadcasted_iota(jnp.int32, (SUBLANE, w), 0)

    def group(gi, hprev):
        sl = pl.ds(pl.multiple_of(gi * SUBLANE, SUBLANE), SUBLANE)
        ag = a_s[sl, :]
        ug = u_s[sl, :]
        for d in (1, 2, 4):
            ok = srow >= d
            a_sh = jnp.where(ok, pltpu.roll(ag, d, 0), 1.0)
            u_sh = jnp.where(ok, pltpu.roll(ug, d, 0), 0.0)
            ug = ag * u_sh + ug
            ag = ag * a_sh
        hg = ug + ag * hprev
        u_s[sl, :] = hg
        return jnp.broadcast_to(hg[SUBLANE - 1:SUBLANE, :], (SUBLANE, w))

    hc_ref[...] = lax.fori_loop(0, tt // SUBLANE, group, hc_ref[...])
    out_ref[...] = (u_s[...] * _gelu_tanh(gb_ref[...].astype(F32))).astype(out_ref.dtype)


def _lru(xb, gb, cw, cb, wra, bra, wrx, brx, lam, bsz, tp, tt):
    w = xb.shape[1]
    nt = tp // tt
    row_map = lambda b, t: (b * nt + t, 0)
    const2 = lambda b, t: (0, 0)
    const3 = lambda b, t: (0, 0, 0)
    return pl.pallas_call(
        _lru_kernel,
        out_shape=jax.ShapeDtypeStruct(xb.shape, BF16),
        grid=(bsz, nt),
        in_specs=[pl.BlockSpec((tt, w), row_map), pl.BlockSpec((tt, w), row_map),
                  pl.BlockSpec((CONV_W, w), const2), pl.BlockSpec((1, w), const2),
                  pl.BlockSpec((B_BLOCKS, B_BLOCK, B_BLOCK), const3), pl.BlockSpec((1, w), const2),
                  pl.BlockSpec((B_BLOCKS, B_BLOCK, B_BLOCK), const3), pl.BlockSpec((1, w), const2),
                  pl.BlockSpec((1, w), const2)],
        out_specs=pl.BlockSpec((tt, w), row_map),
        scratch_shapes=[pltpu.VMEM((SUBLANE, w), F32), pltpu.VMEM((SUBLANE, w), F32),
                        pltpu.VMEM((tt, w), F32), pltpu.VMEM((tt, w), F32)],
        compiler_params=_cparams(("parallel", "arbitrary")),
        name="conv_rglru",
    )(xb, gb, cw, cb, wra, bra, wrx, brx, lam)


def _out_mlp_kernel(n_mix, from_input, *refs):
    n_h = 2 if from_input else 1
    h_refs, refs = refs[:n_h], refs[n_h:]
    mix_refs = refs[:n_mix]
    ln_ref = refs[n_mix]
    n_w = n_mix + 2
    w_hbm = refs[n_mix + 1:n_mix + 1 + n_w]
    out_ref = refs[n_mix + 1 + n_w]
    w_vmem = refs[n_mix + 2 + n_w:n_mix + 2 + 2 * n_w]
    sem = refs[n_mix + 2 + 2 * n_w]

    @pl.when((pl.program_id(0) == 0) & (pl.program_id(1) == 0))
    def _():
        copies = [pltpu.make_async_copy(src, dst, sem.at[i])
                  for i, (src, dst) in enumerate(zip(w_hbm, w_vmem))]
        for cp in copies:
            cp.start()
        for cp in copies:
            cp.wait()

    wo_refs, w1_ref, w2_ref = w_vmem[:n_mix], w_vmem[n_mix], w_vmem[n_mix + 1]
    res = _input_tile(*h_refs) if from_input else h_refs[0][...]
    for m_ref, w_ref in zip(mix_refs, wo_refs):
        res = res + _dot(m_ref[...], w_ref[...])
    hn = _rms_rows(res, ln_ref[...]).astype(BF16)
    a = jnp.maximum(_dot(hn, w1_ref[...]), 0.0)
    out_ref[...] = res + _dot((a * a).astype(BF16), w2_ref[...])


def _out_mlp(h_in, mixes, wouts, ln, w1, w2, tm, rows):
    d = w2.shape[1]
    n_mix = len(mixes)
    kind = rows[0]
    if kind == "all":
        r = rows[1]
        grid, out_rows = (r // tm, 1), r
        mix_spec = lambda n: pl.BlockSpec((tm, n), lambda i, j: (i, 0))
        h_specs, h_args = [mix_spec(d)], [h_in]
        out_spec = pl.BlockSpec((tm, d), lambda i, j: (i, 0))
    elif kind == "tokens":
        _, bsz, tp, seq = rows
        per_seq = seq // tm
        grid, out_rows = (bsz, per_seq), bsz * seq
        mix_spec = lambda n: pl.BlockSpec(
            (pl.Element(tm), pl.Element(n)),
            lambda b, i: (pl.multiple_of(b * tp + (tp - seq) + i * tm, LANE), 0))
        h_specs, h_args = [mix_spec(d)], [h_in]
        out_spec = pl.BlockSpec((tm, d), lambda b, i: (b * per_seq + i, 0))
    else:
        _, bsz, tp, seq = rows
        nt = tp // tm
        grid, out_rows = (bsz, nt), bsz * tp
        mix_spec = lambda n: pl.BlockSpec((tm, n), lambda b, t: (b * nt + t, 0))
        h_specs, h_args = _input_specs(seq, tm, d), list(h_in)
        out_spec = mix_spec(d)
    weights = [*wouts, w1, w2]
    in_specs = h_specs + [mix_spec(m.shape[1]) for m in mixes]
    in_specs += [pl.BlockSpec((1, d), lambda i, j: (0, 0))]
    in_specs += [pl.BlockSpec(memory_space=pl.ANY) for _ in weights]
    scratch = [pltpu.VMEM(w.shape, w.dtype) for w in weights]
    scratch.append(pltpu.SemaphoreType.DMA((len(weights),)))
    return pl.pallas_call(
        functools.partial(_out_mlp_kernel, n_mix, kind == "input"),
        out_shape=jax.ShapeDtypeStruct((out_rows, d), F32),
        grid=grid,
        in_specs=in_specs,
        out_specs=out_spec,
        scratch_shapes=scratch,
        compiler_params=_cparams(("arbitrary", "arbitrary")),
        name="out_mlp",
    )(*h_args, *mixes, ln, *weights)


OD_COLS = dict(cq=(0, 384), ckv=(384, 640), kr=(640, 768), fq=(768, 1280), fk=(1280, 1792),
               fv=(1792, 2304), ff=(2304, 2432))
OD_N = 2432


def _od_in_kernel(h_ref, ln_ref, w_ref, pos_ref, frq_ref, sgn_ref, gqa_ref, gkva_ref,
                  wq_ref, wkn_ref, wv_ref, gq_ref, gk_ref, gkr_ref, gfq_ref, gfk_ref, bf_ref,
                  gm_ref, g64_ref, sel_ref,
                  qm_ref, km_ref, vmt_ref, qf_ref, kf_ref, vft_ref, fc_ref):
    t = pl.program_id(1)
    tm = h_ref.shape[1]

    @pl.when(t == 0)
    def _():
        fc_ref[...] = jnp.zeros_like(fc_ref)

    xn = _rms_rows(h_ref[0], ln_ref[...]).astype(BF16)

    def proj(name):
        a, b = OD_COLS[name]
        return _dot(xn, w_ref[:, a:b])

    wide = 2 * LANE
    lane2 = lax.broadcasted_iota(jnp.int32, (tm, wide), 1) % LANE
    row2 = lax.broadcasted_iota(jnp.int32, (tm, wide), 0)
    unused_row2 = (row2 < PAD) & (t == 0)
    mask_lane = lane2 == MASK_LANE
    ang = pos_ref[0] * frq_ref[...]
    cs = jnp.cos(ang)
    sn = jnp.sin(ang) * sgn_ref[...]
    cs = jnp.concatenate([cs, cs], axis=1)
    sn = jnp.concatenate([sn, sn], axis=1)
    first_half = lane2 < ROPE_MID

    def seg_rms(x, sel, gain):
        ms = _dot((x * x).astype(BF16), sel)
        return x * lax.rsqrt(ms + EPS) * gain

    def rope(y):
        sw = jnp.where(first_half, pltpu.roll(y, wide - HALF, 1), pltpu.roll(y, HALF, 1))
        return y * cs + sw * sn

    gm = gm_ref[...]
    g64 = g64_ref[...]

    cqn = _rms_rows(proj("cq"), gqa_ref[...]).astype(BF16)
    qall = _dot(cqn, wq_ref[...])
    scale_m = (C_NOPE + C_ROPE) ** -0.5 * LOG2E
    for h2 in range(C_HEADS // 2):
        cols = slice(h2 * wide, (h2 + 1) * wide)
        y = rope(seg_rms(qall[:, cols], gm,gq_ref[...]))
        qm_ref[0, :, cols] = jnp.where(mask_lane, 1.0, y * scale_m).astype(BF16)
    ckvn = _rms_rows(proj("ckv"), gkva_ref[...]).astype(BF16)
    knall = _dot(ckvn, wkn_ref[...])
    kr = proj("kr")
    krt = rope(seg_rms(jnp.concatenate([kr, kr], axis=1), gm,gkr_ref[...]))
    krt = jnp.where(mask_lane & unused_row2, NEG, krt)
    for h2 in range(C_HEADS // 2):
        cols = slice(h2 * wide, (h2 + 1) * wide)
        y = seg_rms(knall[:, cols], gm,gk_ref[...]) + krt
        km_ref[0, :, cols] = y.astype(BF16)
    vmt_ref[0] = _dot(ckvn, wv_ref[...]).T.astype(BF16)

    fq = proj("fq")
    fk = proj("fk")
    scale_f = D_HD ** -0.5 * LOG2E
    npair = D_HEADS // 2
    for p2 in range(npair // 2):
        cols = slice(p2 * wide, (p2 + 1) * wide)
        y = seg_rms(fq[:, cols], g64,gfq_ref[...])
        qf_ref[0, :, cols] = (y * scale_f).astype(BF16)
        y = seg_rms(fk[:, cols], g64,gfk_ref[...]).astype(BF16)
        for e in range(2):
            p = 2 * p2 + e
            kf_ref[0, :, 2 * p * LANE:(2 * p + 1) * LANE] = y[:, e * LANE:(e + 1) * LANE]
    vft_ref[0] = proj("fv").T.astype(BF16)

    lane = lax.broadcasted_iota(jnp.int32, (tm, LANE), 1)
    unused_row = (lax.broadcasted_iota(jnp.int32, (tm, LANE), 0) < PAD) & (t == 0)
    valid = jnp.logical_not(unused_row) & (lane < D_HEADS)
    lf = jnp.where(valid, _log_sigmoid(proj("ff") + bf_ref[...]), 0.0)
    fc = _dot_sel_l(_tril_bf16(tm), lf, 3) + fc_ref[...]
    fc_ref[...] = fc[tm - 1:tm, :]
    hi, mid, lo = _split3(jnp.where(unused_row, NEG, fc * (-LOG2E)))
    bias = _dot(hi, sel_ref[0]) + _dot(mid, sel_ref[1]) + _dot(lo, sel_ref[2])
    for p in range(npair):
        kf_ref[0, :, (2 * p + 1) * LANE:(2 * p + 2) * LANE] = bias[:, p * LANE:(p + 1) * LANE].astype(BF16)


def _od_in(h3, pos_col, consts, tm):
    bsz, tp, d = h3.shape
    nt = tp // tm
    names = ["ln", "w", None, "frq", "sgn", "gqa", "gkva", "wq", "wkn", "wv", "gq", "gk", "gkr",
             "gfq", "gfk", "bf", "gm", "g64", "sel"]
    args, in_specs = [h3], [pl.BlockSpec((1, tm, d), lambda b, t: (b, t, 0))]
    for nm in names:
        if nm is None:
            args.append(pos_col)
            in_specs.append(pl.BlockSpec((1, tm, 1), lambda b, t: (b, t, 0)))
            continue
        a = consts[nm]
        args.append(a)
        in_specs.append(pl.BlockSpec(a.shape, (lambda nd: (lambda b, t: (0,) * nd))(a.ndim)))
    hl = C_HEADS * LANE
    pk = D_HEADS * D_HD
    outs = [jax.ShapeDtypeStruct((bsz, tp, hl), BF16), jax.ShapeDtypeStruct((bsz, tp, hl), BF16),
            jax.ShapeDtypeStruct((bsz, pk, tp), BF16), jax.ShapeDtypeStruct((bsz, tp, pk), BF16),
            jax.ShapeDtypeStruct((bsz, tp, 2 * pk), BF16), jax.ShapeDtypeStruct((bsz, pk, tp), BF16)]
    rowspec = lambda n: pl.BlockSpec((1, tm, n), lambda b, t: (b, t, 0))
    tspec = pl.BlockSpec((1, pk, tm), lambda b, t: (b, 0, t))
    return pl.pallas_call(
        _od_in_kernel,
        out_shape=outs,
        grid=(bsz, nt),
        in_specs=in_specs,
        out_specs=[rowspec(hl), rowspec(hl), tspec, rowspec(pk), rowspec(2 * pk), tspec],
        scratch_shapes=[pltpu.VMEM((1, LANE), F32)],
        compiler_params=_cparams(("parallel", "arbitrary")),
        name="od_in",
    )(*args)


def _attn_kernel(fox, blk, nh, q_ref, k_ref, vt_ref, out_ref, st_ref, acc_ref, buf0, buf1, bufm,
                 mx0, mx1, mxm, *maybe_qe):
    tp = k_ref.shape[1]
    n_blocks = (tp - LANE) // blk
    half = LANE // 2

    def start_qblock(q0, nq):
        st_ref[0:nh, 0:nq] = jnp.full((nh, nq), NEG, F32)
        st_ref[nh:2 * nh, 0:nq] = jnp.zeros((nh, nq), F32)
        acc_ref[:, :, 0:nq] = jnp.zeros((nh, LANE, nq), F32)
        if fox:
            qe_ref = maybe_qe[0]
            lane = lax.broadcasted_iota(jnp.int32, (nq, LANE), 1)
            for hd in range(nh):
                pair, e = divmod(hd, 2)
                q = q_ref[0, pl.ds(q0, nq), pair * LANE:(pair + 1) * LANE]
                mine = (lane >= e * half) & (lane < (e + 1) * half)
                ones = ((lane >= 3 * e) & (lane < 3 * e + 3)).astype(F32).astype(BF16)
                qe_ref[hd, 0:nq, :] = jnp.concatenate([jnp.where(mine, q, jnp.zeros_like(q)), ones], axis=1)

    def logits_to(buf, mx, q0, nq, k0, nk, diagonal):
        if diagonal:
            ok = (lax.broadcasted_iota(jnp.int32, (nk, nq), 0)
                  <= lax.broadcasted_iota(jnp.int32, (nk, nq), 1))
        for hd in range(nh):
            if fox:
                pair = hd // 2
                s = _dot_nt(k_ref[0, pl.ds(k0, nk), pair * 2 * LANE:(pair + 1) * 2 * LANE],
                            maybe_qe[0][hd, 0:nq, :])
            else:
                s = _dot_nt(k_ref[0, pl.ds(k0, nk), hd * LANE:(hd + 1) * LANE],
                            q_ref[0, pl.ds(q0, nq), hd * LANE:(hd + 1) * LANE])
            if diagonal:
                s = jnp.where(ok, s, NEG)
            buf[hd, 0:nk, 0:nq] = s
            mx[hd:hd + 1, 0:nq] = jnp.max(s, axis=0, keepdims=True)

    def consume(buf, mx, nq, k0, nk):
        for hd in range(nh):
            pair = hd // 2
            vt = vt_ref[0, pair * LANE:(pair + 1) * LANE, pl.ds(k0, nk)]
            m_old = st_ref[hd:hd + 1, 0:nq]
            l_old = st_ref[nh + hd:nh + hd + 1, 0:nq]
            m_new = jnp.maximum(m_old, mx[hd:hd + 1, 0:nq])
            alpha = jnp.exp2(m_old - m_new)
            p = jnp.exp2(buf[hd, 0:nk, 0:nq] - m_new)
            st_ref[hd:hd + 1, 0:nq] = m_new
            st_ref[nh + hd:nh + hd + 1, 0:nq] = alpha * l_old + jnp.sum(p, axis=0, keepdims=True)
            acc_ref[hd, :, 0:nq] = acc_ref[hd, :, 0:nq] * alpha + _dot(vt, p.astype(BF16))

    def finish_qblock(q0, nq):
        for pair in range(nh // 2):
            a, b = 2 * pair, 2 * pair + 1
            o_a = acc_ref[a, 0:half, 0:nq] / st_ref[nh + a:nh + a + 1, 0:nq]
            o_b = acc_ref[b, half:LANE, 0:nq] / st_ref[nh + b:nh + b + 1, 0:nq]
            out_ref[0, pl.ds(q0, nq), pair * LANE:(pair + 1) * LANE] = (
                jnp.concatenate([o_a, o_b], axis=0).T.astype(out_ref.dtype))

    start_qblock(0, LANE)
    logits_to(bufm, mxm, 0, LANE, 0, LANE, True)
    consume(bufm, mxm, LANE, 0, LANE)
    finish_qblock(0, LANE)

    def qblock(i, carry):
        q0 = pl.multiple_of(LANE + i * blk, LANE)

        def k_of(n):
            return pl.multiple_of(LANE + n * blk, LANE)

        start_qblock(q0, blk)
        logits_to(bufm, mxm, q0, blk, 0, LANE, False)

        @pl.when(i == 0)
        def _():
            logits_to(buf1, mx1, q0, blk, q0, blk, True)
            consume(bufm, mxm, blk, 0, LANE)
            consume(buf1, mx1, blk, q0, blk)

        @pl.when(i > 0)
        def _():
            logits_to(buf0, mx0, q0, blk, k_of(0), blk, False)
            consume(bufm, mxm, blk, 0, LANE)

            def pair(m, c):
                logits_to(buf1, mx1, q0, blk, k_of(2 * m + 1), blk, False)
                consume(buf0, mx0, blk, k_of(2 * m), blk)
                logits_to(buf0, mx0, q0, blk, k_of(2 * m + 2), blk, False)
                consume(buf1, mx1, blk, k_of(2 * m + 1), blk)
                return c

            n_pairs = (i - 1) // 2
            lax.fori_loop(0, n_pairs, pair, 0)
            held = 2 * n_pairs

            @pl.when(i % 2 == 1)
            def _():
                logits_to(buf1, mx1, q0, blk, q0, blk, True)
                consume(buf0, mx0, blk, k_of(held), blk)
                consume(buf1, mx1, blk, q0, blk)

            @pl.when(i % 2 == 0)
            def _():
                logits_to(buf1, mx1, q0, blk, k_of(held + 1), blk, False)
                consume(buf0, mx0, blk, k_of(held), blk)
                logits_to(buf0, mx0, q0, blk, q0, blk, True)
                consume(buf1, mx1, blk, k_of(held + 1), blk)
                consume(buf0, mx0, blk, q0, blk)

        finish_qblock(q0, blk)
        return carry

    lax.fori_loop(0, n_blocks, qblock, 0)


def _attention(q, k, vt, fox, blk, nh):
    bsz, tp, _ = q.shape
    npair = vt.shape[1] // LANE
    g = nh // 2
    qw = (LANE if fox else 2 * LANE) * g
    assert 2 * nh <= SUBLANE and npair % g == 0
    scratch = [pltpu.VMEM((SUBLANE, blk), F32), pltpu.VMEM((nh, LANE, blk), F32),
               pltpu.VMEM((nh, blk, blk), F32), pltpu.VMEM((nh, blk, blk), F32),
               pltpu.VMEM((nh, LANE, blk), F32),
               pltpu.VMEM((SUBLANE, blk), F32), pltpu.VMEM((SUBLANE, blk), F32),
               pltpu.VMEM((SUBLANE, blk), F32)]
    if fox:
        scratch.append(pltpu.VMEM((nh, blk, 2 * LANE), BF16))
    return pl.pallas_call(
        functools.partial(_attn_kernel, fox, blk, nh),
        out_shape=jax.ShapeDtypeStruct((bsz, tp, npair * LANE), BF16),
        grid=(bsz, npair // g),
        in_specs=[pl.BlockSpec((1, tp, qw), lambda b, p: (b, 0, p)),
                  pl.BlockSpec((1, tp, 2 * LANE * g), lambda b, p: (b, 0, p)),
                  pl.BlockSpec((1, LANE * g, tp), lambda b, p: (b, p, 0))],
        out_specs=pl.BlockSpec((1, tp, LANE * g), lambda b, p: (b, 0, p)),
        scratch_shapes=scratch,
        compiler_params=_cparams(("parallel", "parallel")),
        name="fox_attn" if fox else "mla_attn",
    )(q, k, vt)


def _place_cols(w, width, offset):
    k, n = w.shape
    return jnp.pad(w, ((0, 0), (offset, width - offset - n)))


def _ev_consts(w_in, b_if):
    nq, nv = A_HEADS * A_DQK, A_HEADS * A_DV
    o = 0
    wq = w_in[:, o:o + nq]; o += nq
    wk = w_in[:, o:o + nq]; o += nq
    wv = w_in[:, o:o + nv]; o += nv
    wo = w_in[:, o:o + nv]; o += nv
    wg = w_in[:, o:o + 2 * A_HEADS]; o += 2 * A_HEADS
    wxb = w_in[:, o:o + 1024]; o += 1024
    wgb = w_in[:, o:o + 1024]
    w = jnp.concatenate([wq, wk, wv, wo, _place_cols(wg, LANE, 0), wxb, wgb], axis=1).astype(BF16)
    bif = _place_cols(b_if[None, :], LANE, 0)
    return w, bif


def _head_tile_cols(w, n_heads, per_head, take, offset):
    k = w.shape[0]
    wh = w.reshape(k, n_heads, per_head)[:, :, take[0]:take[0] + take[1]]
    wh = jnp.pad(wh, ((0, 0), (0, 0), (offset, LANE - offset - take[1])))
    return wh.reshape(k, n_heads * LANE)


def _row128(parts):
    r = jnp.zeros((LANE,), F32)
    for off, v in parts:
        r = r.at[off:off + v.shape[0]].set(v.astype(F32))
    return r[None, :]


def _od_consts(p):
    c = {}
    w_in = p["w_in"]
    o = 0
    wcq = w_in[:, o:o + C_Q_LORA]; o += C_Q_LORA
    wckv = w_in[:, o:o + C_KV_LORA]; o += C_KV_LORA
    wkr = w_in[:, o:o + C_ROPE]; o += C_ROPE
    nf = D_HEADS * D_HD
    wfq = w_in[:, o:o + nf]; o += nf
    wfk = w_in[:, o:o + nf]; o += nf
    wfv = w_in[:, o:o + nf]; o += nf
    wff = w_in[:, o:o + D_HEADS]
    c["w"] = jnp.concatenate([wcq, wckv, _place_cols(wkr, LANE, ROPE_LO), wfq, wfk, wfv,
                              _place_cols(wff, LANE, 0)], axis=1).astype(BF16)
    c["ln"] = p["ln"][None, :]
    freqs = ROPE_THETA ** (-jnp.arange(HALF, dtype=F32) / HALF)
    c["frq"] = _row128([(ROPE_LO, freqs), (ROPE_MID, freqs)])
    c["sgn"] = _row128([(ROPE_LO, -jnp.ones((HALF,), F32)), (ROPE_MID, jnp.ones((HALF,), F32))])
    c["gqa"] = p["g_qa"][None, :]
    c["gkva"] = p["g_kva"][None, :]
    dq = C_NOPE + C_ROPE
    wuq, wukv = p["w_uq"], p["w_ukv"]
    c["wq"] = (_head_tile_cols(wuq, C_HEADS, dq, (0, dq), 0)).astype(BF16)
    c["wkn"] = _head_tile_cols(wukv, C_HEADS, C_NOPE + C_V, (0, C_NOPE), 0).astype(BF16)
    c["wv"] = wukv.reshape(-1, C_HEADS, C_NOPE + C_V)[:, :, C_NOPE:].reshape(-1, C_HEADS * C_V).astype(BF16)
    c["gq"] = _row128([(0, p["g_qn"]), (ROPE_LO, p["g_qr"])])
    c["gk"] = _row128([(0, p["g_kn"])])
    c["gkr"] = _row128([(ROPE_LO, p["g_kr"])])
    c["gfq"] = _row128([(0, p["g_fq"]), (D_HD, p["g_fq"])])
    c["gfk"] = _row128([(0, p["g_fk"]), (D_HD, p["g_fk"])])
    c["bf"] = _row128([(0, p["b_f"])])
    seg = np.where(np.arange(LANE) < C_NOPE, 0, np.where(np.arange(LANE) < ROPE_HI, 1, 2))
    inv_len = np.where(seg == 0, 1.0 / C_NOPE, 1.0 / C_ROPE)[None, :]
    c["gm"] = jnp.asarray((seg[:, None] == seg[None, :]) * inv_len, F32).astype(BF16)
    seg64 = np.arange(LANE) // D_HD
    c["g64"] = jnp.asarray((seg64[:, None] == seg64[None, :]) * (1.0 / D_HD), F32).astype(BF16)
    sel = np.zeros((3, LANE, (D_HEADS // 2) * LANE), np.float32)
    for h in range(D_HEADS):
        for part in range(3):
            sel[part, h, (h // 2) * LANE + 3 * (h % 2) + part] = 1.0
    c["sel"] = jnp.asarray(sel).astype(BF16)
    for nm in ("gq", "gk", "gkr", "gfq", "gfk"):
        c[nm] = jnp.tile(c[nm], (1, 2))
    for nm in ("gm", "g64"):
        z = jnp.zeros_like(c[nm])
        c[nm] = jnp.block([[c[nm], z], [z, c[nm]]])
    return c


def _time_tile(tp):
    return 3 * LANE if tp % (3 * LANE) == 0 else LANE


def kernel(x, positions, meta, ev_ln, ev_w_in, ev_b_if, ev_a_norm, ev_conv_w, ev_conv_b, ev_w_ra, ev_b_ra, ev_w_rx, ev_b_rx, ev_lam, ev_w_out, od_ln, od_w_in, od_b_f, od_g_qa, od_g_kva, od_w_uq, od_w_ukv, od_g_qn, od_g_qr, od_g_kn, od_g_kr, od_g_fq, od_g_fk, od_w_out, mlp_ln, w_ff1, w_ff2):
    bsz, seq, d = x.shape
    tp = LANE + seq
    r = bsz * tp
    tm = 512 if seq % 512 == 0 else LANE
    tt = _time_tile(tp)

    x2 = x.reshape(bsz * seq, d)
    mb = jnp.concatenate([jnp.zeros((PAD, d), x.dtype), meta.astype(x.dtype)], axis=0)
    pos_full = jnp.concatenate([jnp.zeros((bsz, PAD), jnp.int32),
                                jnp.broadcast_to(jnp.arange(N_META, dtype=jnp.int32)[None, :], (bsz, N_META)),
                                positions + N_META], axis=1)
    pos_col = pos_full.astype(F32)[:, :, None]

    e = 0
    w_a, bif = _ev_consts(ev_w_in[e], ev_b_if[e])
    q, k, v, o, g, xb, gb = _ev_in(x2, mb, ev_ln[e][None, :], w_a, bsz, tp, tt)
    ha = _mlstm(q, k, v, o, g, bif, ev_a_norm[e][None, :], bsz, tp, tt)
    hb = _lru(xb, gb, ev_conv_w[e], ev_conv_b[e][None, :], ev_w_ra[e].astype(BF16), ev_b_ra[e][None, :],
              ev_w_rx[e].astype(BF16), ev_b_rx[e][None, :], ev_lam[e][None, :], bsz, tp, tt)
    nv = A_HEADS * A_DV
    wo = ev_w_out[e].astype(BF16)
    h2 = _out_mlp((x2, mb), [ha, hb], [wo[:nv], wo[nv:]], mlp_ln[0][None, :],
                  w_ff1[0].astype(BF16), w_ff2[0].astype(BF16), tt, ("input", bsz, tp, seq))

    oi = 0
    consts = _od_consts(dict(ln=od_ln[oi], w_in=od_w_in[oi], b_f=od_b_f[oi], g_qa=od_g_qa[oi],
                             g_kva=od_g_kva[oi], w_uq=od_w_uq[oi], w_ukv=od_w_ukv[oi],
                             g_qn=od_g_qn[oi], g_qr=od_g_qr[oi], g_kn=od_g_kn[oi], g_kr=od_g_kr[oi],
                             g_fq=od_g_fq[oi], g_fk=od_g_fk[oi]))
    qm, km, vmt, qf, kf, vft = _od_in(h2.reshape(bsz, tp, d), pos_col, consts, tt)
    blk = 512 if seq % 512 == 0 else LANE
    hc = _attention(qm, km, vmt, False, blk, 4)
    hd = _attention(qf, kf, vft, True, blk, 4)
    nc = C_HEADS * C_V
    wo = od_w_out[oi].astype(BF16)
    out = _out_mlp(h2, [hc.reshape(r, nc), hd.reshape(r, nc)], [wo[:nc], wo[nc:]], mlp_ln[1][None, :],
                   w_ff1[1].astype(BF16), w_ff2[1].astype(BF16), tm, ("tokens", bsz, tp, seq))
    return out.reshape(bsz, seq, d)
```

```python
import functools
import math

import numpy as np
import jax
import jax.numpy as jnp
from jax import lax
from jax.experimental import pallas as pl
from jax.experimental.pallas import tpu as pltpu

F32 = jnp.float32
BF16 = jnp.bfloat16

N_META = 16
EPS = 1e-6
A_HEADS = 4
A_DQK = 128
A_DV = 256
B_BLOCKS = 8
B_BLOCK = 128
CONV_W = 4
LRU_C = 8.0
C_HEADS = 8
C_Q_LORA = 384
C_KV_LORA = 256
C_NOPE = 64
C_ROPE = 32
C_V = 64
ROPE_THETA = 10000.0
D_HEADS = 8
D_HD = 64

LANE = 128
SUBLANE = 8
VMEM_LIMIT = 56 * 1024 * 1024

PAD = LANE - N_META
NEG = -1e30
LOG2E = math.log2(math.e)

HALF = C_ROPE // 2
ROPE_LO = C_NOPE
ROPE_MID = C_NOPE + HALF
ROPE_HI = C_NOPE + C_ROPE
MASK_LANE = ROPE_HI


def _cparams(sem):
    return pltpu.CompilerParams(dimension_semantics=sem, vmem_limit_bytes=VMEM_LIMIT)


def _dot(a, b):
    return jnp.dot(a, b, preferred_element_type=F32)


def _dot_nt(a, b):
    return lax.dot_general(a, b, (((1,), (1,)), ((), ())), preferred_element_type=F32)


def _split2(x):
    hi = x.astype(BF16)
    lo = (x - hi.astype(F32)).astype(BF16)
    return hi, lo


def _split3(x):
    hi = x.astype(BF16)
    r = x - hi.astype(F32)
    mid = r.astype(BF16)
    lo = (r - mid.astype(F32)).astype(BF16)
    return hi, mid, lo


def _dot_sel_r(x, sel, parts):
    pieces = _split3(x) if parts == 3 else _split2(x)
    acc = _dot(pieces[0], sel)
    for p in pieces[1:]:
        acc = acc + _dot(p, sel)
    return acc


def _dot_sel_l(sel, x, parts):
    pieces = _split3(x) if parts == 3 else _split2(x)
    acc = _dot(sel, pieces[0])
    for p in pieces[1:]:
        acc = acc + _dot(sel, p)
    return acc


def _rms_rows(x, g):
    ms = jnp.mean(x * x, axis=-1, keepdims=True)
    return x * lax.rsqrt(ms + EPS) * g


def _log_sigmoid(x):
    return -(jnp.maximum(-x, 0.0) + jnp.log1p(jnp.exp(-jnp.abs(x))))


def _sigmoid(x):
    return 0.5 * jnp.tanh(0.5 * x) + 0.5


def _tril_bf16(n):
    r = lax.broadcasted_iota(jnp.int32, (n, n), 0)
    c = lax.broadcasted_iota(jnp.int32, (n, n), 1)
    return jnp.where(r >= c, 1.0, 0.0).astype(BF16)


EV_SLABS = (("q", 512, BF16), ("k", 512, BF16), ("v", 1024, BF16), ("o", 1024, BF16),
            ("g", 128, F32), ("xb", 1024, BF16), ("gb", 1024, BF16))


def _input_tile(x_ref, mb_ref):
    xt = x_ref[...]
    first = jnp.concatenate([mb_ref[...], xt[:xt.shape[0] - LANE]], axis=0)
    return jnp.where(pl.program_id(1) == 0, first, xt)


def _input_specs(seq, tt, d):
    x_spec = pl.BlockSpec(
        (pl.Element(tt), pl.Element(d)),
        lambda b, t: (pl.multiple_of(b * seq + jnp.maximum(t * tt - LANE, 0), LANE), 0))
    return [x_spec, pl.BlockSpec((LANE, d), lambda b, t: (0, 0))]


def _ev_in_kernel(x_ref, mb_ref, ln_ref, w_ref, *out_refs):
    xn = _rms_rows(_input_tile(x_ref, mb_ref), ln_ref[...]).astype(BF16)
    off = 0
    for (_, width, dt), ref in zip(EV_SLABS, out_refs):
        ref[...] = _dot(xn, w_ref[:, off:off + width]).astype(dt)
        off += width


def _ev_in(x2, mb, ln, w, bsz, tp, tt):
    d = x2.shape[1]
    n = w.shape[1]
    nt = tp // tt
    outs = [jax.ShapeDtypeStruct((bsz * tp, width), dt) for (_, width, dt) in EV_SLABS]
    return pl.pallas_call(
        _ev_in_kernel,
        out_shape=outs,
        grid=(bsz, nt),
        in_specs=_input_specs(x2.shape[0] // bsz, tt, d) + [
            pl.BlockSpec((1, d), lambda b, t: (0, 0)),
            pl.BlockSpec((d, n), lambda b, t: (0, 0))],
        out_specs=[pl.BlockSpec((tt, width), lambda b, t: (b * nt + t, 0)) for (_, width, _) in EV_SLABS],
        compiler_params=_cparams(("parallel", "parallel")),
        name="ev_in",
    )(x2, mb, ln, w)


def _mlstm_kernel(q_ref, k_ref, v_ref, o_ref, g_ref, bif_ref, an_ref, out_ref, cn_ref, m_ref):
    c = pl.program_id(1)
    L = q_ref.shape[0]
    dv_ext = A_DV + LANE

    @pl.when(c == 0)
    def _():
        cn_ref[...] = jnp.zeros_like(cn_ref)
        m_ref[...] = jnp.zeros_like(m_ref)

    lane = lax.broadcasted_iota(jnp.int32, (L, LANE), 1)
    row = lax.broadcasted_iota(jnp.int32, (L, LANE), 0)
    valid = (row >= PAD) | (c > 0)
    g = g_ref[...] + bif_ref[...]
    li = jnp.where(valid, g, NEG)
    lf = jnp.where(valid, _log_sigmoid(g), 0.0)
    is_f = (lane >= A_HEADS) & (lane < 2 * A_HEADS)
    lf = jnp.where(is_f, lf, 0.0)
    tril = _tril_bf16(L)
    bcum = _dot_sel_l(tril, lf, 3)
    li_t = li.T
    bcum_t = bcum.T

    rr = lax.broadcasted_iota(jnp.int32, (L, L), 0)
    cc = lax.broadcasted_iota(jnp.int32, (L, L), 1)
    causal = rr >= cc
    e0 = jnp.where(lane == 0, 1.0, 0.0).astype(BF16)
    scale = A_DQK ** -0.5

    heads = []
    for h in range(A_HEADS):
        qs = (q_ref[:, h * A_DQK:(h + 1) * A_DQK].astype(F32) * scale).astype(BF16)
        kk = k_ref[:, h * A_DQK:(h + 1) * A_DQK]
        vext = jnp.concatenate([v_ref[:, h * A_DV:(h + 1) * A_DV], e0], axis=1)
        i_col = li[:, h:h + 1]
        b_col = bcum[:, A_HEADS + h:A_HEADS + h + 1]
        i_row = li_t[h:h + 1, :]
        b_row = bcum_t[A_HEADS + h:A_HEADS + h + 1, :]
        m_state = m_ref[:, h:h + 1]
        cn = cn_ref[h]
        qk = _dot_nt(qs, kk)
        qc = _dot(qs, cn.astype(BF16))

        gsum = b_col[L - 1:L, :]
        dk_col = gsum - b_col + i_col
        dk_row = gsum - b_row + i_row
        m_new = jnp.maximum(gsum + m_state, jnp.max(dk_row, axis=1, keepdims=True))
        wk = jnp.exp(dk_col - m_new)
        sc = jnp.exp(gsum + m_state - m_new)
        kw_t = (kk.astype(F32) * wk).T.astype(BF16)
        cn_ref[h] = sc * cn + _dot(kw_t, vext)
        m_ref[:, h:h + 1] = m_new
        heads.append((qk, qc, vext, b_col, b_row, i_row, m_state))

    weighted = []
    for qk, qc, vext, b_col, b_row, i_row, m_state in heads:
        dmat = jnp.where(causal, b_col - b_row + i_row, NEG)
        inter = b_col + m_state
        m_row = jnp.maximum(inter, jnp.max(dmat, axis=1, keepdims=True))
        w_intra = jnp.exp(dmat - m_row)
        w_inter = jnp.exp(inter - m_row)
        s = (qk * w_intra).astype(BF16)
        weighted.append((w_inter * qc + _dot(s, vext), m_row))

    for h, (numden, m_row) in enumerate(weighted):
        num = numden[:, :A_DV]
        den = numden[:, A_DV:A_DV + 1]
        hh = num / jnp.maximum(jnp.abs(den), jnp.exp(-m_row))
        hn = _rms_rows(hh, an_ref[...])
        og = _sigmoid(o_ref[:, h * A_DV:(h + 1) * A_DV].astype(F32))
        out_ref[:, h * A_DV:(h + 1) * A_DV] = (hn * og).astype(out_ref.dtype)


def _mlstm(q, k, v, o, g, bif, an, bsz, tp, L):
    nq, nv = A_HEADS * A_DQK, A_HEADS * A_DV
    nc = tp // L
    row_map = lambda b, c: (b * nc + c, 0)
    const = lambda b, c: (0, 0)
    return pl.pallas_call(
        _mlstm_kernel,
        out_shape=jax.ShapeDtypeStruct((bsz * tp, nv), BF16),
        grid=(bsz, nc),
        in_specs=[pl.BlockSpec((L, nq), row_map), pl.BlockSpec((L, nq), row_map),
                  pl.BlockSpec((L, nv), row_map), pl.BlockSpec((L, nv), row_map),
                  pl.BlockSpec((L, LANE), row_map),
                  pl.BlockSpec((1, LANE), const), pl.BlockSpec((1, A_DV), const)],
        out_specs=pl.BlockSpec((L, nv), row_map),
        scratch_shapes=[pltpu.VMEM((A_HEADS, A_DQK, A_DV + LANE), F32),
                        pltpu.VMEM((1, LANE), F32)],
        compiler_params=_cparams(("parallel", "arbitrary")),
        name="mlstm",
    )(q, k, v, o, g, bif, an)


def _gelu_tanh(x):
    c0 = math.sqrt(2.0 / math.pi)
    return x * (0.5 * jnp.tanh(x * (c0 + (c0 * 0.044715) * (x * x))) + 0.5)


def _lru_kernel(xb_ref, gb_ref, cw_ref, cb_ref, wra_ref, bra_ref, wrx_ref, brx_ref, lam_ref,
                out_ref, hist_ref, hc_ref, a_s, u_s):
    t = pl.program_id(1)
    tt, w = xb_ref.shape

    @pl.when(t == 0)
    def _():
        hist_ref[...] = jnp.zeros_like(hist_ref)
        hc_ref[...] = jnp.zeros_like(hc_ref)

    row = lax.broadcasted_iota(jnp.int32, (tt, w), 0)
    valid = (row >= PAD) | (t > 0)
    x = jnp.where(valid, xb_ref[...].astype(F32), 0.0)
    xfull = jnp.concatenate([hist_ref[...], x], axis=0)
    conv = cb_ref[...] + cw_ref[CONV_W - 1:CONV_W, :] * x
    for d in range(1, CONV_W):
        shifted = pltpu.roll(xfull, d, 0)[SUBLANE:, :]
        conv = conv + cw_ref[CONV_W - 1 - d:CONV_W - d, :] * shifted
    hist_ref[...] = x[tt - SUBLANE:, :]

    xcb = conv.astype(BF16)
    r_parts, i_parts = [], []
    for blk in range(B_BLOCKS):
        xs = xcb[:, blk * B_BLOCK:(blk + 1) * B_BLOCK]
        r_parts.append(_dot(xs, wra_ref[blk]))
        i_parts.append(_dot(xs, wrx_ref[blk]))
    r = _sigmoid(jnp.concatenate(r_parts, axis=1) + bra_ref[...])
    ig = _sigmoid(jnp.concatenate(i_parts, axis=1) + brx_ref[...])
    nlam = -lam_ref[...]
    softplus = jnp.maximum(nlam, 0.0) + jnp.log1p(jnp.exp(-jnp.abs(nlam)))
    a = jnp.exp2(((-LRU_C * LOG2E) * softplus) * r)
    y = 1.0 - a * a
    u = jnp.where(y > 0.0, y * lax.rsqrt(y), 0.0) * (ig * conv)
    a_s[...] = a
    u_s[...] = jnp.where(valid, u, 0.0)

    srow = lax.broadcasted_iota(jnp.int32, (SUBLANE, w), 0)

    def group(gi, hprev):
        sl = pl.ds(pl.multiple_of(gi * SUBLANE, SUBLANE), SUBLANE)
        ag = a_s[sl, :]
        ug = u_s[sl, :]
        for d in (1, 2, 4):
            ok = srow >= d
            a_sh = jnp.where(ok, pltpu.roll(ag, d, 0), 1.0)
            u_sh = jnp.where(ok, pltpu.roll(ug, d, 0), 0.0)
            ug = ag * u_sh + ug
            ag = ag * a_sh
        hg = ug + ag * hprev
        u_s[sl, :] = hg
        return jnp.broadcast_to(hg[SUBLANE - 1:SUBLANE, :], (SUBLANE, w))

    hc_ref[...] = lax.fori_loop(0, tt // SUBLANE, group, hc_ref[...])
    out_ref[...] = (u_s[...] * _gelu_tanh(gb_ref[...].astype(F32))).astype(out_ref.dtype)


def _lru(xb, gb, cw, cb, wra, bra, wrx, brx, lam, bsz, tp, tt):
    w = xb.shape[1]
    nt = tp // tt
    row_map = lambda b, t: (b * nt + t, 0)
    const2 = lambda b, t: (0, 0)
    const3 = lambda b, t: (0, 0, 0)
    return pl.pallas_call(
        _lru_kernel,
        out_shape=jax.ShapeDtypeStruct(xb.shape, BF16),
        grid=(bsz, nt),
        in_specs=[pl.BlockSpec((tt, w), row_map), pl.BlockSpec((tt, w), row_map),
                  pl.BlockSpec((CONV_W, w), const2), pl.BlockSpec((1, w), const2),
                  pl.BlockSpec((B_BLOCKS, B_BLOCK, B_BLOCK), const3), pl.BlockSpec((1, w), const2),
                  pl.BlockSpec((B_BLOCKS, B_BLOCK, B_BLOCK), const3), pl.BlockSpec((1, w), const2),
                  pl.BlockSpec((1, w), const2)],
        out_specs=pl.BlockSpec((tt, w), row_map),
        scratch_shapes=[pltpu.VMEM((SUBLANE, w), F32), pltpu.VMEM((SUBLANE, w), F32),
                        pltpu.VMEM((tt, w), F32), pltpu.VMEM((tt, w), F32)],
        compiler_params=_cparams(("parallel", "arbitrary")),
        name="conv_rglru",
    )(xb, gb, cw, cb, wra, bra, wrx, brx, lam)


def _out_mlp_kernel(n_mix, from_input, *refs):
    n_h = 2 if from_input else 1
    h_refs, refs = refs[:n_h], refs[n_h:]
    mix_refs = refs[:n_mix]
    ln_ref = refs[n_mix]
    n_w = n_mix + 2
    w_hbm = refs[n_mix + 1:n_mix + 1 + n_w]
    out_ref = refs[n_mix + 1 + n_w]
    w_vmem = refs[n_mix + 2 + n_w:n_mix + 2 + 2 * n_w]
    sem = refs[n_mix + 2 + 2 * n_w]

    @pl.when((pl.program_id(0) == 0) & (pl.program_id(1) == 0))
    def _():
        copies = [pltpu.make_async_copy(src, dst, sem.at[i])
                  for i, (src, dst) in enumerate(zip(w_hbm, w_vmem))]
        for cp in copies:
            cp.start()
        for cp in copies:
            cp.wait()

    wo_refs, w1_ref, w2_ref = w_vmem[:n_mix], w_vmem[n_mix], w_vmem[n_mix + 1]
    res = _input_tile(*h_refs) if from_input else h_refs[0][...]
    for m_ref, w_ref in zip(mix_refs, wo_refs):
        res = res + _dot(m_ref[...], w_ref[...])
    hn = _rms_rows(res, ln_ref[...]).astype(BF16)
    a = jnp.maximum(_dot(hn, w1_ref[...]), 0.0)
    out_ref[...] = res + _dot((a * a).astype(BF16), w2_ref[...])


def _out_mlp(h_in, mixes, wouts, ln, w1, w2, tm, rows):
    d = w2.shape[1]
    n_mix = len(mixes)
    kind = rows[0]
    if kind == "all":
        r = rows[1]
        grid, out_rows = (r // tm, 1), r
        mix_spec = lambda n: pl.BlockSpec((tm, n), lambda i, j: (i, 0))
        h_specs, h_args = [mix_spec(d)], [h_in]
        out_spec = pl.BlockSpec((tm, d), lambda i, j: (i, 0))
    elif kind == "tokens":
        _, bsz, tp, seq = rows
        per_seq = seq // tm
        grid, out_rows = (bsz, per_seq), bsz * seq
        mix_spec = lambda n: pl.BlockSpec(
            (pl.Element(tm), pl.Element(n)),
            lambda b, i: (pl.multiple_of(b * tp + (tp - seq) + i * tm, LANE), 0))
        h_specs, h_args = [mix_spec(d)], [h_in]
        out_spec = pl.BlockSpec((tm, d), lambda b, i: (b * per_seq + i, 0))
    else:
        _, bsz, tp, seq = rows
        nt = tp // tm
        grid, out_rows = (bsz, nt), bsz * tp
        mix_spec = lambda n: pl.BlockSpec((tm, n), lambda b, t: (b * nt + t, 0))
        h_specs, h_args = _input_specs(seq, tm, d), list(h_in)
        out_spec = mix_spec(d)
    weights = [*wouts, w1, w2]
    in_specs = h_specs + [mix_spec(m.shape[1]) for m in mixes]
    in_specs += [pl.BlockSpec((1, d), lambda i, j: (0, 0))]
    in_specs += [pl.BlockSpec(memory_space=pl.ANY) for _ in weights]
    scratch = [pltpu.VMEM(w.shape, w.dtype) for w in weights]
    scratch.append(pltpu.SemaphoreType.DMA((len(weights),)))
    return pl.pallas_call(
        functools.partial(_out_mlp_kernel, n_mix, kind == "input"),
        out_shape=jax.ShapeDtypeStruct((out_rows, d), F32),
        grid=grid,
        in_specs=in_specs,
        out_specs=out_spec,
        scratch_shapes=scratch,
        compiler_params=_cparams(("arbitrary", "arbitrary")),
        name="out_mlp",
    )(*h_args, *mixes, ln, *weights)


OD_COLS = dict(cq=(0, 384), ckv=(384, 640), kr=(640, 768), fq=(768, 1280), fk=(1280, 1792),
               fv=(1792, 2304), ff=(2304, 2432))
OD_N = 2432


def _od_in_kernel(h_ref, ln_ref, w_ref, pos_ref, frq_ref, phs_ref, gqa_ref, gkva_ref,
                  wq_ref, wkn_ref, wv_ref, gq_ref, gk_ref, gkr_ref, gfq_ref, gfk_ref, bf_ref,
                  gm_ref, g64_ref, sel_ref,
                  qm_ref, km_ref, vmt_ref, qf_ref, kf_ref, vft_ref, fc_ref):
    t = pl.program_id(1)
    tm = h_ref.shape[1]

    @pl.when(t == 0)
    def _():
        fc_ref[...] = jnp.zeros_like(fc_ref)

    xn = _rms_rows(h_ref[0], ln_ref[...]).astype(BF16)

    def proj(name):
        a, b = OD_COLS[name]
        return _dot(xn, w_ref[:, a:b])

    z_cq, z_ckv, z_kr = proj("cq"), proj("ckv"), proj("kr")

    wide = 2 * LANE
    lane2 = lax.broadcasted_iota(jnp.int32, (tm, wide), 1) % LANE
    row2 = lax.broadcasted_iota(jnp.int32, (tm, wide), 0)
    unused_row2 = (row2 < PAD) & (t == 0)
    mask_lane = lane2 == MASK_LANE
    n_grp = LANE // C_ROPE
    tq = tm // n_grp
    pos = pos_ref[0]
    angc = phs_ref[...]
    for g in range(n_grp):
        angc = angc + pos[g * tq:(g + 1) * tq, :] * frq_ref[g:g + 1, :]
    cc = jnp.cos(angc)
    lane1 = lax.broadcasted_iota(jnp.int32, (tq, LANE), 1)
    in_lo = (lane1 >= ROPE_LO) & (lane1 < ROPE_MID)
    in_hi = (lane1 >= ROPE_MID) & (lane1 < ROPE_HI)
    cs_parts, sn_parts = [], []
    for g in range(n_grp):
        shift = (ROPE_LO - C_ROPE * g) % LANE
        c = pltpu.roll(cc, shift, 1) if shift else cc
        cs_parts.append(jnp.where(in_hi, pltpu.roll(c, HALF, 1), jnp.where(in_lo, c, 1.0)))
        sn_parts.append(jnp.where(in_lo, -pltpu.roll(c, LANE - HALF, 1), jnp.where(in_hi, c, 0.0)))
    cs = jnp.concatenate(cs_parts, axis=0)
    sn = jnp.concatenate(sn_parts, axis=0)
    cs = jnp.concatenate([cs, cs], axis=1)
    sn = jnp.concatenate([sn, sn], axis=1)
    first_half = lane2 < ROPE_MID

    def seg_rms(x, sel, gain):
        ms = _dot((x * x).astype(BF16), sel)
        return x * lax.rsqrt(ms + EPS) * gain

    def rope(y):
        sw = jnp.where(first_half, pltpu.roll(y, wide - HALF, 1), pltpu.roll(y, HALF, 1))
        return y * cs + sw * sn

    gm = gm_ref[...]
    g64 = g64_ref[...]

    cqn = _rms_rows(z_cq, gqa_ref[...]).astype(BF16)
    ckvn = _rms_rows(z_ckv, gkva_ref[...]).astype(BF16)
    qall = _dot(cqn, wq_ref[...])
    knall = _dot(ckvn, wkn_ref[...])
    scale_m = (C_NOPE + C_ROPE) ** -0.5 * LOG2E
    for h2 in range(C_HEADS // 2):
        cols = slice(h2 * wide, (h2 + 1) * wide)
        y = rope(seg_rms(qall[:, cols], gm,gq_ref[...]))
        qm_ref[0, :, cols] = jnp.where(mask_lane, 1.0, y * scale_m).astype(BF16)
    krt = rope(seg_rms(jnp.concatenate([z_kr, z_kr], axis=1), gm,gkr_ref[...]))
    krt = jnp.where(mask_lane & unused_row2, NEG, krt)
    for h2 in range(C_HEADS // 2):
        cols = slice(h2 * wide, (h2 + 1) * wide)
        y = seg_rms(knall[:, cols], gm,gk_ref[...]) + krt
        km_ref[0, :, cols] = y.astype(BF16)

    fq = proj("fq")
    fk = proj("fk")
    scale_f = D_HD ** -0.5 * LOG2E
    npair = D_HEADS // 2
    for p2 in range(npair // 2):
        cols = slice(p2 * wide, (p2 + 1) * wide)
        y = seg_rms(fq[:, cols], g64,gfq_ref[...])
        qf_ref[0, :, cols] = (y * scale_f).astype(BF16)
        y = seg_rms(fk[:, cols], g64,gfk_ref[...]).astype(BF16)
        for e in range(2):
            p = 2 * p2 + e
            kf_ref[0, :, 2 * p * LANE:(2 * p + 1) * LANE] = y[:, e * LANE:(e + 1) * LANE]

    lane = lax.broadcasted_iota(jnp.int32, (tm, LANE), 1)
    unused_row = (lax.broadcasted_iota(jnp.int32, (tm, LANE), 0) < PAD) & (t == 0)
    valid = jnp.logical_not(unused_row) & (lane < D_HEADS)
    lf = jnp.where(valid, _log_sigmoid(proj("ff") + bf_ref[...]), 0.0)
    fc = _dot_sel_l(_tril_bf16(tm), lf, 3) + fc_ref[...]
    fc_ref[...] = fc[tm - 1:tm, :]
    hi, mid, lo = _split3(jnp.where(unused_row, NEG, fc * (-LOG2E)))
    bias = _dot(hi, sel_ref[0]) + _dot(mid, sel_ref[1]) + _dot(lo, sel_ref[2])
    for p in range(npair):
        kf_ref[0, :, (2 * p + 1) * LANE:(2 * p + 2) * LANE] = bias[:, p * LANE:(p + 1) * LANE].astype(BF16)

    vft_ref[0] = proj("fv").T.astype(BF16)
    vmt_ref[0] = _dot(ckvn, wv_ref[...]).T.astype(BF16)


def _od_in(h3, pos_col, consts, tm):
    bsz, tp, d = h3.shape
    nt = tp // tm
    names = ["ln", "w", None, "frq", "phs", "gqa", "gkva", "wq", "wkn", "wv", "gq", "gk", "gkr",
             "gfq", "gfk", "bf", "gm", "g64", "sel"]
    args, in_specs = [h3], [pl.BlockSpec((1, tm, d), lambda b, t: (b, t, 0))]
    for nm in names:
        if nm is None:
            args.append(pos_col)
            in_specs.append(pl.BlockSpec((1, tm, 1), lambda b, t: (b, t, 0)))
            continue
        a = consts[nm]
        args.append(a)
        in_specs.append(pl.BlockSpec(a.shape, (lambda nd: (lambda b, t: (0,) * nd))(a.ndim)))
    hl = C_HEADS * LANE
    pk = D_HEADS * D_HD
    outs = [jax.ShapeDtypeStruct((bsz, tp, hl), BF16), jax.ShapeDtypeStruct((bsz, tp, hl), BF16),
            jax.ShapeDtypeStruct((bsz, pk, tp), BF16), jax.ShapeDtypeStruct((bsz, tp, pk), BF16),
            jax.ShapeDtypeStruct((bsz, tp, 2 * pk), BF16), jax.ShapeDtypeStruct((bsz, pk, tp), BF16)]
    rowspec = lambda n: pl.BlockSpec((1, tm, n), lambda b, t: (b, t, 0))
    tspec = pl.BlockSpec((1, pk, tm), lambda b, t: (b, 0, t))
    return pl.pallas_call(
        _od_in_kernel,
        out_shape=outs,
        grid=(bsz, nt),
        in_specs=in_specs,
        out_specs=[rowspec(hl), rowspec(hl), tspec, rowspec(pk), rowspec(2 * pk), tspec],
        scratch_shapes=[pltpu.VMEM((1, LANE), F32)],
        compiler_params=_cparams(("parallel", "arbitrary")),
        name="od_in",
    )(*args)


def _attn_kernel(fox, blk, nh, q_ref, k_ref, vt_ref, out_ref, st_ref, acc_ref, buf0, buf1, bufm,
                 mx0, mx1, mxm, *maybe_qe):
    tp = k_ref.shape[1]
    n_blocks = (tp - LANE) // blk
    half = LANE // 2

    def start_qblock(q0, nq):
        st_ref[0:nh, 0:nq] = jnp.full((nh, nq), NEG, F32)
        st_ref[nh:2 * nh, 0:nq] = jnp.zeros((nh, nq), F32)
        acc_ref[:, :, 0:nq] = jnp.zeros((nh, LANE, nq), F32)
        if fox:
            qe_ref = maybe_qe[0]
            lane = lax.broadcasted_iota(jnp.int32, (nq, LANE), 1)
            for hd in range(nh):
                pair, e = divmod(hd, 2)
                q = q_ref[0, pl.ds(q0, nq), pair * LANE:(pair + 1) * LANE]
                mine = (lane >= e * half) & (lane < (e + 1) * half)
                ones = ((lane >= 3 * e) & (lane < 3 * e + 3)).astype(F32).astype(BF16)
                qe_ref[hd, 0:nq, :] = jnp.concatenate([jnp.where(mine, q, jnp.zeros_like(q)), ones], axis=1)

    def logits_to(buf, mx, q0, nq, k0, nk, diagonal):
        if diagonal:
            ok = (lax.broadcasted_iota(jnp.int32, (nk, nq), 0)
                  <= lax.broadcasted_iota(jnp.int32, (nk, nq), 1))
        for hd in range(nh):
            if fox:
                pair = hd // 2
                s = _dot_nt(k_ref[0, pl.ds(k0, nk), pair * 2 * LANE:(pair + 1) * 2 * LANE],
                            maybe_qe[0][hd, 0:nq, :])
            else:
                s = _dot_nt(k_ref[0, pl.ds(k0, nk), hd * LANE:(hd + 1) * LANE],
                            q_ref[0, pl.ds(q0, nq), hd * LANE:(hd + 1) * LANE])
            if diagonal:
                s = jnp.where(ok, s, NEG)
            buf[hd, 0:nk, 0:nq] = s
            mx[hd:hd + 1, 0:nq] = jnp.max(s, axis=0, keepdims=True)

    def consume(buf, mx, nq, k0, nk):
        for hd in range(nh):
            pair = hd // 2
            vt = vt_ref[0, pair * LANE:(pair + 1) * LANE, pl.ds(k0, nk)]
            m_old = st_ref[hd:hd + 1, 0:nq]
            l_old = st_ref[nh + hd:nh + hd + 1, 0:nq]
            m_new = jnp.maximum(m_old, mx[hd:hd + 1, 0:nq])
            alpha = jnp.exp2(m_old - m_new)
            p = jnp.exp2(buf[hd, 0:nk, 0:nq] - m_new)
            st_ref[hd:hd + 1, 0:nq] = m_new
            st_ref[nh + hd:nh + hd + 1, 0:nq] = alpha * l_old + jnp.sum(p, axis=0, keepdims=True)
            acc_ref[hd, :, 0:nq] = acc_ref[hd, :, 0:nq] * alpha + _dot(vt, p.astype(BF16))

    def finish_qblock(q0, nq):
        for pair in range(nh // 2):
            a, b = 2 * pair, 2 * pair + 1
            o_a = acc_ref[a, 0:half, 0:nq] / st_ref[nh + a:nh + a + 1, 0:nq]
            o_b = acc_ref[b, half:LANE, 0:nq] / st_ref[nh + b:nh + b + 1, 0:nq]
            out_ref[0, pl.ds(q0, nq), pair * LANE:(pair + 1) * LANE] = (
                jnp.concatenate([o_a, o_b], axis=0).T.astype(out_ref.dtype))

    start_qblock(0, LANE)
    logits_to(bufm, mxm, 0, LANE, 0, LANE, True)
    consume(bufm, mxm, LANE, 0, LANE)
    finish_qblock(0, LANE)

    def qblock(i, carry):
        q0 = pl.multiple_of(LANE + i * blk, LANE)

        def k_of(n):
            return pl.multiple_of(LANE + n * blk, LANE)

        start_qblock(q0, blk)
        logits_to(bufm, mxm, q0, blk, 0, LANE, False)

        @pl.when(i == 0)
        def _():
            logits_to(buf1, mx1, q0, blk, q0, blk, True)
            consume(bufm, mxm, blk, 0, LANE)
            consume(buf1, mx1, blk, q0, blk)

        @pl.when(i > 0)
        def _():
            logits_to(buf0, mx0, q0, blk, k_of(0), blk, False)
            consume(bufm, mxm, blk, 0, LANE)

            def pair(m, c):
                logits_to(buf1, mx1, q0, blk, k_of(2 * m + 1), blk, False)
                consume(buf0, mx0, blk, k_of(2 * m), blk)
                logits_to(buf0, mx0, q0, blk, k_of(2 * m + 2), blk, False)
                consume(buf1, mx1, blk, k_of(2 * m + 1), blk)
                return c

            n_pairs = (i - 1) // 2
            lax.fori_loop(0, n_pairs, pair, 0)
            held = 2 * n_pairs

            @pl.when(i % 2 == 1)
            def _():
                logits_to(buf1, mx1, q0, blk, q0, blk, True)
                consume(buf0, mx0, blk, k_of(held), blk)
                consume(buf1, mx1, blk, q0, blk)

            @pl.when(i % 2 == 0)
            def _():
                logits_to(buf1, mx1, q0, blk, k_of(held + 1), blk, False)
                consume(buf0, mx0, blk, k_of(held), blk)
                logits_to(buf0, mx0, q0, blk, q0, blk, True)
                consume(buf1, mx1, blk, k_of(held + 1), blk)
                consume(buf0, mx0, blk, q0, blk)

        finish_qblock(q0, blk)
        return carry

    lax.fori_loop(0, n_blocks, qblock, 0)


def _attention(q, k, vt, fox, blk, nh):
    bsz, tp, _ = q.shape
    npair = vt.shape[1] // LANE
    g = nh // 2
    qw = (LANE if fox else 2 * LANE) * g
    assert 2 * nh <= SUBLANE and npair % g == 0
    scratch = [pltpu.VMEM((SUBLANE, blk), F32), pltpu.VMEM((nh, LANE, blk), F32),
               pltpu.VMEM((nh, blk, blk), F32), pltpu.VMEM((nh, blk, blk), F32),
               pltpu.VMEM((nh, LANE, blk), F32),
               pltpu.VMEM((SUBLANE, blk), F32), pltpu.VMEM((SUBLANE, blk), F32),
               pltpu.VMEM((SUBLANE, blk), F32)]
    if fox:
        scratch.append(pltpu.VMEM((nh, blk, 2 * LANE), BF16))
    return pl.pallas_call(
        functools.partial(_attn_kernel, fox, blk, nh),
        out_shape=jax.ShapeDtypeStruct((bsz, tp, npair * LANE), BF16),
        grid=(bsz, npair // g),
        in_specs=[pl.BlockSpec((1, tp, qw), lambda b, p: (b, 0, p)),
                  pl.BlockSpec((1, tp, 2 * LANE * g), lambda b, p: (b, 0, p)),
                  pl.BlockSpec((1, LANE * g, tp), lambda b, p: (b, p, 0))],
        out_specs=pl.BlockSpec((1, tp, LANE * g), lambda b, p: (b, 0, p)),
        scratch_shapes=scratch,
        compiler_params=_cparams(("parallel", "parallel")),
        name="fox_attn" if fox else "mla_attn",
    )(q, k, vt)


def _place_cols(w, width, offset):
    k, n = w.shape
    return jnp.pad(w, ((0, 0), (offset, width - offset - n)))


def _ev_consts(w_in, b_if):
    nq, nv = A_HEADS * A_DQK, A_HEADS * A_DV
    o = 0
    wq = w_in[:, o:o + nq]; o += nq
    wk = w_in[:, o:o + nq]; o += nq
    wv = w_in[:, o:o + nv]; o += nv
    wo = w_in[:, o:o + nv]; o += nv
    wg = w_in[:, o:o + 2 * A_HEADS]; o += 2 * A_HEADS
    wxb = w_in[:, o:o + 1024]; o += 1024
    wgb = w_in[:, o:o + 1024]
    w = jnp.concatenate([wq, wk, wv, wo, _place_cols(wg, LANE, 0), wxb, wgb], axis=1).astype(BF16)
    bif = _place_cols(b_if[None, :], LANE, 0)
    return w, bif


def _head_tile_cols(w, n_heads, per_head, take, offset):
    k = w.shape[0]
    wh = w.reshape(k, n_heads, per_head)[:, :, take[0]:take[0] + take[1]]
    wh = jnp.pad(wh, ((0, 0), (0, 0), (offset, LANE - offset - take[1])))
    return wh.reshape(k, n_heads * LANE)


def _row128(parts):
    r = jnp.zeros((LANE,), F32)
    for off, v in parts:
        r = r.at[off:off + v.shape[0]].set(v.astype(F32))
    return r[None, :]


def _od_consts(p):
    c = {}
    w_in = p["w_in"]
    o = 0
    wcq = w_in[:, o:o + C_Q_LORA]; o += C_Q_LORA
    wckv = w_in[:, o:o + C_KV_LORA]; o += C_KV_LORA
    wkr = w_in[:, o:o + C_ROPE]; o += C_ROPE
    nf = D_HEADS * D_HD
    wfq = w_in[:, o:o + nf]; o += nf
    wfk = w_in[:, o:o + nf]; o += nf
    wfv = w_in[:, o:o + nf]; o += nf
    wff = w_in[:, o:o + D_HEADS]
    c["w"] = jnp.concatenate([wcq, wckv, _place_cols(wkr, LANE, ROPE_LO), wfq, wfk, wfv,
                              _place_cols(wff, LANE, 0)], axis=1).astype(BF16)
    c["ln"] = p["ln"][None, :]
    freqs = ROPE_THETA ** (-jnp.arange(HALF, dtype=F32) / HALF)
    n_grp = LANE // C_ROPE
    c["frq"] = jnp.concatenate([_row128([(C_ROPE * g, freqs), (C_ROPE * g + HALF, freqs)])
                                for g in range(n_grp)], axis=0)
    c["phs"] = _row128([(C_ROPE * g + HALF, jnp.full((HALF,), -0.5 * math.pi, F32)) for g in range(n_grp)])
    c["gqa"] = p["g_qa"][None, :]
    c["gkva"] = p["g_kva"][None, :]
    dq = C_NOPE + C_ROPE
    wuq, wukv = p["w_uq"], p["w_ukv"]
    c["wq"] = (_head_tile_cols(wuq, C_HEADS, dq, (0, dq), 0)).astype(BF16)
    c["wkn"] = _head_tile_cols(wukv, C_HEADS, C_NOPE + C_V, (0, C_NOPE), 0).astype(BF16)
    c["wv"] = wukv.reshape(-1, C_HEADS, C_NOPE + C_V)[:, :, C_NOPE:].reshape(-1, C_HEADS * C_V).astype(BF16)
    c["gq"] = _row128([(0, p["g_qn"]), (ROPE_LO, p["g_qr"])])
    c["gk"] = _row128([(0, p["g_kn"])])
    c["gkr"] = _row128([(ROPE_LO, p["g_kr"])])
    c["gfq"] = _row128([(0, p["g_fq"]), (D_HD, p["g_fq"])])
    c["gfk"] = _row128([(0, p["g_fk"]), (D_HD, p["g_fk"])])
    c["bf"] = _row128([(0, p["b_f"])])
    seg = np.where(np.arange(LANE) < C_NOPE, 0, np.where(np.arange(LANE) < ROPE_HI, 1, 2))
    inv_len = np.where(seg == 0, 1.0 / C_NOPE, 1.0 / C_ROPE)[None, :]
    c["gm"] = jnp.asarray((seg[:, None] == seg[None, :]) * inv_len, F32).astype(BF16)
    seg64 = np.arange(LANE) // D_HD
    c["g64"] = jnp.asarray((seg64[:, None] == seg64[None, :]) * (1.0 / D_HD), F32).astype(BF16)
    sel = np.zeros((3, LANE, (D_HEADS // 2) * LANE), np.float32)
    for h in range(D_HEADS):
        for part in range(3):
            sel[part, h, (h // 2) * LANE + 3 * (h % 2) + part] = 1.0
    c["sel"] = jnp.asarray(sel).astype(BF16)
    for nm in ("gq", "gk", "gkr", "gfq", "gfk"):
        c[nm] = jnp.tile(c[nm], (1, 2))
    for nm in ("gm", "g64"):
        z = jnp.zeros_like(c[nm])
        c[nm] = jnp.block([[c[nm], z], [z, c[nm]]])
    return c


def _time_tile(tp):
    return 3 * LANE if tp % (3 * LANE) == 0 else LANE


def kernel(x, positions, meta, ev_ln, ev_w_in, ev_b_if, ev_a_norm, ev_conv_w, ev_conv_b, ev_w_ra, ev_b_ra, ev_w_rx, ev_b_rx, ev_lam, ev_w_out, od_ln, od_w_in, od_b_f, od_g_qa, od_g_kva, od_w_uq, od_w_ukv, od_g_qn, od_g_qr, od_g_kn, od_g_kr, od_g_fq, od_g_fk, od_w_out, mlp_ln, w_ff1, w_ff2):
    bsz, seq, d = x.shape
    tp = LANE + seq
    r = bsz * tp
    tm = 512 if seq % 512 == 0 else LANE
    tt = _time_tile(tp)

    x2 = x.reshape(bsz * seq, d)
    mb = jnp.concatenate([jnp.zeros((PAD, d), x.dtype), meta.astype(x.dtype)], axis=0)
    pos_full = jnp.concatenate([jnp.zeros((bsz, PAD), jnp.int32),
                                jnp.broadcast_to(jnp.arange(N_META, dtype=jnp.int32)[None, :], (bsz, N_META)),
                                positions + N_META], axis=1)
    pos_col = pos_full.astype(F32)[:, :, None]

    e = 0
    w_a, bif = _ev_consts(ev_w_in[e], ev_b_if[e])
    q, k, v, o, g, xb, gb = _ev_in(x2, mb, ev_ln[e][None, :], w_a, bsz, tp, tt)
    ha = _mlstm(q, k, v, o, g, bif, ev_a_norm[e][None, :], bsz, tp, tt)
    hb = _lru(xb, gb, ev_conv_w[e], ev_conv_b[e][None, :], ev_w_ra[e].astype(BF16), ev_b_ra[e][None, :],
              ev_w_rx[e].astype(BF16), ev_b_rx[e][None, :], ev_lam[e][None, :], bsz, tp, tt)
    nv = A_HEADS * A_DV
    wo = ev_w_out[e].astype(BF16)
    h2 = _out_mlp((x2, mb), [ha, hb], [wo[:nv], wo[nv:]], mlp_ln[0][None, :],
                  w_ff1[0].astype(BF16), w_ff2[0].astype(BF16), tt, ("input", bsz, tp, seq))

    oi = 0
    consts = _od_consts(dict(ln=od_ln[oi], w_in=od_w_in[oi], b_f=od_b_f[oi], g_qa=od_g_qa[oi],
                             g_kva=od_g_kva[oi], w_uq=od_w_uq[oi], w_ukv=od_w_ukv[oi],
                             g_qn=od_g_qn[oi], g_qr=od_g_qr[oi], g_kn=od_g_kn[oi], g_kr=od_g_kr[oi],
                             g_fq=od_g_fq[oi], g_fk=od_g_fk[oi]))
    qm, km, vmt, qf, kf, vft = _od_in(h2.reshape(bsz, tp, d), pos_col, consts, tt)
    blk = 512 if seq % 512 == 0 else LANE
    hc = _attention(qm, km, vmt, False, blk, 4)
    hd = _attention(qf, kf, vft, True, blk, 4)
    nc = C_HEADS * C_V
    wo = od_w_out[oi].astype(BF16)
    out = _out_mlp(h2, [hc.reshape(r, nc), hd.reshape(r, nc)], [wo[:nc], wo[nc:]], mlp_ln[1][None, :],
                   w_ff1[1].astype(BF16), w_ff2[1].astype(BF16), tm, ("tokens", bsz, tp, seq))
    return out.reshape(bsz, seq, d)
```

```python
import functools
import math

import numpy as np
import jax
import jax.numpy as jnp
from jax import lax
from jax.experimental import pallas as pl
from jax.experimental.pallas import tpu as pltpu

F32 = jnp.float32
BF16 = jnp.bfloat16

N_META = 16
EPS = 1e-6
A_HEADS = 4
A_DQK = 128
A_DV = 256
B_BLOCKS = 8
B_BLOCK = 128
CONV_W = 4
LRU_C = 8.0
C_HEADS = 8
C_Q_LORA = 384
C_KV_LORA = 256
C_NOPE = 64
C_ROPE = 32
C_V = 64
ROPE_THETA = 10000.0
D_HEADS = 8
D_HD = 64

LANE = 128
SUBLANE = 8
VMEM_LIMIT = 56 * 1024 * 1024

PAD = LANE - N_META
NEG = -1e30
LOG2E = math.log2(math.e)

HALF = C_ROPE // 2
ROPE_LO = C_NOPE
ROPE_MID = C_NOPE + HALF
ROPE_HI = C_NOPE + C_ROPE
MASK_LANE = ROPE_HI


def _cparams(sem):
    return pltpu.CompilerParams(dimension_semantics=sem, vmem_limit_bytes=VMEM_LIMIT)


def _dot(a, b):
    return jnp.dot(a, b, preferred_element_type=F32)


def _dot_nt(a, b):
    return lax.dot_general(a, b, (((1,), (1,)), ((), ())), preferred_element_type=F32)


def _split2(x):
    hi = x.astype(BF16)
    lo = (x - hi.astype(F32)).astype(BF16)
    return hi, lo


def _split3(x):
    hi = x.astype(BF16)
    r = x - hi.astype(F32)
    mid = r.astype(BF16)
    lo = (r - mid.astype(F32)).astype(BF16)
    return hi, mid, lo


def _dot_sel_r(x, sel, parts):
    pieces = _split3(x) if parts == 3 else _split2(x)
    acc = _dot(pieces[0], sel)
    for p in pieces[1:]:
        acc = acc + _dot(p, sel)
    return acc


def _dot_sel_l(sel, x, parts):
    pieces = _split3(x) if parts == 3 else _split2(x)
    acc = _dot(sel, pieces[0])
    for p in pieces[1:]:
        acc = acc + _dot(sel, p)
    return acc


def _rms_rows(x, g):
    ms = jnp.mean(x * x, axis=-1, keepdims=True)
    return x * lax.rsqrt(ms + EPS) * g


def _log_sigmoid(x):
    return -(jnp.maximum(-x, 0.0) + jnp.log1p(jnp.exp(-jnp.abs(x))))


def _sigmoid(x):
    return 0.5 * jnp.tanh(0.5 * x) + 0.5


def _tril_bf16(n):
    r = lax.broadcasted_iota(jnp.int32, (n, n), 0)
    c = lax.broadcasted_iota(jnp.int32, (n, n), 1)
    return jnp.where(r >= c, 1.0, 0.0).astype(BF16)


EV_SLABS = (("q", 512, BF16), ("k", 512, BF16), ("v", 1024, BF16), ("o", 1024, BF16),
            ("g", 128, F32), ("xb", 1024, BF16), ("gb", 1024, BF16))


def _input_tile(x_ref, mb_ref):
    xt = x_ref[...]
    first = jnp.concatenate([mb_ref[...], xt[:xt.shape[0] - LANE]], axis=0)
    return jnp.where(pl.program_id(1) == 0, first, xt)


def _input_specs(seq, tt, d):
    x_spec = pl.BlockSpec(
        (pl.Element(tt), pl.Element(d)),
        lambda b, t: (pl.multiple_of(b * seq + jnp.maximum(t * tt - LANE, 0), LANE), 0))
    return [x_spec, pl.BlockSpec((LANE, d), lambda b, t: (0, 0))]


def _ev_in_kernel(x_ref, mb_ref, ln_ref, w_ref, *out_refs):
    xn = _rms_rows(_input_tile(x_ref, mb_ref), ln_ref[...]).astype(BF16)
    off = 0
    for (_, width, dt), ref in zip(EV_SLABS, out_refs):
        ref[...] = _dot(xn, w_ref[:, off:off + width]).astype(dt)
        off += width


def _ev_in(x2, mb, ln, w, bsz, tp, tt):
    d = x2.shape[1]
    n = w.shape[1]
    nt = tp // tt
    outs = [jax.ShapeDtypeStruct((bsz * tp, width), dt) for (_, width, dt) in EV_SLABS]
    return pl.pallas_call(
        _ev_in_kernel,
        out_shape=outs,
        grid=(bsz, nt),
        in_specs=_input_specs(x2.shape[0] // bsz, tt, d) + [
            pl.BlockSpec((1, d), lambda b, t: (0, 0)),
            pl.BlockSpec((d, n), lambda b, t: (0, 0))],
        out_specs=[pl.BlockSpec((tt, width), lambda b, t: (b * nt + t, 0)) for (_, width, _) in EV_SLABS],
        compiler_params=_cparams(("parallel", "parallel")),
        name="ev_in",
    )(x2, mb, ln, w)


def _mlstm_kernel(q_ref, k_ref, v_ref, o_ref, g_ref, bif_ref, an_ref, out_ref, cn_ref, m_ref):
    c = pl.program_id(1)
    L = q_ref.shape[0]
    dv_ext = A_DV + LANE

    @pl.when(c == 0)
    def _():
        cn_ref[...] = jnp.zeros_like(cn_ref)
        m_ref[...] = jnp.zeros_like(m_ref)

    lane = lax.broadcasted_iota(jnp.int32, (L, LANE), 1)
    row = lax.broadcasted_iota(jnp.int32, (L, LANE), 0)
    valid = (row >= PAD) | (c > 0)
    g = g_ref[...] + bif_ref[...]
    li = jnp.where(valid, g, NEG)
    lf = jnp.where(valid, _log_sigmoid(g), 0.0)
    is_f = (lane >= A_HEADS) & (lane < 2 * A_HEADS)
    lf = jnp.where(is_f, lf, 0.0)
    tril = _tril_bf16(L)
    bcum = _dot_sel_l(tril, lf, 3)
    li_t = li.T
    bcum_t = bcum.T

    rr = lax.broadcasted_iota(jnp.int32, (L, L), 0)
    cc = lax.broadcasted_iota(jnp.int32, (L, L), 1)
    causal = rr >= cc
    e0 = jnp.where(lane == 0, 1.0, 0.0).astype(BF16)
    scale = A_DQK ** -0.5

    heads = []
    for h in range(A_HEADS):
        qs = (q_ref[:, h * A_DQK:(h + 1) * A_DQK].astype(F32) * scale).astype(BF16)
        kk = k_ref[:, h * A_DQK:(h + 1) * A_DQK]
        vext = jnp.concatenate([v_ref[:, h * A_DV:(h + 1) * A_DV], e0], axis=1)
        i_col = li[:, h:h + 1]
        b_col = bcum[:, A_HEADS + h:A_HEADS + h + 1]
        i_row = li_t[h:h + 1, :]
        b_row = bcum_t[A_HEADS + h:A_HEADS + h + 1, :]
        m_state = m_ref[:, h:h + 1]
        cn = cn_ref[h]
        qk = _dot_nt(qs, kk)
        qc = _dot(qs, cn.astype(BF16))

        gsum = b_col[L - 1:L, :]
        dk_col = gsum - b_col + i_col
        dk_row = gsum - b_row + i_row
        m_new = jnp.maximum(gsum + m_state, jnp.max(dk_row, axis=1, keepdims=True))
        wk = jnp.exp(dk_col - m_new)
        sc = jnp.exp(gsum + m_state - m_new)
        kw_t = (kk.astype(F32) * wk).T.astype(BF16)
        cn_ref[h] = sc * cn + _dot(kw_t, vext)
        m_ref[:, h:h + 1] = m_new
        heads.append((qk, qc, vext, b_col, b_row, i_row, m_state))

    weighted = []
    for qk, qc, vext, b_col, b_row, i_row, m_state in heads:
        dmat = jnp.where(causal, b_col - b_row + i_row, NEG)
        inter = b_col + m_state
        m_row = jnp.maximum(inter, jnp.max(dmat, axis=1, keepdims=True))
        w_intra = jnp.exp(dmat - m_row)
        w_inter = jnp.exp(inter - m_row)
        s = (qk * w_intra).astype(BF16)
        weighted.append((w_inter * qc + _dot(s, vext), m_row))

    for h, (numden, m_row) in enumerate(weighted):
        num = numden[:, :A_DV]
        den = numden[:, A_DV:A_DV + 1]
        hh = num / jnp.maximum(jnp.abs(den), jnp.exp(-m_row))
        hn = _rms_rows(hh, an_ref[...])
        og = _sigmoid(o_ref[:, h * A_DV:(h + 1) * A_DV].astype(F32))
        out_ref[:, h * A_DV:(h + 1) * A_DV] = (hn * og).astype(out_ref.dtype)


def _mlstm(q, k, v, o, g, bif, an, bsz, tp, L):
    nq, nv = A_HEADS * A_DQK, A_HEADS * A_DV
    nc = tp // L
    row_map = lambda b, c: (b * nc + c, 0)
    const = lambda b, c: (0, 0)
    return pl.pallas_call(
        _mlstm_kernel,
        out_shape=jax.ShapeDtypeStruct((bsz * tp, nv), BF16),
        grid=(bsz, nc),
        in_specs=[pl.BlockSpec((L, nq), row_map), pl.BlockSpec((L, nq), row_map),
                  pl.BlockSpec((L, nv), row_map), pl.BlockSpec((L, nv), row_map),
                  pl.BlockSpec((L, LANE), row_map),
                  pl.BlockSpec((1, LANE), const), pl.BlockSpec((1, A_DV), const)],
        out_specs=pl.BlockSpec((L, nv), row_map),
        scratch_shapes=[pltpu.VMEM((A_HEADS, A_DQK, A_DV + LANE), F32),
                        pltpu.VMEM((1, LANE), F32)],
        compiler_params=_cparams(("parallel", "arbitrary")),
        name="mlstm",
    )(q, k, v, o, g, bif, an)


def _gelu_tanh(x):
    c0 = math.sqrt(2.0 / math.pi)
    return x * (0.5 * jnp.tanh(x * (c0 + (c0 * 0.044715) * (x * x))) + 0.5)


def _lru_kernel(xb_ref, gb_ref, cw_ref, cb_ref, wra_ref, bra_ref, wrx_ref, brx_ref, lam_ref,
                out_ref, hist_ref, hc_ref, a_s, u_s):
    t = pl.program_id(1)
    tt, w = xb_ref.shape

    @pl.when(t == 0)
    def _():
        hist_ref[...] = jnp.zeros_like(hist_ref)
        hc_ref[...] = jnp.zeros_like(hc_ref)

    row = lax.broadcasted_iota(jnp.int32, (tt, w), 0)
    valid = (row >= PAD) | (t > 0)
    x = jnp.where(valid, xb_ref[...].astype(F32), 0.0)
    xfull = jnp.concatenate([hist_ref[...], x], axis=0)
    conv = cb_ref[...] + cw_ref[CONV_W - 1:CONV_W, :] * x
    for d in range(1, CONV_W):
        shifted = pltpu.roll(xfull, d, 0)[SUBLANE:, :]
        conv = conv + cw_ref[CONV_W - 1 - d:CONV_W - d, :] * shifted
    hist_ref[...] = x[tt - SUBLANE:, :]

    xcb = conv.astype(BF16)
    r_parts, i_parts = [], []
    for blk in range(B_BLOCKS):
        xs = xcb[:, blk * B_BLOCK:(blk + 1) * B_BLOCK]
        r_parts.append(_dot(xs, wra_ref[blk]))
        i_parts.append(_dot(xs, wrx_ref[blk]))
    r = _sigmoid(jnp.concatenate(r_parts, axis=1) + bra_ref[...])
    ig = _sigmoid(jnp.concatenate(i_parts, axis=1) + brx_ref[...])
    nlam = -lam_ref[...]
    softplus = jnp.maximum(nlam, 0.0) + jnp.log1p(jnp.exp(-jnp.abs(nlam)))
    a = jnp.exp2(((-LRU_C * LOG2E) * softplus) * r)
    y = 1.0 - a * a
    u = jnp.where(y > 0.0, y * lax.rsqrt(y), 0.0) * (ig * conv)
    a_s[...] = a
    u_s[...] = jnp.where(valid, u, 0.0)

    srow = lax.broadcasted_iota(jnp.int32, (SUBLANE, w), 0)

    def group(gi, hprev):
        sl = pl.ds(pl.multiple_of(gi * SUBLANE, SUBLANE), SUBLANE)
        ag = a_s[sl, :]
        ug = u_s[sl, :]
        for d in (1, 2, 4):
            ok = srow >= d
            a_sh = jnp.where(ok, pltpu.roll(ag, d, 0), 1.0)
            u_sh = jnp.where(ok, pltpu.roll(ug, d, 0), 0.0)
            ug = ag * u_sh + ug
            ag = ag * a_sh
        hg = ug + ag * hprev
        u_s[sl, :] = hg
        return jnp.broadcast_to(hg[SUBLANE - 1:SUBLANE, :], (SUBLANE, w))

    hc_ref[...] = lax.fori_loop(0, tt // SUBLANE, group, hc_ref[...])
    out_ref[...] = (u_s[...] * _gelu_tanh(gb_ref[...].astype(F32))).astype(out_ref.dtype)


def _lru(xb, gb, cw, cb, wra, bra, wrx, brx, lam, bsz, tp, tt):
    w = xb.shape[1]
    nt = tp // tt
    row_map = lambda b, t: (b * nt + t, 0)
    const2 = lambda b, t: (0, 0)
    const3 = lambda b, t: (0, 0, 0)
    return pl.pallas_call(
        _lru_kernel,
        out_shape=jax.ShapeDtypeStruct(xb.shape, BF16),
        grid=(bsz, nt),
        in_specs=[pl.BlockSpec((tt, w), row_map), pl.BlockSpec((tt, w), row_map),
                  pl.BlockSpec((CONV_W, w), const2), pl.BlockSpec((1, w), const2),
                  pl.BlockSpec((B_BLOCKS, B_BLOCK, B_BLOCK), const3), pl.BlockSpec((1, w), const2),
                  pl.BlockSpec((B_BLOCKS, B_BLOCK, B_BLOCK), const3), pl.BlockSpec((1, w), const2),
                  pl.BlockSpec((1, w), const2)],
        out_specs=pl.BlockSpec((tt, w), row_map),
        scratch_shapes=[pltpu.VMEM((SUBLANE, w), F32), pltpu.VMEM((SUBLANE, w), F32),
                        pltpu.VMEM((tt, w), F32), pltpu.VMEM((tt, w), F32)],
        compiler_params=_cparams(("parallel", "arbitrary")),
        name="conv_rglru",
    )(xb, gb, cw, cb, wra, bra, wrx, brx, lam)


def _out_mlp_kernel(n_mix, from_input, *refs):
    n_h = 2 if from_input else 1
    h_refs, refs = refs[:n_h], refs[n_h:]
    mix_refs = refs[:n_mix]
    ln_ref = refs[n_mix]
    n_w = n_mix + 2
    w_hbm = refs[n_mix + 1:n_mix + 1 + n_w]
    out_ref = refs[n_mix + 1 + n_w]
    w_vmem = refs[n_mix + 2 + n_w:n_mix + 2 + 2 * n_w]
    sem = refs[n_mix + 2 + 2 * n_w]

    @pl.when((pl.program_id(0) == 0) & (pl.program_id(1) == 0))
    def _():
        copies = [pltpu.make_async_copy(src, dst, sem.at[i])
                  for i, (src, dst) in enumerate(zip(w_hbm, w_vmem))]
        for cp in copies:
            cp.start()
        for cp in copies:
            cp.wait()

    wo_refs, w1_ref, w2_ref = w_vmem[:n_mix], w_vmem[n_mix], w_vmem[n_mix + 1]
    res = _input_tile(*h_refs) if from_input else h_refs[0][...]
    for m_ref, w_ref in zip(mix_refs, wo_refs):
        res = res + _dot(m_ref[...], w_ref[...])
    hn = _rms_rows(res, ln_ref[...]).astype(BF16)
    a = jnp.maximum(_dot(hn, w1_ref[...]), 0.0)
    out_ref[...] = res + _dot((a * a).astype(BF16), w2_ref[...])


def _out_mlp(h_in, mixes, wouts, ln, w1, w2, tm, rows):
    d = w2.shape[1]
    n_mix = len(mixes)
    kind = rows[0]
    if kind == "all":
        r = rows[1]
        grid, out_rows = (r // tm, 1), r
        mix_spec = lambda n: pl.BlockSpec((tm, n), lambda i, j: (i, 0))
        h_specs, h_args = [mix_spec(d)], [h_in]
        out_spec = pl.BlockSpec((tm, d), lambda i, j: (i, 0))
    elif kind == "tokens":
        _, bsz, tp, seq = rows
        per_seq = seq // tm
        grid, out_rows = (bsz, per_seq), bsz * seq
        mix_spec = lambda n: pl.BlockSpec(
            (pl.Element(tm), pl.Element(n)),
            lambda b, i: (pl.multiple_of(b * tp + (tp - seq) + i * tm, LANE), 0))
        h_specs, h_args = [mix_spec(d)], [h_in]
        out_spec = pl.BlockSpec((tm, d), lambda b, i: (b * per_seq + i, 0))
    else:
        _, bsz, tp, seq = rows
        nt = tp // tm
        grid, out_rows = (bsz, nt), bsz * tp
        mix_spec = lambda n: pl.BlockSpec((tm, n), lambda b, t: (b * nt + t, 0))
        h_specs, h_args = _input_specs(seq, tm, d), list(h_in)
        out_spec = mix_spec(d)
    weights = [*wouts, w1, w2]
    in_specs = h_specs + [mix_spec(m.shape[1]) for m in mixes]
    in_specs += [pl.BlockSpec((1, d), lambda i, j: (0, 0))]
    in_specs += [pl.BlockSpec(memory_space=pl.ANY) for _ in weights]
    scratch = [pltpu.VMEM(w.shape, w.dtype) for w in weights]
    scratch.append(pltpu.SemaphoreType.DMA((len(weights),)))
    return pl.pallas_call(
        functools.partial(_out_mlp_kernel, n_mix, kind == "input"),
        out_shape=jax.ShapeDtypeStruct((out_rows, d), F32),
        grid=grid,
        in_specs=in_specs,
        out_specs=out_spec,
        scratch_shapes=scratch,
        compiler_params=_cparams(("arbitrary", "arbitrary")),
        name="out_mlp",
    )(*h_args, *mixes, ln, *weights)


OD_COLS = dict(cq=(0, 384), ckv=(384, 640), kr=(640, 768), fq=(768, 1280), fk=(1280, 1792),
               fv=(1792, 2304), ff=(2304, 2432))
OD_N = 2432


def _od_in_kernel(h_ref, ln_ref, w_ref, pos_ref, frq_ref, phs_ref, gqa_ref, gkva_ref,
                  wq_ref, wkn_ref, wv_ref, gq_ref, gk_ref, gkr_ref, gfq_ref, gfk_ref, bf_ref,
                  gm_ref, g64_ref, sel_ref,
                  qm_ref, km_ref, vmt_ref, qf_ref, kf_ref, vft_ref, fc_ref):
    t = pl.program_id(1)
    tm = h_ref.shape[1]

    @pl.when(t == 0)
    def _():
        fc_ref[...] = jnp.zeros_like(fc_ref)

    xn = _rms_rows(h_ref[0], ln_ref[...]).astype(BF16)

    def proj(name):
        a, b = OD_COLS[name]
        return _dot(xn, w_ref[:, a:b])

    z_cq, z_ckv, z_kr = proj("cq"), proj("ckv"), proj("kr")

    wide = 2 * LANE
    lane2 = lax.broadcasted_iota(jnp.int32, (tm, wide), 1) % LANE
    row2 = lax.broadcasted_iota(jnp.int32, (tm, wide), 0)
    unused_row2 = (row2 < PAD) & (t == 0)
    mask_lane = lane2 == MASK_LANE
    n_grp = LANE // C_ROPE
    tq = tm // n_grp
    pos = pos_ref[0]
    angc = phs_ref[...]
    for g in range(n_grp):
        angc = angc + pos[g * tq:(g + 1) * tq, :] * frq_ref[g:g + 1, :]
    cc = jnp.cos(angc)
    lane1 = lax.broadcasted_iota(jnp.int32, (tq, LANE), 1)
    in_lo = (lane1 >= ROPE_LO) & (lane1 < ROPE_MID)
    in_hi = (lane1 >= ROPE_MID) & (lane1 < ROPE_HI)
    cs_parts, sn_parts = [], []
    for g in range(n_grp):
        shift = (ROPE_LO - C_ROPE * g) % LANE
        c = pltpu.roll(cc, shift, 1) if shift else cc
        cs_parts.append(jnp.where(in_hi, pltpu.roll(c, HALF, 1), jnp.where(in_lo, c, 1.0)))
        sn_parts.append(jnp.where(in_lo, -pltpu.roll(c, LANE - HALF, 1), jnp.where(in_hi, c, 0.0)))
    cs = jnp.concatenate(cs_parts, axis=0)
    sn = jnp.concatenate(sn_parts, axis=0)
    cs = jnp.concatenate([cs, cs], axis=1)
    sn = jnp.concatenate([sn, sn], axis=1)
    first_half = lane2 < ROPE_MID

    def seg_rms(x, sel, gain):
        ms = _dot((x * x).astype(BF16), sel)
        return x * lax.rsqrt(ms + EPS) * gain

    def rope(y):
        sw = jnp.where(first_half, pltpu.roll(y, wide - HALF, 1), pltpu.roll(y, HALF, 1))
        return y * cs + sw * sn

    gm = gm_ref[...]
    g64 = g64_ref[...]

    cqn = _rms_rows(z_cq, gqa_ref[...]).astype(BF16)
    ckvn = _rms_rows(z_ckv, gkva_ref[...]).astype(BF16)
    qall = _dot(cqn, wq_ref[...])
    knall = _dot(ckvn, wkn_ref[...])
    scale_m = (C_NOPE + C_ROPE) ** -0.5 * LOG2E
    for h2 in range(C_HEADS // 2):
        cols = slice(h2 * wide, (h2 + 1) * wide)
        y = rope(seg_rms(qall[:, cols], gm,gq_ref[...]))
        qm_ref[0, :, cols] = jnp.where(mask_lane, 1.0, y * scale_m).astype(BF16)
    krt = rope(seg_rms(jnp.concatenate([z_kr, z_kr], axis=1), gm,gkr_ref[...]))
    krt = jnp.where(mask_lane & unused_row2, NEG, krt)
    for h2 in range(C_HEADS // 2):
        cols = slice(h2 * wide, (h2 + 1) * wide)
        y = seg_rms(knall[:, cols], gm,gk_ref[...]) + krt
        km_ref[0, :, cols] = y.astype(BF16)

    fq = proj("fq")
    fk = proj("fk")
    scale_f = D_HD ** -0.5 * LOG2E
    npair = D_HEADS // 2
    for p2 in range(npair // 2):
        cols = slice(p2 * wide, (p2 + 1) * wide)
        y = seg_rms(fq[:, cols], g64,gfq_ref[...])
        qf_ref[0, :, cols] = (y * scale_f).astype(BF16)
        y = seg_rms(fk[:, cols], g64,gfk_ref[...]).astype(BF16)
        for e in range(2):
            p = 2 * p2 + e
            kf_ref[0, :, 2 * p * LANE:(2 * p + 1) * LANE] = y[:, e * LANE:(e + 1) * LANE]

    lane = lax.broadcasted_iota(jnp.int32, (tm, LANE), 1)
    unused_row = (lax.broadcasted_iota(jnp.int32, (tm, LANE), 0) < PAD) & (t == 0)
    valid = jnp.logical_not(unused_row) & (lane < D_HEADS)
    lf = jnp.where(valid, _log_sigmoid(proj("ff") + bf_ref[...]), 0.0)
    fc = _dot_sel_l(_tril_bf16(tm), lf, 3) + fc_ref[...]
    fc_ref[...] = fc[tm - 1:tm, :]
    hi, mid, lo = _split3(jnp.where(unused_row, NEG, fc * (-LOG2E)))
    bias = _dot(hi, sel_ref[0]) + _dot(mid, sel_ref[1]) + _dot(lo, sel_ref[2])
    for p in range(npair):
        kf_ref[0, :, (2 * p + 1) * LANE:(2 * p + 2) * LANE] = bias[:, p * LANE:(p + 1) * LANE].astype(BF16)

    vft_ref[0] = proj("fv").T.astype(BF16)
    vmt_ref[0] = _dot(ckvn, wv_ref[...]).T.astype(BF16)


def _od_in(h3, pos_col, consts, tm):
    bsz, tp, d = h3.shape
    nt = tp // tm
    names = ["ln", "w", None, "frq", "phs", "gqa", "gkva", "wq", "wkn", "wv", "gq", "gk", "gkr",
             "gfq", "gfk", "bf", "gm", "g64", "sel"]
    args, in_specs = [h3], [pl.BlockSpec((1, tm, d), lambda b, t: (b, t, 0))]
    for nm in names:
        if nm is None:
            args.append(pos_col)
            in_specs.append(pl.BlockSpec((1, tm, 1), lambda b, t: (b, t, 0)))
            continue
        a = consts[nm]
        args.append(a)
        in_specs.append(pl.BlockSpec(a.shape, (lambda nd: (lambda b, t: (0,) * nd))(a.ndim)))
    hl = C_HEADS * LANE
    pk = D_HEADS * D_HD
    outs = [jax.ShapeDtypeStruct((bsz, tp, hl), BF16), jax.ShapeDtypeStruct((bsz, tp, hl), BF16),
            jax.ShapeDtypeStruct((bsz, pk, tp), BF16), jax.ShapeDtypeStruct((bsz, tp, pk), BF16),
            jax.ShapeDtypeStruct((bsz, tp, 2 * pk), BF16), jax.ShapeDtypeStruct((bsz, pk, tp), BF16)]
    rowspec = lambda n: pl.BlockSpec((1, tm, n), lambda b, t: (b, t, 0))
    tspec = pl.BlockSpec((1, pk, tm), lambda b, t: (b, 0, t))
    return pl.pallas_call(
        _od_in_kernel,
        out_shape=outs,
        grid=(bsz, nt),
        in_specs=in_specs,
        out_specs=[rowspec(hl), rowspec(hl), tspec, rowspec(pk), rowspec(2 * pk), tspec],
        scratch_shapes=[pltpu.VMEM((1, LANE), F32)],
        compiler_params=_cparams(("parallel", "arbitrary")),
        name="od_in",
    )(*args)


def _attn_kernel(fox, blk, nh, q_ref, k_ref, vt_ref, out_ref, st_ref, acc_ref, buf0, buf1, bufm,
                 mx0, mx1, mxm, *maybe_qe):
    tp = k_ref.shape[1]
    n_blocks = (tp - LANE) // blk
    half = LANE // 2

    def start_qblock(q0, nq, slot):
        st_ref[0:nh, 0:nq] = jnp.full((nh, nq), NEG, F32)
        st_ref[nh:2 * nh, 0:nq] = jnp.zeros((nh, nq), F32)
        acc_ref[:, :, 0:nq] = jnp.zeros((nh, LANE, nq), F32)
        if fox:
            qe_ref = maybe_qe[0]
            lane = lax.broadcasted_iota(jnp.int32, (nq, LANE), 1)
            for hd in range(nh):
                pair, e = divmod(hd, 2)
                q = q_ref[0, pl.ds(q0, nq), pair * LANE:(pair + 1) * LANE]
                mine = (lane >= e * half) & (lane < (e + 1) * half)
                ones = ((lane >= 3 * e) & (lane < 3 * e + 3)).astype(F32).astype(BF16)
                qe_ref[slot * nh + hd, 0:nq, :] = jnp.concatenate([jnp.where(mine, q, jnp.zeros_like(q)), ones], axis=1)

    def logits_to(buf, mx, q0, nq, k0, nk, diagonal, slot):
        if diagonal:
            ok = (lax.broadcasted_iota(jnp.int32, (nk, nq), 0)
                  <= lax.broadcasted_iota(jnp.int32, (nk, nq), 1))
        for hd in range(nh):
            if fox:
                pair = hd // 2
                s = _dot_nt(k_ref[0, pl.ds(k0, nk), pair * 2 * LANE:(pair + 1) * 2 * LANE],
                            maybe_qe[0][slot * nh + hd, 0:nq, :])
            else:
                s = _dot_nt(k_ref[0, pl.ds(k0, nk), hd * LANE:(hd + 1) * LANE],
                            q_ref[0, pl.ds(q0, nq), hd * LANE:(hd + 1) * LANE])
            if diagonal:
                s = jnp.where(ok, s, NEG)
            buf[hd, 0:nk, 0:nq] = s
            mx[hd:hd + 1, 0:nq] = jnp.max(s, axis=0, keepdims=True)

    def consume(buf, mx, nq, k0, nk):
        for hd in range(nh):
            pair = hd // 2
            vt = vt_ref[0, pair * LANE:(pair + 1) * LANE, pl.ds(k0, nk)]
            m_old = st_ref[hd:hd + 1, 0:nq]
            l_old = st_ref[nh + hd:nh + hd + 1, 0:nq]
            m_new = jnp.maximum(m_old, mx[hd:hd + 1, 0:nq])
            alpha = jnp.exp2(m_old - m_new)
            p = jnp.exp2(buf[hd, 0:nk, 0:nq] - m_new)
            st_ref[hd:hd + 1, 0:nq] = m_new
            st_ref[nh + hd:nh + hd + 1, 0:nq] = alpha * l_old + jnp.sum(p, axis=0, keepdims=True)
            acc_ref[hd, :, 0:nq] = acc_ref[hd, :, 0:nq] * alpha + _dot(vt, p.astype(BF16))

    def finish_qblock(q0, nq):
        for pair in range(nh // 2):
            a, b = 2 * pair, 2 * pair + 1
            o_a = acc_ref[a, 0:half, 0:nq] / st_ref[nh + a:nh + a + 1, 0:nq]
            o_b = acc_ref[b, half:LANE, 0:nq] / st_ref[nh + b:nh + b + 1, 0:nq]
            out_ref[0, pl.ds(q0, nq), pair * LANE:(pair + 1) * LANE] = (
                jnp.concatenate([o_a, o_b], axis=0).T.astype(out_ref.dtype))

    start_qblock(0, LANE, 0)
    logits_to(bufm, mxm, 0, LANE, 0, LANE, True, 0)
    consume(bufm, mxm, LANE, 0, LANE)
    finish_qblock(0, LANE)

    n_static = min(3, n_blocks)
    for i in range(n_static):
        q0 = LANE + i * blk
        tiles = [(bufm, mxm, 0, LANE, False)]
        for n in range(i):
            tiles.append(((buf0, buf1)[n % 2], (mx0, mx1)[n % 2], LANE + n * blk, blk, False))
        tiles.append(((buf0, buf1)[i % 2], (mx0, mx1)[i % 2], q0, blk, True))
        start_qblock(q0, blk, i + 1)
        logits_to(tiles[0][0], tiles[0][1], q0, blk, tiles[0][2], tiles[0][3], tiles[0][4], i + 1)
        for n, (buf, mx, k0, nk, _) in enumerate(tiles):
            if n + 1 < len(tiles):
                nbuf, nmx, nk0, nnk, ndiag = tiles[n + 1]
                logits_to(nbuf, nmx, q0, blk, nk0, nnk, ndiag, i + 1)
            consume(buf, mx, blk, k0, nk)
        finish_qblock(q0, blk)

    def qblock(i, carry):
        q0 = pl.multiple_of(LANE + i * blk, LANE)
        slot = n_static + 1

        def k_of(n):
            return pl.multiple_of(LANE + n * blk, LANE)

        def pair(m, c):
            logits_to(buf1, mx1, q0, blk, k_of(2 * m + 1), blk, False, slot)
            consume(buf0, mx0, blk, k_of(2 * m), blk)
            logits_to(buf0, mx0, q0, blk, k_of(2 * m + 2), blk, False, slot)
            consume(buf1, mx1, blk, k_of(2 * m + 1), blk)
            return c

        start_qblock(q0, blk, slot)
        logits_to(bufm, mxm, q0, blk, 0, LANE, False, slot)
        logits_to(buf0, mx0, q0, blk, k_of(0), blk, False, slot)
        consume(bufm, mxm, blk, 0, LANE)
        pair(0, 0)
        n_pairs = (i - 1) // 2
        lax.fori_loop(1, n_pairs, pair, 0)
        held = 2 * n_pairs

        @pl.when(i % 2 == 1)
        def _():
            logits_to(buf1, mx1, q0, blk, q0, blk, True, slot)
            consume(buf0, mx0, blk, k_of(held), blk)
            consume(buf1, mx1, blk, q0, blk)
            finish_qblock(q0, blk)

        @pl.when(i % 2 == 0)
        def _():
            logits_to(buf1, mx1, q0, blk, k_of(held + 1), blk, False, slot)
            consume(buf0, mx0, blk, k_of(held), blk)
            logits_to(buf0, mx0, q0, blk, q0, blk, True, slot)
            consume(buf1, mx1, blk, k_of(held + 1), blk)
            consume(buf0, mx0, blk, q0, blk)
            finish_qblock(q0, blk)

        return carry

    lax.fori_loop(n_static, n_blocks, qblock, 0)


def _attention(q, k, vt, fox, blk, nh):
    bsz, tp, _ = q.shape
    npair = vt.shape[1] // LANE
    g = nh // 2
    qw = (LANE if fox else 2 * LANE) * g
    assert 2 * nh <= SUBLANE and npair % g == 0
    scratch = [pltpu.VMEM((SUBLANE, blk), F32), pltpu.VMEM((nh, LANE, blk), F32),
               pltpu.VMEM((nh, blk, blk), F32), pltpu.VMEM((nh, blk, blk), F32),
               pltpu.VMEM((nh, LANE, blk), F32),
               pltpu.VMEM((SUBLANE, blk), F32), pltpu.VMEM((SUBLANE, blk), F32),
               pltpu.VMEM((SUBLANE, blk), F32)]
    if fox:
        scratch.append(pltpu.VMEM((5 * nh, blk, 2 * LANE), BF16))
    return pl.pallas_call(
        functools.partial(_attn_kernel, fox, blk, nh),
        out_shape=jax.ShapeDtypeStruct((bsz, tp, npair * LANE), BF16),
        grid=(bsz, npair // g),
        in_specs=[pl.BlockSpec((1, tp, qw), lambda b, p: (b, 0, p)),
                  pl.BlockSpec((1, tp, 2 * LANE * g), lambda b, p: (b, 0, p)),
                  pl.BlockSpec((1, LANE * g, tp), lambda b, p: (b, p, 0))],
        out_specs=pl.BlockSpec((1, tp, LANE * g), lambda b, p: (b, 0, p)),
        scratch_shapes=scratch,
        compiler_params=_cparams(("parallel", "parallel")),
        name="fox_attn" if fox else "mla_attn",
    )(q, k, vt)


def _place_cols(w, width, offset):
    k, n = w.shape
    return jnp.pad(w, ((0, 0), (offset, width - offset - n)))


def _ev_consts(w_in, b_if):
    nq, nv = A_HEADS * A_DQK, A_HEADS * A_DV
    o = 0
    wq = w_in[:, o:o + nq]; o += nq
    wk = w_in[:, o:o + nq]; o += nq
    wv = w_in[:, o:o + nv]; o += nv
    wo = w_in[:, o:o + nv]; o += nv
    wg = w_in[:, o:o + 2 * A_HEADS]; o += 2 * A_HEADS
    wxb = w_in[:, o:o + 1024]; o += 1024
    wgb = w_in[:, o:o + 1024]
    w = jnp.concatenate([wq, wk, wv, wo, _place_cols(wg, LANE, 0), wxb, wgb], axis=1).astype(BF16)
    bif = _place_cols(b_if[None, :], LANE, 0)
    return w, bif


def _head_tile_cols(w, n_heads, per_head, take, offset):
    k = w.shape[0]
    wh = w.reshape(k, n_heads, per_head)[:, :, take[0]:take[0] + take[1]]
    wh = jnp.pad(wh, ((0, 0), (0, 0), (offset, LANE - offset - take[1])))
    return wh.reshape(k, n_heads * LANE)


def _row128(parts):
    r = jnp.zeros((LANE,), F32)
    for off, v in parts:
        r = r.at[off:off + v.shape[0]].set(v.astype(F32))
    return r[None, :]


def _od_consts(p):
    c = {}
    w_in = p["w_in"]
    o = 0
    wcq = w_in[:, o:o + C_Q_LORA]; o += C_Q_LORA
    wckv = w_in[:, o:o + C_KV_LORA]; o += C_KV_LORA
    wkr = w_in[:, o:o + C_ROPE]; o += C_ROPE
    nf = D_HEADS * D_HD
    wfq = w_in[:, o:o + nf]; o += nf
    wfk = w_in[:, o:o + nf]; o += nf
    wfv = w_in[:, o:o + nf]; o += nf
    wff = w_in[:, o:o + D_HEADS]
    c["w"] = jnp.concatenate([wcq, wckv, _place_cols(wkr, LANE, ROPE_LO), wfq, wfk, wfv,
                              _place_cols(wff, LANE, 0)], axis=1).astype(BF16)
    c["ln"] = p["ln"][None, :]
    freqs = ROPE_THETA ** (-jnp.arange(HALF, dtype=F32) / HALF)
    n_grp = LANE // C_ROPE
    c["frq"] = jnp.concatenate([_row128([(C_ROPE * g, freqs), (C_ROPE * g + HALF, freqs)])
                                for g in range(n_grp)], axis=0)
    c["phs"] = _row128([(C_ROPE * g + HALF, jnp.full((HALF,), -0.5 * math.pi, F32)) for g in range(n_grp)])
    c["gqa"] = p["g_qa"][None, :]
    c["gkva"] = p["g_kva"][None, :]
    dq = C_NOPE + C_ROPE
    wuq, wukv = p["w_uq"], p["w_ukv"]
    c["wq"] = (_head_tile_cols(wuq, C_HEADS, dq, (0, dq), 0)).astype(BF16)
    c["wkn"] = _head_tile_cols(wukv, C_HEADS, C_NOPE + C_V, (0, C_NOPE), 0).astype(BF16)
    c["wv"] = wukv.reshape(-1, C_HEADS, C_NOPE + C_V)[:, :, C_NOPE:].reshape(-1, C_HEADS * C_V).astype(BF16)
    c["gq"] = _row128([(0, p["g_qn"]), (ROPE_LO, p["g_qr"])])
    c["gk"] = _row128([(0, p["g_kn"])])
    c["gkr"] = _row128([(ROPE_LO, p["g_kr"])])
    c["gfq"] = _row128([(0, p["g_fq"]), (D_HD, p["g_fq"])])
    c["gfk"] = _row128([(0, p["g_fk"]), (D_HD, p["g_fk"])])
    c["bf"] = _row128([(0, p["b_f"])])
    seg = np.where(np.arange(LANE) < C_NOPE, 0, np.where(np.arange(LANE) < ROPE_HI, 1, 2))
    inv_len = np.where(seg == 0, 1.0 / C_NOPE, 1.0 / C_ROPE)[None, :]
    c["gm"] = jnp.asarray((seg[:, None] == seg[None, :]) * inv_len, F32).astype(BF16)
    seg64 = np.arange(LANE) // D_HD
    c["g64"] = jnp.asarray((seg64[:, None] == seg64[None, :]) * (1.0 / D_HD), F32).astype(BF16)
    sel = np.zeros((3, LANE, (D_HEADS // 2) * LANE), np.float32)
    for h in range(D_HEADS):
        for part in range(3):
            sel[part, h, (h // 2) * LANE + 3 * (h % 2) + part] = 1.0
    c["sel"] = jnp.asarray(sel).astype(BF16)
    for nm in ("gq", "gk", "gkr", "gfq", "gfk"):
        c[nm] = jnp.tile(c[nm], (1, 2))
    for nm in ("gm", "g64"):
        z = jnp.zeros_like(c[nm])
        c[nm] = jnp.block([[c[nm], z], [z, c[nm]]])
    return c


def _time_tile(tp):
    return 3 * LANE if tp % (3 * LANE) == 0 else LANE


def kernel(x, positions, meta, ev_ln, ev_w_in, ev_b_if, ev_a_norm, ev_conv_w, ev_conv_b, ev_w_ra, ev_b_ra, ev_w_rx, ev_b_rx, ev_lam, ev_w_out, od_ln, od_w_in, od_b_f, od_g_qa, od_g_kva, od_w_uq, od_w_ukv, od_g_qn, od_g_qr, od_g_kn, od_g_kr, od_g_fq, od_g_fk, od_w_out, mlp_ln, w_ff1, w_ff2):
    bsz, seq, d = x.shape
    tp = LANE + seq
    r = bsz * tp
    tm = 512 if seq % 512 == 0 else LANE
    tt = _time_tile(tp)

    x2 = x.reshape(bsz * seq, d)
    mb = jnp.concatenate([jnp.zeros((PAD, d), x.dtype), meta.astype(x.dtype)], axis=0)
    pos_full = jnp.concatenate([jnp.zeros((bsz, PAD), jnp.int32),
                                jnp.broadcast_to(jnp.arange(N_META, dtype=jnp.int32)[None, :], (bsz, N_META)),
                                positions + N_META], axis=1)
    pos_col = pos_full.astype(F32)[:, :, None]

    e = 0
    w_a, bif = _ev_consts(ev_w_in[e], ev_b_if[e])
    q, k, v, o, g, xb, gb = _ev_in(x2, mb, ev_ln[e][None, :], w_a, bsz, tp, tt)
    ha = _mlstm(q, k, v, o, g, bif, ev_a_norm[e][None, :], bsz, tp, tt)
    hb = _lru(xb, gb, ev_conv_w[e], ev_conv_b[e][None, :], ev_w_ra[e].astype(BF16), ev_b_ra[e][None, :],
              ev_w_rx[e].astype(BF16), ev_b_rx[e][None, :], ev_lam[e][None, :], bsz, tp, tt)
    nv = A_HEADS * A_DV
    wo = ev_w_out[e].astype(BF16)
    h2 = _out_mlp((x2, mb), [ha, hb], [wo[:nv], wo[nv:]], mlp_ln[0][None, :],
                  w_ff1[0].astype(BF16), w_ff2[0].astype(BF16), tt, ("input", bsz, tp, seq))

    oi = 0
    consts = _od_consts(dict(ln=od_ln[oi], w_in=od_w_in[oi], b_f=od_b_f[oi], g_qa=od_g_qa[oi],
                             g_kva=od_g_kva[oi], w_uq=od_w_uq[oi], w_ukv=od_w_ukv[oi],
                             g_qn=od_g_qn[oi], g_qr=od_g_qr[oi], g_kn=od_g_kn[oi], g_kr=od_g_kr[oi],
                             g_fq=od_g_fq[oi], g_fk=od_g_fk[oi]))
    qm, km, vmt, qf, kf, vft = _od_in(h2.reshape(bsz, tp, d), pos_col, consts, tt)
    blk = 512 if seq % 512 == 0 else LANE
    hc = _attention(qm, km, vmt, False, blk, 4)
    hd = _attention(qf, kf, vft, True, blk, 4)
    nc = C_HEADS * C_V
    wo = od_w_out[oi].astype(BF16)
    out = _out_mlp(h2, [hc.reshape(r, nc), hd.reshape(r, nc)], [wo[:nc], wo[nc:]], mlp_ln[1][None, :],
                   w_ff1[1].astype(BF16), w_ff2[1].astype(BF16), tm, ("tokens", bsz, tp, seq))
    return out.reshape(bsz, seq, d)
```

```python
import functools
import math

import numpy as np
import jax
import jax.numpy as jnp
from jax import lax
from jax.experimental import pallas as pl
from jax.experimental.pallas import tpu as pltpu

F32 = jnp.float32
BF16 = jnp.bfloat16

N_META = 16
EPS = 1e-6
A_HEADS = 4
A_DQK = 128
A_DV = 256
B_BLOCKS = 8
B_BLOCK = 128
CONV_W = 4
LRU_C = 8.0
C_HEADS = 8
C_Q_LORA = 384
C_KV_LORA = 256
C_NOPE = 64
C_ROPE = 32
C_V = 64
ROPE_THETA = 10000.0
D_HEADS = 8
D_HD = 64

LANE = 128
SUBLANE = 8
VMEM_LIMIT = 56 * 1024 * 1024

PAD = LANE - N_META
NEG = -1e30
LOG2E = math.log2(math.e)

HALF = C_ROPE // 2
ROPE_LO = C_NOPE
ROPE_MID = C_NOPE + HALF
ROPE_HI = C_NOPE + C_ROPE
MASK_LANE = ROPE_HI


def _cparams(sem):
    return pltpu.CompilerParams(dimension_semantics=sem, vmem_limit_bytes=VMEM_LIMIT)


def _dot(a, b):
    return jnp.dot(a, b, preferred_element_type=F32)


def _dot_nt(a, b):
    return lax.dot_general(a, b, (((1,), (1,)), ((), ())), preferred_element_type=F32)


def _split2(x):
    hi = x.astype(BF16)
    lo = (x - hi.astype(F32)).astype(BF16)
    return hi, lo


def _split3(x):
    hi = x.astype(BF16)
    r = x - hi.astype(F32)
    mid = r.astype(BF16)
    lo = (r - mid.astype(F32)).astype(BF16)
    return hi, mid, lo


def _dot_sel_r(x, sel, parts):
    pieces = _split3(x) if parts == 3 else _split2(x)
    acc = _dot(pieces[0], sel)
    for p in pieces[1:]:
        acc = acc + _dot(p, sel)
    return acc


def _dot_sel_l(sel, x, parts):
    pieces = _split3(x) if parts == 3 else _split2(x)
    acc = _dot(sel, pieces[0])
    for p in pieces[1:]:
        acc = acc + _dot(sel, p)
    return acc


def _rms_rows(x, g):
    ms = jnp.mean(x * x, axis=-1, keepdims=True)
    return x * lax.rsqrt(ms + EPS) * g


def _log_sigmoid(x):
    return -(jnp.maximum(-x, 0.0) + jnp.log1p(jnp.exp(-jnp.abs(x))))


def _sigmoid(x):
    return 0.5 * jnp.tanh(0.5 * x) + 0.5


def _tril_bf16(n):
    r = lax.broadcasted_iota(jnp.int32, (n, n), 0)
    c = lax.broadcasted_iota(jnp.int32, (n, n), 1)
    return jnp.where(r >= c, 1.0, 0.0).astype(BF16)


EV_SLABS = (("q", 512, BF16), ("k", 512, BF16), ("v", 1024, BF16), ("o", 1024, BF16),
            ("g", 128, F32), ("xb", 1024, BF16), ("gb", 1024, BF16))


def _input_tile(x_ref, mb_ref):
    xt = x_ref[...]
    first = jnp.concatenate([mb_ref[...], xt[:xt.shape[0] - LANE]], axis=0)
    return jnp.where(pl.program_id(1) == 0, first, xt)


def _input_specs(seq, tt, d):
    x_spec = pl.BlockSpec(
        (pl.Element(tt), pl.Element(d)),
        lambda b, t: (pl.multiple_of(b * seq + jnp.maximum(t * tt - LANE, 0), LANE), 0))
    return [x_spec, pl.BlockSpec((LANE, d), lambda b, t: (0, 0))]


def _ev_in_kernel(x_ref, mb_ref, ln_ref, w_ref, *out_refs):
    xn = _rms_rows(_input_tile(x_ref, mb_ref), ln_ref[...]).astype(BF16)
    off = 0
    for (_, width, dt), ref in zip(EV_SLABS, out_refs):
        ref[...] = _dot(xn, w_ref[:, off:off + width]).astype(dt)
        off += width


def _ev_in(x2, mb, ln, w, bsz, tp, tt):
    d = x2.shape[1]
    n = w.shape[1]
    nt = tp // tt
    outs = [jax.ShapeDtypeStruct((bsz * tp, width), dt) for (_, width, dt) in EV_SLABS]
    return pl.pallas_call(
        _ev_in_kernel,
        out_shape=outs,
        grid=(bsz, nt),
        in_specs=_input_specs(x2.shape[0] // bsz, tt, d) + [
            pl.BlockSpec((1, d), lambda b, t: (0, 0)),
            pl.BlockSpec((d, n), lambda b, t: (0, 0))],
        out_specs=[pl.BlockSpec((tt, width), lambda b, t: (b * nt + t, 0)) for (_, width, _) in EV_SLABS],
        compiler_params=_cparams(("parallel", "parallel")),
        name="ev_in",
    )(x2, mb, ln, w)


def _mlstm_kernel(q_ref, k_ref, v_ref, o_ref, g_ref, bif_ref, an_ref, out_ref, cn_ref, m_ref):
    c = pl.program_id(1)
    L = q_ref.shape[0]
    dv_ext = A_DV + LANE

    @pl.when(c == 0)
    def _():
        cn_ref[...] = jnp.zeros_like(cn_ref)
        m_ref[...] = jnp.zeros_like(m_ref)

    lane = lax.broadcasted_iota(jnp.int32, (L, LANE), 1)
    row = lax.broadcasted_iota(jnp.int32, (L, LANE), 0)
    valid = (row >= PAD) | (c > 0)
    g = g_ref[...] + bif_ref[...]
    li = jnp.where(valid, g, NEG)
    lf = jnp.where(valid, _log_sigmoid(g), 0.0)
    is_f = (lane >= A_HEADS) & (lane < 2 * A_HEADS)
    lf = jnp.where(is_f, lf, 0.0)
    tril = _tril_bf16(L)
    bcum = _dot_sel_l(tril, lf, 2)
    li_t = li.T
    bcum_t = bcum.T

    rr = lax.broadcasted_iota(jnp.int32, (L, L), 0)
    cc = lax.broadcasted_iota(jnp.int32, (L, L), 1)
    causal = rr >= cc
    e0 = jnp.where(lane == 0, 1.0, 0.0).astype(BF16)
    scale = A_DQK ** -0.5

    heads = []
    for h in range(A_HEADS):
        qs = (q_ref[:, h * A_DQK:(h + 1) * A_DQK].astype(F32) * scale).astype(BF16)
        kk = k_ref[:, h * A_DQK:(h + 1) * A_DQK]
        vext = jnp.concatenate([v_ref[:, h * A_DV:(h + 1) * A_DV], e0], axis=1)
        i_col = li[:, h:h + 1]
        b_col = bcum[:, A_HEADS + h:A_HEADS + h + 1]
        i_row = li_t[h:h + 1, :]
        b_row = bcum_t[A_HEADS + h:A_HEADS + h + 1, :]
        m_state = m_ref[:, h:h + 1]
        cn = cn_ref[h]
        qk = _dot_nt(qs, kk)
        qc = _dot(qs, cn.astype(BF16))

        gsum = b_col[L - 1:L, :]
        dk_col = gsum - b_col + i_col
        dk_row = gsum - b_row + i_row
        m_new = jnp.maximum(gsum + m_state, jnp.max(dk_row, axis=1, keepdims=True))
        wk = jnp.exp(dk_col - m_new)
        sc = jnp.exp(gsum + m_state - m_new)
        kw_t = (kk.astype(F32) * wk).T.astype(BF16)
        cn_ref[h] = sc * cn + _dot(kw_t, vext)
        m_ref[:, h:h + 1] = m_new
        heads.append((qk, qc, vext, b_col, b_row, i_row, m_state))

    weighted = []
    for qk, qc, vext, b_col, b_row, i_row, m_state in heads:
        dmat = jnp.where(causal, b_col - b_row + i_row, NEG)
        inter = b_col + m_state
        m_row = jnp.maximum(inter, jnp.max(dmat, axis=1, keepdims=True))
        w_intra = jnp.exp(dmat - m_row)
        w_inter = jnp.exp(inter - m_row)
        s = (qk * w_intra).astype(BF16)
        weighted.append((w_inter * qc + _dot(s, vext), m_row))

    for h, (numden, m_row) in enumerate(weighted):
        num = numden[:, :A_DV]
        den = numden[:, A_DV:A_DV + 1]
        hh = num / jnp.maximum(jnp.abs(den), jnp.exp(-m_row))
        hn = _rms_rows(hh, an_ref[...])
        og = _sigmoid(o_ref[:, h * A_DV:(h + 1) * A_DV].astype(F32))
        out_ref[:, h * A_DV:(h + 1) * A_DV] = (hn * og).astype(out_ref.dtype)


def _mlstm(q, k, v, o, g, bif, an, bsz, tp, L):
    nq, nv = A_HEADS * A_DQK, A_HEADS * A_DV
    nc = tp // L
    row_map = lambda b, c: (b * nc + c, 0)
    const = lambda b, c: (0, 0)
    return pl.pallas_call(
        _mlstm_kernel,
        out_shape=jax.ShapeDtypeStruct((bsz * tp, nv), BF16),
        grid=(bsz, nc),
        in_specs=[pl.BlockSpec((L, nq), row_map), pl.BlockSpec((L, nq), row_map),
                  pl.BlockSpec((L, nv), row_map), pl.BlockSpec((L, nv), row_map),
                  pl.BlockSpec((L, LANE), row_map),
                  pl.BlockSpec((1, LANE), const), pl.BlockSpec((1, A_DV), const)],
        out_specs=pl.BlockSpec((L, nv), row_map),
        scratch_shapes=[pltpu.VMEM((A_HEADS, A_DQK, A_DV + LANE), F32),
                        pltpu.VMEM((1, LANE), F32)],
        compiler_params=_cparams(("parallel", "arbitrary")),
        name="mlstm",
    )(q, k, v, o, g, bif, an)


def _gelu_tanh(x):
    c0 = math.sqrt(2.0 / math.pi)
    return x * (0.5 * jnp.tanh(x * (c0 + (c0 * 0.044715) * (x * x))) + 0.5)


def _lru_kernel(xb_ref, gb_ref, cw_ref, cb_ref, wra_ref, bra_ref, wrx_ref, brx_ref, lam_ref,
                out_ref, hist_ref, hc_ref, a_s, u_s):
    t = pl.program_id(1)
    tt, w = xb_ref.shape

    @pl.when(t == 0)
    def _():
        hist_ref[...] = jnp.zeros_like(hist_ref)
        hc_ref[...] = jnp.zeros_like(hc_ref)

    row = lax.broadcasted_iota(jnp.int32, (tt, w), 0)
    valid = (row >= PAD) | (t > 0)
    x = xb_ref[...].astype(F32)
    xfull = jnp.concatenate([hist_ref[...], x], axis=0)
    conv = cb_ref[...] + cw_ref[CONV_W - 1:CONV_W, :] * x
    for d in range(1, CONV_W):
        shifted = pltpu.roll(xfull, d, 0)[SUBLANE:, :]
        conv = conv + cw_ref[CONV_W - 1 - d:CONV_W - d, :] * shifted
    hist_ref[...] = x[tt - SUBLANE:, :]

    xcb = conv.astype(BF16)
    r_parts, i_parts = [], []
    for blk in range(B_BLOCKS):
        xs = xcb[:, blk * B_BLOCK:(blk + 1) * B_BLOCK]
        r_parts.append(_dot(xs, wra_ref[blk]))
        i_parts.append(_dot(xs, wrx_ref[blk]))
    r = 0.5 * jnp.tanh(jnp.concatenate(r_parts, axis=1) + bra_ref[...]) + 0.5
    ig = 0.5 * jnp.tanh(jnp.concatenate(i_parts, axis=1) + brx_ref[...]) + 0.5
    nlam = -lam_ref[...]
    softplus = jnp.maximum(nlam, 0.0) + jnp.log1p(jnp.exp(-jnp.abs(nlam)))
    a = jnp.exp2(((-LRU_C * LOG2E) * softplus) * r)
    y = 1.0 - a * a
    u = jnp.where(y > 0.0, y * lax.rsqrt(y), 0.0) * (ig * conv)
    a_s[...] = a
    u_s[...] = jnp.where(valid, u, 0.0)

    srow = lax.broadcasted_iota(jnp.int32, (SUBLANE, w), 0)

    def group(gi, hprev):
        sl = pl.ds(pl.multiple_of(gi * SUBLANE, SUBLANE), SUBLANE)
        ag = a_s[sl, :]
        ug = u_s[sl, :]
        for d in (1, 2, 4):
            ok = srow >= d
            a_sh = jnp.where(ok, pltpu.roll(ag, d, 0), 1.0)
            u_sh = jnp.where(ok, pltpu.roll(ug, d, 0), 0.0)
            ug = ag * u_sh + ug
            ag = ag * a_sh
        hg = ug + ag * hprev
        u_s[sl, :] = hg
        return jnp.broadcast_to(hg[SUBLANE - 1:SUBLANE, :], (SUBLANE, w))

    hc_ref[...] = lax.fori_loop(0, tt // SUBLANE, group, hc_ref[...], unroll=4)
    out_ref[...] = (u_s[...] * _gelu_tanh(gb_ref[...].astype(F32))).astype(out_ref.dtype)


def _lru(xb, gb, cw, cb, wra, bra, wrx, brx, lam, bsz, tp, tt):
    w = xb.shape[1]
    nt = tp // tt
    row_map = lambda b, t: (b * nt + t, 0)
    const2 = lambda b, t: (0, 0)
    const3 = lambda b, t: (0, 0, 0)
    return pl.pallas_call(
        _lru_kernel,
        out_shape=jax.ShapeDtypeStruct(xb.shape, BF16),
        grid=(bsz, nt),
        in_specs=[pl.BlockSpec((tt, w), row_map), pl.BlockSpec((tt, w), row_map),
                  pl.BlockSpec((CONV_W, w), const2), pl.BlockSpec((1, w), const2),
                  pl.BlockSpec((B_BLOCKS, B_BLOCK, B_BLOCK), const3), pl.BlockSpec((1, w), const2),
                  pl.BlockSpec((B_BLOCKS, B_BLOCK, B_BLOCK), const3), pl.BlockSpec((1, w), const2),
                  pl.BlockSpec((1, w), const2)],
        out_specs=pl.BlockSpec((tt, w), row_map),
        scratch_shapes=[pltpu.VMEM((SUBLANE, w), F32), pltpu.VMEM((SUBLANE, w), F32),
                        pltpu.VMEM((tt, w), F32), pltpu.VMEM((tt, w), F32)],
        compiler_params=_cparams(("parallel", "arbitrary")),
        name="conv_rglru",
    )(xb, gb, cw, cb, wra, bra, wrx, brx, lam)


def _out_mlp_kernel(n_mix, from_input, *refs):
    n_h = 2 if from_input else 1
    h_refs, refs = refs[:n_h], refs[n_h:]
    mix_refs = refs[:n_mix]
    ln_ref = refs[n_mix]
    n_w = n_mix + 2
    w_hbm = refs[n_mix + 1:n_mix + 1 + n_w]
    out_ref = refs[n_mix + 1 + n_w]
    w_vmem = refs[n_mix + 2 + n_w:n_mix + 2 + 2 * n_w]
    sem = refs[n_mix + 2 + 2 * n_w]

    @pl.when((pl.program_id(0) == 0) & (pl.program_id(1) == 0))
    def _():
        copies = [pltpu.make_async_copy(src, dst, sem.at[i])
                  for i, (src, dst) in enumerate(zip(w_hbm, w_vmem))]
        for cp in copies:
            cp.start()
        for cp in copies:
            cp.wait()

    wo_refs, w1_ref, w2_ref = w_vmem[:n_mix], w_vmem[n_mix], w_vmem[n_mix + 1]
    res = _input_tile(*h_refs) if from_input else h_refs[0][...]
    for m_ref, w_ref in zip(mix_refs, wo_refs):
        res = res + _dot(m_ref[...], w_ref[...])
    hn = _rms_rows(res, ln_ref[...]).astype(BF16)
    a = jnp.maximum(_dot(hn, w1_ref[...]), 0.0)
    out_ref[...] = res + _dot((a * a).astype(BF16), w2_ref[...])


def _out_mlp(h_in, mixes, wouts, ln, w1, w2, tm, rows):
    d = w2.shape[1]
    n_mix = len(mixes)
    kind = rows[0]
    if kind == "all":
        r = rows[1]
        grid, out_rows = (r // tm, 1), r
        mix_spec = lambda n: pl.BlockSpec((tm, n), lambda i, j: (i, 0))
        h_specs, h_args = [mix_spec(d)], [h_in]
        out_spec = pl.BlockSpec((tm, d), lambda i, j: (i, 0))
    elif kind == "tokens":
        _, bsz, tp, seq = rows
        per_seq = seq // tm
        grid, out_rows = (bsz, per_seq), bsz * seq
        mix_spec = lambda n: pl.BlockSpec(
            (pl.Element(tm), pl.Element(n)),
            lambda b, i: (pl.multiple_of(b * tp + (tp - seq) + i * tm, LANE), 0))
        h_specs, h_args = [mix_spec(d)], [h_in]
        out_spec = pl.BlockSpec((tm, d), lambda b, i: (b * per_seq + i, 0))
    else:
        _, bsz, tp, seq = rows
        nt = tp // tm
        grid, out_rows = (bsz, nt), bsz * tp
        mix_spec = lambda n: pl.BlockSpec((tm, n), lambda b, t: (b * nt + t, 0))
        h_specs, h_args = _input_specs(seq, tm, d), list(h_in)
        out_spec = mix_spec(d)
    weights = [*wouts, w1, w2]
    in_specs = h_specs + [mix_spec(m.shape[1]) for m in mixes]
    in_specs += [pl.BlockSpec((1, d), lambda i, j: (0, 0))]
    in_specs += [pl.BlockSpec(memory_space=pl.ANY) for _ in weights]
    scratch = [pltpu.VMEM(w.shape, w.dtype) for w in weights]
    scratch.append(pltpu.SemaphoreType.DMA((len(weights),)))
    return pl.pallas_call(
        functools.partial(_out_mlp_kernel, n_mix, kind == "input"),
        out_shape=jax.ShapeDtypeStruct((out_rows, d), F32),
        grid=grid,
        in_specs=in_specs,
        out_specs=out_spec,
        scratch_shapes=scratch,
        compiler_params=_cparams(("arbitrary", "arbitrary")),
        name="out_mlp",
    )(*h_args, *mixes, ln, *weights)


OD_COLS = dict(cq=(0, 384), ckv=(384, 640), kr=(640, 768), fq=(768, 1280), fk=(1280, 1792),
               fv=(1792, 2304), ff=(2304, 2432))
OD_N = 2432


def _od_in_kernel(h_ref, ln_ref, w_ref, pos_ref, frq_ref, phs_ref, gqa_ref, gkva_ref,
                  wq_ref, wkn_ref, wv_ref, gq_ref, gk_ref, gkr_ref, gfq_ref, gfk_ref, bf_ref,
                  gm_ref, g64_ref, sel_ref,
                  qm_ref, km_ref, vmt_ref, qf_ref, kf_ref, vft_ref, fc_ref):
    t = pl.program_id(1)
    tm = h_ref.shape[1]

    @pl.when(t == 0)
    def _():
        fc_ref[...] = jnp.zeros_like(fc_ref)

    xn = _rms_rows(h_ref[0], ln_ref[...]).astype(BF16)

    def proj(name):
        a, b = OD_COLS[name]
        return _dot(xn, w_ref[:, a:b])

    z_cq, z_ckv, z_kr = proj("cq"), proj("ckv"), proj("kr")

    wide = 2 * LANE
    lane2 = lax.broadcasted_iota(jnp.int32, (tm, wide), 1) % LANE
    row2 = lax.broadcasted_iota(jnp.int32, (tm, wide), 0)
    unused_row2 = (row2 < PAD) & (t == 0)
    mask_lane = lane2 == MASK_LANE
    n_grp = LANE // C_ROPE
    tq = tm // n_grp
    pos = pos_ref[0]
    angc = phs_ref[...]
    for g in range(n_grp):
        angc = angc + pos[g * tq:(g + 1) * tq, :] * frq_ref[g:g + 1, :]
    cc = jnp.cos(angc)
    lane1 = lax.broadcasted_iota(jnp.int32, (tq, LANE), 1)
    in_lo = (lane1 >= ROPE_LO) & (lane1 < ROPE_MID)
    in_hi = (lane1 >= ROPE_MID) & (lane1 < ROPE_HI)
    cs_parts, sn_parts = [], []
    for g in range(n_grp):
        shift = (ROPE_LO - C_ROPE * g) % LANE
        c = pltpu.roll(cc, shift, 1) if shift else cc
        cs_parts.append(jnp.where(in_hi, pltpu.roll(c, HALF, 1), jnp.where(in_lo, c, 1.0)))
        sn_parts.append(jnp.where(in_lo, -pltpu.roll(c, LANE - HALF, 1), jnp.where(in_hi, c, 0.0)))
    cs = jnp.concatenate(cs_parts, axis=0)
    sn = jnp.concatenate(sn_parts, axis=0)
    cs = jnp.concatenate([cs, cs], axis=1)
    sn = jnp.concatenate([sn, sn], axis=1)
    first_half = lane2 < ROPE_MID

    def seg_rms(x, sel, gain):
        ms = _dot((x * x).astype(BF16), sel)
        return x * lax.rsqrt(ms + EPS) * gain

    def rope(y):
        sw = jnp.where(first_half, pltpu.roll(y, wide - HALF, 1), pltpu.roll(y, HALF, 1))
        return y * cs + sw * sn

    gm = gm_ref[...]
    g64 = g64_ref[...]

    cqn = _rms_rows(z_cq, gqa_ref[...]).astype(BF16)
    ckvn = _rms_rows(z_ckv, gkva_ref[...]).astype(BF16)
    qall = _dot(cqn, wq_ref[...])
    knall = _dot(ckvn, wkn_ref[...])
    scale_m = (C_NOPE + C_ROPE) ** -0.5 * LOG2E
    for h2 in range(C_HEADS // 2):
        cols = slice(h2 * wide, (h2 + 1) * wide)
        y = rope(seg_rms(qall[:, cols], gm,gq_ref[...]))
        qm_ref[0, :, cols] = jnp.where(mask_lane, 1.0, y * scale_m).astype(BF16)
    krt = rope(seg_rms(jnp.concatenate([z_kr, z_kr], axis=1), gm,gkr_ref[...]))
    krt = jnp.where(mask_lane & unused_row2, NEG, krt)
    for h2 in range(C_HEADS // 2):
        cols = slice(h2 * wide, (h2 + 1) * wide)
        y = seg_rms(knall[:, cols], gm,gk_ref[...]) + krt
        km_ref[0, :, cols] = y.astype(BF16)

    fq = proj("fq")
    fk = proj("fk")
    scale_f = D_HD ** -0.5 * LOG2E
    npair = D_HEADS // 2
    for p2 in range(npair // 2):
        cols = slice(p2 * wide, (p2 + 1) * wide)
        y = seg_rms(fq[:, cols], g64,gfq_ref[...])
        qf_ref[0, :, cols] = (y * scale_f).astype(BF16)
        y = seg_rms(fk[:, cols], g64,gfk_ref[...]).astype(BF16)
        for e in range(2):
            p = 2 * p2 + e
            kf_ref[0, :, 2 * p * LANE:(2 * p + 1) * LANE] = y[:, e * LANE:(e + 1) * LANE]

    lane = lax.broadcasted_iota(jnp.int32, (tm, LANE), 1)
    unused_row = (lax.broadcasted_iota(jnp.int32, (tm, LANE), 0) < PAD) & (t == 0)
    valid = jnp.logical_not(unused_row) & (lane < D_HEADS)
    lf = jnp.where(valid, _log_sigmoid(proj("ff") + bf_ref[...]), 0.0)
    fc = _dot_sel_l(_tril_bf16(tm), lf, 2) + fc_ref[...]
    fc_ref[...] = fc[tm - 1:tm, :]
    hi, mid, lo = _split3(jnp.where(unused_row, NEG, fc * (-LOG2E)))
    bias = _dot(hi, sel_ref[0]) + _dot(mid, sel_ref[1]) + _dot(lo, sel_ref[2])
    for p in range(npair):
        kf_ref[0, :, (2 * p + 1) * LANE:(2 * p + 2) * LANE] = bias[:, p * LANE:(p + 1) * LANE].astype(BF16)

    vft_ref[0] = proj("fv").T.astype(BF16)
    vmt_ref[0] = _dot(ckvn, wv_ref[...]).T.astype(BF16)


def _od_in(h3, pos_col, consts, tm):
    bsz, tp, d = h3.shape
    nt = tp // tm
    names = ["ln", "w", None, "frq", "phs", "gqa", "gkva", "wq", "wkn", "wv", "gq", "gk", "gkr",
             "gfq", "gfk", "bf", "gm", "g64", "sel"]
    args, in_specs = [h3], [pl.BlockSpec((1, tm, d), lambda b, t: (b, t, 0))]
    for nm in names:
        if nm is None:
            args.append(pos_col)
            in_specs.append(pl.BlockSpec((1, tm, 1), lambda b, t: (b, t, 0)))
            continue
        a = consts[nm]
        args.append(a)
        in_specs.append(pl.BlockSpec(a.shape, (lambda nd: (lambda b, t: (0,) * nd))(a.ndim)))
    hl = C_HEADS * LANE
    pk = D_HEADS * D_HD
    outs = [jax.ShapeDtypeStruct((bsz, tp, hl), BF16), jax.ShapeDtypeStruct((bsz, tp, hl), BF16),
            jax.ShapeDtypeStruct((bsz, pk, tp), BF16), jax.ShapeDtypeStruct((bsz, tp, pk), BF16),
            jax.ShapeDtypeStruct((bsz, tp, 2 * pk), BF16), jax.ShapeDtypeStruct((bsz, pk, tp), BF16)]
    rowspec = lambda n: pl.BlockSpec((1, tm, n), lambda b, t: (b, t, 0))
    tspec = pl.BlockSpec((1, pk, tm), lambda b, t: (b, 0, t))
    return pl.pallas_call(
        _od_in_kernel,
        out_shape=outs,
        grid=(bsz, nt),
        in_specs=in_specs,
        out_specs=[rowspec(hl), rowspec(hl), tspec, rowspec(pk), rowspec(2 * pk), tspec],
        scratch_shapes=[pltpu.VMEM((1, LANE), F32)],
        compiler_params=_cparams(("parallel", "arbitrary")),
        name="od_in",
    )(*args)


def _attn_kernel(fox, blk, nh, q_ref, k_ref, vt_ref, out_ref, st_ref, acc_ref, buf0, buf1, bufm,
                 mx0, mx1, mxm, *maybe_qe):
    tp = k_ref.shape[1]
    n_blocks = (tp - LANE) // blk
    half = LANE // 2

    def start_qblock(q0, nq, slot):
        st_ref[0:nh, 0:nq] = jnp.full((nh, nq), NEG, F32)
        st_ref[nh:2 * nh, 0:nq] = jnp.zeros((nh, nq), F32)
        acc_ref[:, :, 0:nq] = jnp.zeros((nh, LANE, nq), F32)
        if fox:
            qe_ref = maybe_qe[0]
            lane = lax.broadcasted_iota(jnp.int32, (nq, LANE), 1)
            for hd in range(nh):
                pair, e = divmod(hd, 2)
                q = q_ref[0, pl.ds(q0, nq), pair * LANE:(pair + 1) * LANE]
                mine = (lane >= e * half) & (lane < (e + 1) * half)
                ones = ((lane >= 3 * e) & (lane < 3 * e + 3)).astype(F32).astype(BF16)
                qe_ref[slot * nh + hd, 0:nq, :] = jnp.concatenate([jnp.where(mine, q, jnp.zeros_like(q)), ones], axis=1)

    def logits_to(buf, mx, q0, nq, k0, nk, diagonal, slot):
        if diagonal:
            ok = (lax.broadcasted_iota(jnp.int32, (nk, nq), 0)
                  <= lax.broadcasted_iota(jnp.int32, (nk, nq), 1))
        for hd in range(nh):
            if fox:
                pair = hd // 2
                s = _dot_nt(k_ref[0, pl.ds(k0, nk), pair * 2 * LANE:(pair + 1) * 2 * LANE],
                            maybe_qe[0][slot * nh + hd, 0:nq, :])
            else:
                s = _dot_nt(k_ref[0, pl.ds(k0, nk), hd * LANE:(hd + 1) * LANE],
                            q_ref[0, pl.ds(q0, nq), hd * LANE:(hd + 1) * LANE])
            if diagonal:
                s = jnp.where(ok, s, NEG)
            buf[hd, 0:nk, 0:nq] = s
            mx[hd:hd + 1, 0:nq] = jnp.max(s, axis=0, keepdims=True)

    def consume(buf, mx, nq, k0, nk):
        for hd in range(nh):
            pair = hd // 2
            vt = vt_ref[0, pair * LANE:(pair + 1) * LANE, pl.ds(k0, nk)]
            m_old = st_ref[hd:hd + 1, 0:nq]
            l_old = st_ref[nh + hd:nh + hd + 1, 0:nq]
            m_new = jnp.maximum(m_old, mx[hd:hd + 1, 0:nq])
            alpha = jnp.exp2(m_old - m_new)
            p = jnp.exp2(buf[hd, 0:nk, 0:nq] - m_new)
            st_ref[hd:hd + 1, 0:nq] = m_new
            st_ref[nh + hd:nh + hd + 1, 0:nq] = alpha * l_old + jnp.sum(p, axis=0, keepdims=True)
            acc_ref[hd, :, 0:nq] = acc_ref[hd, :, 0:nq] * alpha + _dot(vt, p.astype(BF16))

    def finish_qblock(q0, nq):
        for pair in range(nh // 2):
            a, b = 2 * pair, 2 * pair + 1
            o_a = acc_ref[a, 0:half, 0:nq] / st_ref[nh + a:nh + a + 1, 0:nq]
            o_b = acc_ref[b, half:LANE, 0:nq] / st_ref[nh + b:nh + b + 1, 0:nq]
            out_ref[0, pl.ds(q0, nq), pair * LANE:(pair + 1) * LANE] = (
                jnp.concatenate([o_a, o_b], axis=0).T.astype(out_ref.dtype))

    start_qblock(0, LANE, 0)
    logits_to(bufm, mxm, 0, LANE, 0, LANE, True, 0)
    consume(bufm, mxm, LANE, 0, LANE)
    finish_qblock(0, LANE)

    n_static = min(3, n_blocks)
    for i in range(n_static):
        q0 = LANE + i * blk
        tiles = [(bufm, mxm, 0, LANE, False)]
        for n in range(i):
            tiles.append(((buf0, buf1)[n % 2], (mx0, mx1)[n % 2], LANE + n * blk, blk, False))
        tiles.append(((buf0, buf1)[i % 2], (mx0, mx1)[i % 2], q0, blk, True))
        start_qblock(q0, blk, i + 1)
        logits_to(tiles[0][0], tiles[0][1], q0, blk, tiles[0][2], tiles[0][3], tiles[0][4], i + 1)
        for n, (buf, mx, k0, nk, _) in enumerate(tiles):
            if n + 1 < len(tiles):
                nbuf, nmx, nk0, nnk, ndiag = tiles[n + 1]
                logits_to(nbuf, nmx, q0, blk, nk0, nnk, ndiag, i + 1)
            consume(buf, mx, blk, k0, nk)
        finish_qblock(q0, blk)

    def qblock(i, carry):
        q0 = pl.multiple_of(LANE + i * blk, LANE)
        slot = n_static + 1

        def k_of(n):
            return pl.multiple_of(LANE + n * blk, LANE)

        def pair(m, c):
            logits_to(buf1, mx1, q0, blk, k_of(2 * m + 1), blk, False, slot)
            consume(buf0, mx0, blk, k_of(2 * m), blk)
            logits_to(buf0, mx0, q0, blk, k_of(2 * m + 2), blk, False, slot)
            consume(buf1, mx1, blk, k_of(2 * m + 1), blk)
            return c

        start_qblock(q0, blk, slot)
        logits_to(bufm, mxm, q0, blk, 0, LANE, False, slot)
        logits_to(buf0, mx0, q0, blk, k_of(0), blk, False, slot)
        consume(bufm, mxm, blk, 0, LANE)
        pair(0, 0)
        n_pairs = (i - 1) // 2
        lax.fori_loop(1, n_pairs, pair, 0)
        held = 2 * n_pairs

        @pl.when(i % 2 == 1)
        def _():
            logits_to(buf1, mx1, q0, blk, q0, blk, True, slot)
            consume(buf0, mx0, blk, k_of(held), blk)
            consume(buf1, mx1, blk, q0, blk)
            finish_qblock(q0, blk)

        @pl.when(i % 2 == 0)
        def _():
            logits_to(buf1, mx1, q0, blk, k_of(held + 1), blk, False, slot)
            consume(buf0, mx0, blk, k_of(held), blk)
            logits_to(buf0, mx0, q0, blk, q0, blk, True, slot)
            consume(buf1, mx1, blk, k_of(held + 1), blk)
            consume(buf0, mx0, blk, q0, blk)
            finish_qblock(q0, blk)

        return carry

    lax.fori_loop(n_static, n_blocks, qblock, 0)


def _attention(q, k, vt, fox, blk, nh):
    bsz, tp, _ = q.shape
    npair = vt.shape[1] // LANE
    g = nh // 2
    qw = (LANE if fox else 2 * LANE) * g
    assert 2 * nh <= SUBLANE and npair % g == 0
    scratch = [pltpu.VMEM((SUBLANE, blk), F32), pltpu.VMEM((nh, LANE, blk), F32),
               pltpu.VMEM((nh, blk, blk), F32), pltpu.VMEM((nh, blk, blk), F32),
               pltpu.VMEM((nh, LANE, blk), F32),
               pltpu.VMEM((SUBLANE, blk), F32), pltpu.VMEM((SUBLANE, blk), F32),
               pltpu.VMEM((SUBLANE, blk), F32)]
    if fox:
        scratch.append(pltpu.VMEM((5 * nh, blk, 2 * LANE), BF16))
    return pl.pallas_call(
        functools.partial(_attn_kernel, fox, blk, nh),
        out_shape=jax.ShapeDtypeStruct((bsz, tp, npair * LANE), BF16),
        grid=(bsz, npair // g),
        in_specs=[pl.BlockSpec((1, tp, qw), lambda b, p: (b, 0, p)),
                  pl.BlockSpec((1, tp, 2 * LANE * g), lambda b, p: (b, 0, p)),
                  pl.BlockSpec((1, LANE * g, tp), lambda b, p: (b, p, 0))],
        out_specs=pl.BlockSpec((1, tp, LANE * g), lambda b, p: (b, 0, p)),
        scratch_shapes=scratch,
        compiler_params=_cparams(("parallel", "parallel")),
        name="fox_attn" if fox else "mla_attn",
    )(q, k, vt)


def _place_cols(w, width, offset):
    k, n = w.shape
    return jnp.pad(w, ((0, 0), (offset, width - offset - n)))


def _ev_consts(w_in, b_if):
    nq, nv = A_HEADS * A_DQK, A_HEADS * A_DV
    o = 0
    wq = w_in[:, o:o + nq]; o += nq
    wk = w_in[:, o:o + nq]; o += nq
    wv = w_in[:, o:o + nv]; o += nv
    wo = w_in[:, o:o + nv]; o += nv
    wg = w_in[:, o:o + 2 * A_HEADS]; o += 2 * A_HEADS
    wxb = w_in[:, o:o + 1024]; o += 1024
    wgb = w_in[:, o:o + 1024]
    w = jnp.concatenate([wq, wk, wv, wo, _place_cols(wg, LANE, 0), wxb, wgb], axis=1).astype(BF16)
    bif = _place_cols(b_if[None, :], LANE, 0)
    return w, bif


def _head_tile_cols(w, n_heads, per_head, take, offset):
    k = w.shape[0]
    wh = w.reshape(k, n_heads, per_head)[:, :, take[0]:take[0] + take[1]]
    wh = jnp.pad(wh, ((0, 0), (0, 0), (offset, LANE - offset - take[1])))
    return wh.reshape(k, n_heads * LANE)


def _row128(parts):
    r = jnp.zeros((LANE,), F32)
    for off, v in parts:
        r = r.at[off:off + v.shape[0]].set(v.astype(F32))
    return r[None, :]


def _od_consts(p):
    c = {}
    w_in = p["w_in"]
    o = 0
    wcq = w_in[:, o:o + C_Q_LORA]; o += C_Q_LORA
    wckv = w_in[:, o:o + C_KV_LORA]; o += C_KV_LORA
    wkr = w_in[:, o:o + C_ROPE]; o += C_ROPE
    nf = D_HEADS * D_HD
    wfq = w_in[:, o:o + nf]; o += nf
    wfk = w_in[:, o:o + nf]; o += nf
    wfv = w_in[:, o:o + nf]; o += nf
    wff = w_in[:, o:o + D_HEADS]
    c["w"] = jnp.concatenate([wcq, wckv, _place_cols(wkr, LANE, ROPE_LO), wfq, wfk, wfv,
                              _place_cols(wff, LANE, 0)], axis=1).astype(BF16)
    c["ln"] = p["ln"][None, :]
    freqs = ROPE_THETA ** (-jnp.arange(HALF, dtype=F32) / HALF)
    n_grp = LANE // C_ROPE
    c["frq"] = jnp.concatenate([_row128([(C_ROPE * g, freqs), (C_ROPE * g + HALF, freqs)])
                                for g in range(n_grp)], axis=0)
    c["phs"] = _row128([(C_ROPE * g + HALF, jnp.full((HALF,), -0.5 * math.pi, F32)) for g in range(n_grp)])
    c["gqa"] = p["g_qa"][None, :]
    c["gkva"] = p["g_kva"][None, :]
    dq = C_NOPE + C_ROPE
    wuq, wukv = p["w_uq"], p["w_ukv"]
    c["wq"] = (_head_tile_cols(wuq, C_HEADS, dq, (0, dq), 0)).astype(BF16)
    c["wkn"] = _head_tile_cols(wukv, C_HEADS, C_NOPE + C_V, (0, C_NOPE), 0).astype(BF16)
    c["wv"] = wukv.reshape(-1, C_HEADS, C_NOPE + C_V)[:, :, C_NOPE:].reshape(-1, C_HEADS * C_V).astype(BF16)
    c["gq"] = _row128([(0, p["g_qn"]), (ROPE_LO, p["g_qr"])])
    c["gk"] = _row128([(0, p["g_kn"])])
    c["gkr"] = _row128([(ROPE_LO, p["g_kr"])])
    c["gfq"] = _row128([(0, p["g_fq"]), (D_HD, p["g_fq"])])
    c["gfk"] = _row128([(0, p["g_fk"]), (D_HD, p["g_fk"])])
    c["bf"] = _row128([(0, p["b_f"])])
    seg = np.where(np.arange(LANE) < C_NOPE, 0, np.where(np.arange(LANE) < ROPE_HI, 1, 2))
    inv_len = np.where(seg == 0, 1.0 / C_NOPE, 1.0 / C_ROPE)[None, :]
    c["gm"] = jnp.asarray((seg[:, None] == seg[None, :]) * inv_len, F32).astype(BF16)
    seg64 = np.arange(LANE) // D_HD
    c["g64"] = jnp.asarray((seg64[:, None] == seg64[None, :]) * (1.0 / D_HD), F32).astype(BF16)
    sel = np.zeros((3, LANE, (D_HEADS // 2) * LANE), np.float32)
    for h in range(D_HEADS):
        for part in range(3):
            sel[part, h, (h // 2) * LANE + 3 * (h % 2) + part] = 1.0
    c["sel"] = jnp.asarray(sel).astype(BF16)
    for nm in ("gq", "gk", "gkr", "gfq", "gfk"):
        c[nm] = jnp.tile(c[nm], (1, 2))
    for nm in ("gm", "g64"):
        z = jnp.zeros_like(c[nm])
        c[nm] = jnp.block([[c[nm], z], [z, c[nm]]])
    return c


def _time_tile(tp):
    return 3 * LANE if tp % (3 * LANE) == 0 else LANE


def kernel(x, positions, meta, ev_ln, ev_w_in, ev_b_if, ev_a_norm, ev_conv_w, ev_conv_b, ev_w_ra, ev_b_ra, ev_w_rx, ev_b_rx, ev_lam, ev_w_out, od_ln, od_w_in, od_b_f, od_g_qa, od_g_kva, od_w_uq, od_w_ukv, od_g_qn, od_g_qr, od_g_kn, od_g_kr, od_g_fq, od_g_fk, od_w_out, mlp_ln, w_ff1, w_ff2):
    bsz, seq, d = x.shape
    tp = LANE + seq
    r = bsz * tp
    tm = 512 if seq % 512 == 0 else LANE
    tt = _time_tile(tp)

    x2 = x.reshape(bsz * seq, d)
    mb = jnp.concatenate([jnp.zeros((PAD, d), x.dtype), meta.astype(x.dtype)], axis=0)
    pos_full = jnp.concatenate([jnp.zeros((bsz, PAD), jnp.int32),
                                jnp.broadcast_to(jnp.arange(N_META, dtype=jnp.int32)[None, :], (bsz, N_META)),
                                positions + N_META], axis=1)
    pos_col = pos_full.astype(F32)[:, :, None]

    e = 0
    w_a, bif = _ev_consts(ev_w_in[e], ev_b_if[e])
    q, k, v, o, g, xb, gb = _ev_in(x2, mb, ev_ln[e][None, :], w_a, bsz, tp, tt)
    ha = _mlstm(q, k, v, o, g, bif, ev_a_norm[e][None, :], bsz, tp, tt)
    hb = _lru(xb, gb, ev_conv_w[e], ev_conv_b[e][None, :],
              (0.5 * ev_w_ra[e]).astype(BF16), 0.5 * ev_b_ra[e][None, :],
              (0.5 * ev_w_rx[e]).astype(BF16), 0.5 * ev_b_rx[e][None, :], ev_lam[e][None, :], bsz, tp, tt)
    nv = A_HEADS * A_DV
    wo = ev_w_out[e].astype(BF16)
    h2 = _out_mlp((x2, mb), [ha, hb], [wo[:nv], wo[nv:]], mlp_ln[0][None, :],
                  w_ff1[0].astype(BF16), w_ff2[0].astype(BF16), tt, ("input", bsz, tp, seq))

    oi = 0
    consts = _od_consts(dict(ln=od_ln[oi], w_in=od_w_in[oi], b_f=od_b_f[oi], g_qa=od_g_qa[oi],
                             g_kva=od_g_kva[oi], w_uq=od_w_uq[oi], w_ukv=od_w_ukv[oi],
                             g_qn=od_g_qn[oi], g_qr=od_g_qr[oi], g_kn=od_g_kn[oi], g_kr=od_g_kr[oi],
                             g_fq=od_g_fq[oi], g_fk=od_g_fk[oi]))
    qm, km, vmt, qf, kf, vft = _od_in(h2.reshape(bsz, tp, d), pos_col, consts, tt)
    blk = 512 if seq % 512 == 0 else LANE
    hc = _attention(qm, km, vmt, False, blk, 4)
    hd = _attention(qf, kf, vft, True, blk, 4)
    nc = C_HEADS * C_V
    wo = od_w_out[oi].astype(BF16)
    out = _out_mlp(h2, [hc.reshape(r, nc), hd.reshape(r, nc)], [wo[:nc], wo[nc:]], mlp_ln[1][None, :],
                   w_ff1[1].astype(BF16), w_ff2[1].astype(BF16), tm, ("tokens", bsz, tp, seq))
    return out.reshape(bsz, seq, d)
```

```python
import functools
import math

import numpy as np
import jax
import jax.numpy as jnp
from jax import lax
from jax.experimental import pallas as pl
from jax.experimental.pallas import tpu as pltpu

F32 = jnp.float32
BF16 = jnp.bfloat16

N_META = 16
EPS = 1e-6
A_HEADS = 4
A_DQK = 128
A_DV = 256
B_BLOCKS = 8
B_BLOCK = 128
CONV_W = 4
LRU_C = 8.0
C_HEADS = 8
C_Q_LORA = 384
C_KV_LORA = 256
C_NOPE = 64
C_ROPE = 32
C_V = 64
ROPE_THETA = 10000.0
D_HEADS = 8
D_HD = 64

LANE = 128
SUBLANE = 8
VMEM_LIMIT = 56 * 1024 * 1024

PAD = LANE - N_META
NEG = -1e30
LOG2E = math.log2(math.e)

HALF = C_ROPE // 2
ROPE_LO = C_NOPE
ROPE_MID = C_NOPE + HALF
ROPE_HI = C_NOPE + C_ROPE
MASK_LANE = ROPE_HI


def _cparams(sem):
    return pltpu.CompilerParams(dimension_semantics=sem, vmem_limit_bytes=VMEM_LIMIT)


def _dot(a, b):
    return jnp.dot(a, b, preferred_element_type=F32)


def _dot_nt(a, b):
    return lax.dot_general(a, b, (((1,), (1,)), ((), ())), preferred_element_type=F32)


def _split2(x):
    hi = x.astype(BF16)
    lo = (x - hi.astype(F32)).astype(BF16)
    return hi, lo


def _split3(x):
    hi = x.astype(BF16)
    r = x - hi.astype(F32)
    mid = r.astype(BF16)
    lo = (r - mid.astype(F32)).astype(BF16)
    return hi, mid, lo


def _dot_sel_r(x, sel, parts):
    pieces = _split3(x) if parts == 3 else _split2(x)
    acc = _dot(pieces[0], sel)
    for p in pieces[1:]:
        acc = acc + _dot(p, sel)
    return acc


def _dot_sel_l(sel, x, parts):
    pieces = _split3(x) if parts == 3 else _split2(x)
    acc = _dot(sel, pieces[0])
    for p in pieces[1:]:
        acc = acc + _dot(sel, p)
    return acc


def _rms_rows(x, g):
    ms = jnp.mean(x * x, axis=-1, keepdims=True)
    return x * lax.rsqrt(ms + EPS) * g


def _log_sigmoid(x):
    return -(jnp.maximum(-x, 0.0) + jnp.log1p(jnp.exp(-jnp.abs(x))))


def _sigmoid(x):
    return 0.5 * jnp.tanh(0.5 * x) + 0.5


def _tril_bf16(n):
    r = lax.broadcasted_iota(jnp.int32, (n, n), 0)
    c = lax.broadcasted_iota(jnp.int32, (n, n), 1)
    return jnp.where(r >= c, 1.0, 0.0).astype(BF16)


EV_SLABS = (("q", 512, BF16), ("k", 512, BF16), ("v", 1024, BF16), ("o", 1024, BF16),
            ("g", 128, F32), ("xb", 1024, BF16), ("gb", 1024, BF16))


def _input_tile(x_ref, mb_ref):
    xt = x_ref[...]
    first = jnp.concatenate([mb_ref[...], xt[:xt.shape[0] - LANE]], axis=0)
    return jnp.where(pl.program_id(1) == 0, first, xt)


def _input_specs(seq, tt, d):
    x_spec = pl.BlockSpec(
        (pl.Element(tt), pl.Element(d)),
        lambda b, t: (pl.multiple_of(b * seq + jnp.maximum(t * tt - LANE, 0), LANE), 0))
    return [x_spec, pl.BlockSpec((LANE, d), lambda b, t: (0, 0))]


def _ev_in_kernel(x_ref, mb_ref, ln_ref, w_ref, cw_ref, cb_ref, *refs):
    out_refs, hist_ref = refs[:-1], refs[-1]
    tt = x_ref.shape[0]

    @pl.when(pl.program_id(1) == 0)
    def _():
        hist_ref[...] = jnp.zeros_like(hist_ref)

    xn = _rms_rows(_input_tile(x_ref, mb_ref), ln_ref[...]).astype(BF16)
    offsets = np.cumsum([0] + [width for _, width, _ in EV_SLABS])
    order = sorted(range(len(EV_SLABS)), key=lambda s: {"xb": 0, "gb": 1, "o": 2}.get(EV_SLABS[s][0], 3))
    for s in order:
        name, width, dt = EV_SLABS[s]
        ref = out_refs[s]
        z = _dot(xn, w_ref[:, int(offsets[s]):int(offsets[s]) + width])
        if name == "q":
            z = z * (A_DQK ** -0.5)
        elif name == "o":
            z = _sigmoid(z)
        elif name == "gb":
            z = _gelu_tanh(z)
        elif name == "xb":
            xfull = jnp.concatenate([hist_ref[...], z], axis=0)
            hist_ref[...] = z[tt - SUBLANE:, :]
            conv = cb_ref[...] + cw_ref[CONV_W - 1:CONV_W, :] * z
            for d in range(1, CONV_W):
                conv = conv + cw_ref[CONV_W - 1 - d:CONV_W - d, :] * pltpu.roll(xfull, d, 0)[SUBLANE:, :]
            z = conv
        ref[...] = z.astype(dt)


def _ev_in(x2, mb, ln, w, cw, cb, bsz, tp, tt):
    d = x2.shape[1]
    n = w.shape[1]
    nt = tp // tt
    const = lambda b, t: (0, 0)
    outs = [jax.ShapeDtypeStruct((bsz * tp, width), dt) for (_, width, dt) in EV_SLABS]
    return pl.pallas_call(
        _ev_in_kernel,
        out_shape=outs,
        grid=(bsz, nt),
        in_specs=_input_specs(x2.shape[0] // bsz, tt, d) + [
            pl.BlockSpec((1, d), const), pl.BlockSpec((d, n), const),
            pl.BlockSpec(cw.shape, const), pl.BlockSpec(cb.shape, const)],
        out_specs=[pl.BlockSpec((tt, width), lambda b, t: (b * nt + t, 0)) for (_, width, _) in EV_SLABS],
        scratch_shapes=[pltpu.VMEM((SUBLANE, cw.shape[1]), F32)],
        compiler_params=_cparams(("parallel", "arbitrary")),
        name="ev_in",
    )(x2, mb, ln, w, cw, cb)


def _mlstm_kernel(q_ref, k_ref, v_ref, o_ref, g_ref, bif_ref, an_ref, out_ref, cn_ref, m_ref):
    c = pl.program_id(1)
    L = q_ref.shape[0]
    dv_ext = A_DV + LANE

    @pl.when(c == 0)
    def _():
        cn_ref[...] = jnp.zeros_like(cn_ref)
        m_ref[...] = jnp.zeros_like(m_ref)

    lane = lax.broadcasted_iota(jnp.int32, (L, LANE), 1)
    row = lax.broadcasted_iota(jnp.int32, (L, LANE), 0)
    valid = (row >= PAD) | (c > 0)
    g = g_ref[...] + bif_ref[...]
    li = jnp.where(valid, g, NEG)
    lf = jnp.where(valid, _log_sigmoid(g), 0.0)
    is_f = (lane >= A_HEADS) & (lane < 2 * A_HEADS)
    lf = jnp.where(is_f, lf, 0.0)
    tril = _tril_bf16(L)
    bcum = _dot_sel_l(tril, lf, 2)
    li_t = li.T
    bcum_t = bcum.T

    rr = lax.broadcasted_iota(jnp.int32, (L, L), 0)
    cc = lax.broadcasted_iota(jnp.int32, (L, L), 1)
    causal = rr >= cc
    e0 = jnp.where(lane == 0, 1.0, 0.0).astype(BF16)

    heads = []
    for h in range(A_HEADS):
        qs = q_ref[:, h * A_DQK:(h + 1) * A_DQK]
        kk = k_ref[:, h * A_DQK:(h + 1) * A_DQK]
        vext = jnp.concatenate([v_ref[:, h * A_DV:(h + 1) * A_DV], e0], axis=1)
        i_col = li[:, h:h + 1]
        b_col = bcum[:, A_HEADS + h:A_HEADS + h + 1]
        i_row = li_t[h:h + 1, :]
        b_row = bcum_t[A_HEADS + h:A_HEADS + h + 1, :]
        m_state = m_ref[:, h:h + 1]
        cn = cn_ref[h]
        qk = _dot_nt(qs, kk)
        qc = _dot(qs, cn.astype(BF16))

        gsum = b_col[L - 1:L, :]
        dk_col = gsum - b_col + i_col
        dk_row = gsum - b_row + i_row
        m_new = jnp.maximum(gsum + m_state, jnp.max(dk_row, axis=1, keepdims=True))
        wk = jnp.exp(dk_col - m_new)
        sc = jnp.exp(gsum + m_state - m_new)
        kw_t = (kk.astype(F32) * wk).T.astype(BF16)
        cn_ref[h] = sc * cn + _dot(kw_t, vext)
        m_ref[:, h:h + 1] = m_new
        heads.append((qk, qc, vext, b_col, b_row, i_row, m_state))

    weighted = []
    for qk, qc, vext, b_col, b_row, i_row, m_state in heads:
        dmat = jnp.where(causal, b_col - b_row + i_row, NEG)
        inter = b_col + m_state
        m_row = jnp.maximum(inter, jnp.max(dmat, axis=1, keepdims=True))
        w_intra = jnp.exp(dmat - m_row)
        w_inter = jnp.exp(inter - m_row)
        s = (qk * w_intra).astype(BF16)
        weighted.append((w_inter * qc + _dot(s, vext), m_row))

    for h, (numden, m_row) in enumerate(weighted):
        num = numden[:, :A_DV]
        den = numden[:, A_DV:A_DV + 1]
        hh = num / jnp.maximum(jnp.abs(den), jnp.exp(-m_row))
        hn = _rms_rows(hh, an_ref[...])
        og = o_ref[:, h * A_DV:(h + 1) * A_DV].astype(F32)
        out_ref[:, h * A_DV:(h + 1) * A_DV] = (hn * og).astype(out_ref.dtype)


def _mlstm(q, k, v, o, g, bif, an, bsz, tp, L):
    nq, nv = A_HEADS * A_DQK, A_HEADS * A_DV
    nc = tp // L
    row_map = lambda b, c: (b * nc + c, 0)
    const = lambda b, c: (0, 0)
    return pl.pallas_call(
        _mlstm_kernel,
        out_shape=jax.ShapeDtypeStruct((bsz * tp, nv), BF16),
        grid=(bsz, nc),
        in_specs=[pl.BlockSpec((L, nq), row_map), pl.BlockSpec((L, nq), row_map),
                  pl.BlockSpec((L, nv), row_map), pl.BlockSpec((L, nv), row_map),
                  pl.BlockSpec((L, LANE), row_map),
                  pl.BlockSpec((1, LANE), const), pl.BlockSpec((1, A_DV), const)],
        out_specs=pl.BlockSpec((L, nv), row_map),
        scratch_shapes=[pltpu.VMEM((A_HEADS, A_DQK, A_DV + LANE), F32),
                        pltpu.VMEM((1, LANE), F32)],
        compiler_params=_cparams(("parallel", "arbitrary")),
        name="mlstm",
    )(q, k, v, o, g, bif, an)


def _gelu_tanh(x):
    c0 = math.sqrt(2.0 / math.pi)
    return x * (0.5 * jnp.tanh(x * (c0 + (c0 * 0.044715) * (x * x))) + 0.5)


def _lru_kernel(xc_ref, gate_ref, wra_ref, bra_ref, wrx_ref, brx_ref, lam_ref,
                out_ref, hc_ref, a_s, u_s):
    t = pl.program_id(1)
    tt, w = xc_ref.shape

    @pl.when(t == 0)
    def _():
        hc_ref[...] = jnp.zeros_like(hc_ref)

    row = lax.broadcasted_iota(jnp.int32, (tt, w), 0)
    valid = (row >= PAD) | (t > 0)
    xcb = xc_ref[...]
    conv = xcb.astype(F32)
    r_parts, i_parts = [], []
    for blk in range(B_BLOCKS):
        xs = xcb[:, blk * B_BLOCK:(blk + 1) * B_BLOCK]
        r_parts.append(_dot(xs, wra_ref[blk]))
        i_parts.append(_dot(xs, wrx_ref[blk]))
    r = 0.5 * jnp.tanh(jnp.concatenate(r_parts, axis=1) + bra_ref[...]) + 0.5
    ig = 0.5 * jnp.tanh(jnp.concatenate(i_parts, axis=1) + brx_ref[...]) + 0.5
    nlam = -lam_ref[...]
    softplus = jnp.maximum(nlam, 0.0) + jnp.log1p(jnp.exp(-jnp.abs(nlam)))
    a = jnp.exp2(((-LRU_C * LOG2E) * softplus) * r)
    y = 1.0 - a * a
    u = jnp.where(y > 0.0, y * lax.rsqrt(y), 0.0) * (ig * conv)
    a_s[...] = a
    u_s[...] = jnp.where(valid, u, 0.0)

    srow = lax.broadcasted_iota(jnp.int32, (SUBLANE, w), 0)

    def group(gi, hprev):
        sl = pl.ds(pl.multiple_of(gi * SUBLANE, SUBLANE), SUBLANE)
        ag = a_s[sl, :]
        ug = u_s[sl, :]
        for d in (1, 2, 4):
            ok = srow >= d
            a_sh = jnp.where(ok, pltpu.roll(ag, d, 0), 1.0)
            u_sh = jnp.where(ok, pltpu.roll(ug, d, 0), 0.0)
            ug = ag * u_sh + ug
            ag = ag * a_sh
        hg = ug + ag * hprev
        u_s[sl, :] = hg
        return jnp.broadcast_to(hg[SUBLANE - 1:SUBLANE, :], (SUBLANE, w))

    hc_ref[...] = lax.fori_loop(0, tt // SUBLANE, group, hc_ref[...], unroll=4)
    out_ref[...] = (u_s[...] * gate_ref[...].astype(F32)).astype(out_ref.dtype)


def _lru(xc, gate, wra, bra, wrx, brx, lam, bsz, tp, tt):
    w = xc.shape[1]
    nt = tp // tt
    row_map = lambda b, t: (b * nt + t, 0)
    const2 = lambda b, t: (0, 0)
    const3 = lambda b, t: (0, 0, 0)
    return pl.pallas_call(
        _lru_kernel,
        out_shape=jax.ShapeDtypeStruct(xc.shape, BF16),
        grid=(bsz, nt),
        in_specs=[pl.BlockSpec((tt, w), row_map), pl.BlockSpec((tt, w), row_map),
                  pl.BlockSpec((B_BLOCKS, B_BLOCK, B_BLOCK), const3), pl.BlockSpec((1, w), const2),
                  pl.BlockSpec((B_BLOCKS, B_BLOCK, B_BLOCK), const3), pl.BlockSpec((1, w), const2),
                  pl.BlockSpec((1, w), const2)],
        out_specs=pl.BlockSpec((tt, w), row_map),
        scratch_shapes=[pltpu.VMEM((SUBLANE, w), F32),
                        pltpu.VMEM((tt, w), F32), pltpu.VMEM((tt, w), F32)],
        compiler_params=_cparams(("parallel", "arbitrary")),
        name="rglru",
    )(xc, gate, wra, bra, wrx, brx, lam)


def _out_mlp_kernel(n_mix, from_input, *refs):
    n_h = 2 if from_input else 1
    h_refs, refs = refs[:n_h], refs[n_h:]
    mix_refs = refs[:n_mix]
    ln_ref = refs[n_mix]
    n_w = n_mix + 2
    w_hbm = refs[n_mix + 1:n_mix + 1 + n_w]
    out_ref = refs[n_mix + 1 + n_w]
    w_vmem = refs[n_mix + 2 + n_w:n_mix + 2 + 2 * n_w]
    sem = refs[n_mix + 2 + 2 * n_w]

    @pl.when((pl.program_id(0) == 0) & (pl.program_id(1) == 0))
    def _():
        copies = [pltpu.make_async_copy(src, dst, sem.at[i])
                  for i, (src, dst) in enumerate(zip(w_hbm, w_vmem))]
        for cp in copies:
            cp.start()
        for cp in copies:
            cp.wait()

    wo_refs, w1_ref, w2_ref = w_vmem[:n_mix], w_vmem[n_mix], w_vmem[n_mix + 1]
    res = _input_tile(*h_refs) if from_input else h_refs[0][...]
    for m_ref, w_ref in zip(mix_refs, wo_refs):
        res = res + _dot(m_ref[...], w_ref[...])
    hn = _rms_rows(res, ln_ref[...]).astype(BF16)
    a = jnp.maximum(_dot(hn, w1_ref[...]), 0.0)
    out_ref[...] = res + _dot((a * a).astype(BF16), w2_ref[...])


def _out_mlp(h_in, mixes, wouts, ln, w1, w2, tm, rows):
    d = w2.shape[1]
    n_mix = len(mixes)
    kind = rows[0]
    if kind == "all":
        r = rows[1]
        grid, out_rows = (r // tm, 1), r
        mix_spec = lambda n: pl.BlockSpec((tm, n), lambda i, j: (i, 0))
        h_specs, h_args = [mix_spec(d)], [h_in]
        out_spec = pl.BlockSpec((tm, d), lambda i, j: (i, 0))
    elif kind == "tokens":
        _, bsz, tp, seq = rows
        per_seq = seq // tm
        grid, out_rows = (bsz, per_seq), bsz * seq
        mix_spec = lambda n: pl.BlockSpec(
            (pl.Element(tm), pl.Element(n)),
            lambda b, i: (pl.multiple_of(b * tp + (tp - seq) + i * tm, LANE), 0))
        h_specs, h_args = [mix_spec(d)], [h_in]
        out_spec = pl.BlockSpec((tm, d), lambda b, i: (b * per_seq + i, 0))
    else:
        _, bsz, tp, seq = rows
        nt = tp // tm
        grid, out_rows = (bsz, nt), bsz * tp
        mix_spec = lambda n: pl.BlockSpec((tm, n), lambda b, t: (b * nt + t, 0))
        h_specs, h_args = _input_specs(seq, tm, d), list(h_in)
        out_spec = mix_spec(d)
    weights = [*wouts, w1, w2]
    in_specs = h_specs + [mix_spec(m.shape[1]) for m in mixes]
    in_specs += [pl.BlockSpec((1, d), lambda i, j: (0, 0))]
    in_specs += [pl.BlockSpec(memory_space=pl.ANY) for _ in weights]
    scratch = [pltpu.VMEM(w.shape, w.dtype) for w in weights]
    scratch.append(pltpu.SemaphoreType.DMA((len(weights),)))
    return pl.pallas_call(
        functools.partial(_out_mlp_kernel, n_mix, kind == "input"),
        out_shape=jax.ShapeDtypeStruct((out_rows, d), F32),
        grid=grid,
        in_specs=in_specs,
        out_specs=out_spec,
        scratch_shapes=scratch,
        compiler_params=_cparams(("arbitrary", "arbitrary")),
        name="out_mlp",
    )(*h_args, *mixes, ln, *weights)


OD_COLS = dict(cq=(0, 384), ckv=(384, 640), kr=(640, 768), fq=(768, 1280), fk=(1280, 1792),
               fv=(1792, 2304), ff=(2304, 2432))
OD_N = 2432


def _od_in_kernel(h_ref, ln_ref, w_ref, pos_ref, frq_ref, phs_ref, gqa_ref, gkva_ref,
                  wq_ref, wkn_ref, wv_ref, gq_ref, gk_ref, gkr_ref, gfq_ref, gfk_ref, bf_ref,
                  gm_ref, g64_ref, sel_ref,
                  qm_ref, km_ref, vmt_ref, qf_ref, kf_ref, vft_ref, fc_ref):
    t = pl.program_id(1)
    tm = h_ref.shape[1]

    @pl.when(t == 0)
    def _():
        fc_ref[...] = jnp.zeros_like(fc_ref)

    xn = _rms_rows(h_ref[0], ln_ref[...]).astype(BF16)

    def proj(name):
        a, b = OD_COLS[name]
        return _dot(xn, w_ref[:, a:b])

    z_cq, z_ckv, z_kr = proj("cq"), proj("ckv"), proj("kr")

    wide = 2 * LANE
    lane2 = lax.broadcasted_iota(jnp.int32, (tm, wide), 1) % LANE
    row2 = lax.broadcasted_iota(jnp.int32, (tm, wide), 0)
    unused_row2 = (row2 < PAD) & (t == 0)
    mask_lane = lane2 == MASK_LANE
    n_grp = LANE // C_ROPE
    tq = tm // n_grp
    pos = pos_ref[0]
    angc = phs_ref[...]
    for g in range(n_grp):
        angc = angc + pos[g * tq:(g + 1) * tq, :] * frq_ref[g:g + 1, :]
    cc = jnp.cos(angc)
    lane1 = lax.broadcasted_iota(jnp.int32, (tq, LANE), 1)
    in_lo = (lane1 >= ROPE_LO) & (lane1 < ROPE_MID)
    in_hi = (lane1 >= ROPE_MID) & (lane1 < ROPE_HI)
    cs_parts, sn_parts = [], []
    for g in range(n_grp):
        shift = (ROPE_LO - C_ROPE * g) % LANE
        c = pltpu.roll(cc, shift, 1) if shift else cc
        cs_parts.append(jnp.where(in_hi, pltpu.roll(c, HALF, 1), jnp.where(in_lo, c, 1.0)))
        sn_parts.append(jnp.where(in_lo, -pltpu.roll(c, LANE - HALF, 1), jnp.where(in_hi, c, 0.0)))
    cs = jnp.concatenate(cs_parts, axis=0)
    sn = jnp.concatenate(sn_parts, axis=0)
    cs = jnp.concatenate([cs, cs], axis=1)
    sn = jnp.concatenate([sn, sn], axis=1)
    first_half = lane2 < ROPE_MID

    def seg_rms(x, sel, gain):
        ms = _dot((x * x).astype(BF16), sel)
        return x * lax.rsqrt(ms + EPS) * gain

    def rope(y):
        sw = jnp.where(first_half, pltpu.roll(y, wide - HALF, 1), pltpu.roll(y, HALF, 1))
        return y * cs + sw * sn

    gm = gm_ref[...]
    g64 = g64_ref[...]

    cqn = _rms_rows(z_cq, gqa_ref[...]).astype(BF16)
    ckvn = _rms_rows(z_ckv, gkva_ref[...]).astype(BF16)
    qall = _dot(cqn, wq_ref[...])
    knall = _dot(ckvn, wkn_ref[...])
    scale_m = (C_NOPE + C_ROPE) ** -0.5 * LOG2E
    for h2 in range(C_HEADS // 2):
        cols = slice(h2 * wide, (h2 + 1) * wide)
        y = rope(seg_rms(qall[:, cols], gm,gq_ref[...]))
        qm_ref[0, :, cols] = jnp.where(mask_lane, 1.0, y * scale_m).astype(BF16)
    krt = rope(seg_rms(jnp.concatenate([z_kr, z_kr], axis=1), gm,gkr_ref[...]))
    krt = jnp.where(mask_lane & unused_row2, NEG, krt)
    for h2 in range(C_HEADS // 2):
        cols = slice(h2 * wide, (h2 + 1) * wide)
        y = seg_rms(knall[:, cols], gm,gk_ref[...]) + krt
        km_ref[0, :, cols] = y.astype(BF16)

    fq = proj("fq")
    fk = proj("fk")
    scale_f = D_HD ** -0.5 * LOG2E
    npair = D_HEADS // 2
    for p2 in range(npair // 2):
        cols = slice(p2 * wide, (p2 + 1) * wide)
        y = seg_rms(fq[:, cols], g64,gfq_ref[...])
        qf_ref[0, :, cols] = (y * scale_f).astype(BF16)
        y = seg_rms(fk[:, cols], g64,gfk_ref[...]).astype(BF16)
        for e in range(2):
            p = 2 * p2 + e
            kf_ref[0, :, 2 * p * LANE:(2 * p + 1) * LANE] = y[:, e * LANE:(e + 1) * LANE]

    lane = lax.broadcasted_iota(jnp.int32, (tm, LANE), 1)
    unused_row = (lax.broadcasted_iota(jnp.int32, (tm, LANE), 0) < PAD) & (t == 0)
    valid = jnp.logical_not(unused_row) & (lane < D_HEADS)
    lf = jnp.where(valid, _log_sigmoid(proj("ff") + bf_ref[...]), 0.0)
    fc = _dot_sel_l(_tril_bf16(tm), lf, 2) + fc_ref[...]
    fc_ref[...] = fc[tm - 1:tm, :]
    hi, mid, lo = _split3(jnp.where(unused_row, NEG, fc * (-LOG2E)))
    bias = _dot(hi, sel_ref[0]) + _dot(mid, sel_ref[1]) + _dot(lo, sel_ref[2])
    for p in range(npair):
        kf_ref[0, :, (2 * p + 1) * LANE:(2 * p + 2) * LANE] = bias[:, p * LANE:(p + 1) * LANE].astype(BF16)

    vft_ref[0] = proj("fv").T.astype(BF16)
    vmt_ref[0] = _dot(ckvn, wv_ref[...]).T.astype(BF16)


def _od_in(h3, pos_col, consts, tm):
    bsz, tp, d = h3.shape
    nt = tp // tm
    names = ["ln", "w", None, "frq", "phs", "gqa", "gkva", "wq", "wkn", "wv", "gq", "gk", "gkr",
             "gfq", "gfk", "bf", "gm", "g64", "sel"]
    args, in_specs = [h3], [pl.BlockSpec((1, tm, d), lambda b, t: (b, t, 0))]
    for nm in names:
        if nm is None:
            args.append(pos_col)
            in_specs.append(pl.BlockSpec((1, tm, 1), lambda b, t: (b, t, 0)))
            continue
        a = consts[nm]
        args.append(a)
        in_specs.append(pl.BlockSpec(a.shape, (lambda nd: (lambda b, t: (0,) * nd))(a.ndim)))
    hl = C_HEADS * LANE
    pk = D_HEADS * D_HD
    outs = [jax.ShapeDtypeStruct((bsz, tp, hl), BF16), jax.ShapeDtypeStruct((bsz, tp, hl), BF16),
            jax.ShapeDtypeStruct((bsz, pk, tp), BF16), jax.ShapeDtypeStruct((bsz, tp, pk), BF16),
            jax.ShapeDtypeStruct((bsz, tp, 2 * pk), BF16), jax.ShapeDtypeStruct((bsz, pk, tp), BF16)]
    rowspec = lambda n: pl.BlockSpec((1, tm, n), lambda b, t: (b, t, 0))
    tspec = pl.BlockSpec((1, pk, tm), lambda b, t: (b, 0, t))
    return pl.pallas_call(
        _od_in_kernel,
        out_shape=outs,
        grid=(bsz, nt),
        in_specs=in_specs,
        out_specs=[rowspec(hl), rowspec(hl), tspec, rowspec(pk), rowspec(2 * pk), tspec],
        scratch_shapes=[pltpu.VMEM((1, LANE), F32)],
        compiler_params=_cparams(("parallel", "arbitrary")),
        name="od_in",
    )(*args)


def _attn_kernel(fox, blk, nh, q_ref, k_ref, vt_ref, out_ref, st_ref, acc_ref, buf0, buf1, bufm,
                 mx0, mx1, mxm, *maybe_qe):
    tp = k_ref.shape[1]
    n_blocks = (tp - LANE) // blk
    half = LANE // 2

    def start_qblock(q0, nq, slot):
        st_ref[0:nh, 0:nq] = jnp.full((nh, nq), NEG, F32)
        st_ref[nh:2 * nh, 0:nq] = jnp.zeros((nh, nq), F32)
        acc_ref[:, :, 0:nq] = jnp.zeros((nh, LANE, nq), F32)
        if fox:
            qe_ref = maybe_qe[0]
            lane = lax.broadcasted_iota(jnp.int32, (nq, LANE), 1)
            for hd in range(nh):
                pair, e = divmod(hd, 2)
                q = q_ref[0, pl.ds(q0, nq), pair * LANE:(pair + 1) * LANE]
                mine = (lane >= e * half) & (lane < (e + 1) * half)
                ones = ((lane >= 3 * e) & (lane < 3 * e + 3)).astype(F32).astype(BF16)
                qe_ref[slot * nh + hd, 0:nq, :] = jnp.concatenate([jnp.where(mine, q, jnp.zeros_like(q)), ones], axis=1)

    def logits_to(buf, mx, q0, nq, k0, nk, diagonal, slot):
        if diagonal:
            ok = (lax.broadcasted_iota(jnp.int32, (nk, nq), 0)
                  <= lax.broadcasted_iota(jnp.int32, (nk, nq), 1))
        for hd in range(nh):
            if fox:
                pair = hd // 2
                s = _dot_nt(k_ref[0, pl.ds(k0, nk), pair * 2 * LANE:(pair + 1) * 2 * LANE],
                            maybe_qe[0][slot * nh + hd, 0:nq, :])
            else:
                s = _dot_nt(k_ref[0, pl.ds(k0, nk), hd * LANE:(hd + 1) * LANE],
                            q_ref[0, pl.ds(q0, nq), hd * LANE:(hd + 1) * LANE])
            if diagonal:
                s = jnp.where(ok, s, NEG)
            buf[hd, 0:nk, 0:nq] = s
            mx[hd:hd + 1, 0:nq] = jnp.max(s, axis=0, keepdims=True)

    def consume(buf, mx, nq, k0, nk):
        for hd in range(nh):
            pair = hd // 2
            vt = vt_ref[0, pair * LANE:(pair + 1) * LANE, pl.ds(k0, nk)]
            m_old = st_ref[hd:hd + 1, 0:nq]
            l_old = st_ref[nh + hd:nh + hd + 1, 0:nq]
            m_new = jnp.maximum(m_old, mx[hd:hd + 1, 0:nq])
            alpha = jnp.exp2(m_old - m_new)
            p = jnp.exp2(buf[hd, 0:nk, 0:nq] - m_new)
            st_ref[hd:hd + 1, 0:nq] = m_new
            st_ref[nh + hd:nh + hd + 1, 0:nq] = alpha * l_old + jnp.sum(p, axis=0, keepdims=True)
            acc_ref[hd, :, 0:nq] = acc_ref[hd, :, 0:nq] * alpha + _dot(vt, p.astype(BF16))

    def finish_qblock(q0, nq):
        for pair in range(nh // 2):
            a, b = 2 * pair, 2 * pair + 1
            o_a = acc_ref[a, 0:half, 0:nq] / st_ref[nh + a:nh + a + 1, 0:nq]
            o_b = acc_ref[b, half:LANE, 0:nq] / st_ref[nh + b:nh + b + 1, 0:nq]
            out_ref[0, pl.ds(q0, nq), pair * LANE:(pair + 1) * LANE] = (
                jnp.concatenate([o_a, o_b], axis=0).T.astype(out_ref.dtype))

    start_qblock(0, LANE, 0)
    logits_to(bufm, mxm, 0, LANE, 0, LANE, True, 0)
    consume(bufm, mxm, LANE, 0, LANE)
    finish_qblock(0, LANE)

    n_static = min(3, n_blocks)
    for i in range(n_static):
        q0 = LANE + i * blk
        tiles = [(bufm, mxm, 0, LANE, False)]
        for n in range(i):
            tiles.append(((buf0, buf1)[n % 2], (mx0, mx1)[n % 2], LANE + n * blk, blk, False))
        tiles.append(((buf0, buf1)[i % 2], (mx0, mx1)[i % 2], q0, blk, True))
        start_qblock(q0, blk, i + 1)
        logits_to(tiles[0][0], tiles[0][1], q0, blk, tiles[0][2], tiles[0][3], tiles[0][4], i + 1)
        for n, (buf, mx, k0, nk, _) in enumerate(tiles):
            if n + 1 < len(tiles):
                nbuf, nmx, nk0, nnk, ndiag = tiles[n + 1]
                logits_to(nbuf, nmx, q0, blk, nk0, nnk, ndiag, i + 1)
            consume(buf, mx, blk, k0, nk)
        finish_qblock(q0, blk)

    def qblock(i, carry):
        q0 = pl.multiple_of(LANE + i * blk, LANE)
        slot = n_static + 1

        def k_of(n):
            return pl.multiple_of(LANE + n * blk, LANE)

        def pair(m, c):
            logits_to(buf1, mx1, q0, blk, k_of(2 * m + 1), blk, False, slot)
            consume(buf0, mx0, blk, k_of(2 * m), blk)
            logits_to(buf0, mx0, q0, blk, k_of(2 * m + 2), blk, False, slot)
            consume(buf1, mx1, blk, k_of(2 * m + 1), blk)
            return c

        start_qblock(q0, blk, slot)
        logits_to(bufm, mxm, q0, blk, 0, LANE, False, slot)
        logits_to(buf0, mx0, q0, blk, k_of(0), blk, False, slot)
        consume(bufm, mxm, blk, 0, LANE)
        pair(0, 0)
        n_pairs = (i - 1) // 2
        lax.fori_loop(1, n_pairs, pair, 0)
        held = 2 * n_pairs

        @pl.when(i % 2 == 1)
        def _():
            logits_to(buf1, mx1, q0, blk, q0, blk, True, slot)
            consume(buf0, mx0, blk, k_of(held), blk)
            consume(buf1, mx1, blk, q0, blk)
            finish_qblock(q0, blk)

        @pl.when(i % 2 == 0)
        def _():
            logits_to(buf1, mx1, q0, blk, k_of(held + 1), blk, False, slot)
            consume(buf0, mx0, blk, k_of(held), blk)
            logits_to(buf0, mx0, q0, blk, q0, blk, True, slot)
            consume(buf1, mx1, blk, k_of(held + 1), blk)
            consume(buf0, mx0, blk, q0, blk)
            finish_qblock(q0, blk)

        return carry

    lax.fori_loop(n_static, n_blocks, qblock, 0)


def _attention(q, k, vt, fox, blk, nh):
    bsz, tp, _ = q.shape
    npair = vt.shape[1] // LANE
    g = nh // 2
    qw = (LANE if fox else 2 * LANE) * g
    assert 2 * nh <= SUBLANE and npair % g == 0
    scratch = [pltpu.VMEM((SUBLANE, blk), F32), pltpu.VMEM((nh, LANE, blk), F32),
               pltpu.VMEM((nh, blk, blk), F32), pltpu.VMEM((nh, blk, blk), F32),
               pltpu.VMEM((nh, LANE, blk), F32),
               pltpu.VMEM((SUBLANE, blk), F32), pltpu.VMEM((SUBLANE, blk), F32),
               pltpu.VMEM((SUBLANE, blk), F32)]
    if fox:
        scratch.append(pltpu.VMEM((5 * nh, blk, 2 * LANE), BF16))
    return pl.pallas_call(
        functools.partial(_attn_kernel, fox, blk, nh),
        out_shape=jax.ShapeDtypeStruct((bsz, tp, npair * LANE), BF16),
        grid=(bsz, npair // g),
        in_specs=[pl.BlockSpec((1, tp, qw), lambda b, p: (b, 0, p)),
                  pl.BlockSpec((1, tp, 2 * LANE * g), lambda b, p: (b, 0, p)),
                  pl.BlockSpec((1, LANE * g, tp), lambda b, p: (b, p, 0))],
        out_specs=pl.BlockSpec((1, tp, LANE * g), lambda b, p: (b, 0, p)),
        scratch_shapes=scratch,
        compiler_params=_cparams(("parallel", "parallel")),
        name="fox_attn" if fox else "mla_attn",
    )(q, k, vt)


def _place_cols(w, width, offset):
    k, n = w.shape
    return jnp.pad(w, ((0, 0), (offset, width - offset - n)))


def _ev_consts(w_in, b_if):
    nq, nv = A_HEADS * A_DQK, A_HEADS * A_DV
    o = 0
    wq = w_in[:, o:o + nq]; o += nq
    wk = w_in[:, o:o + nq]; o += nq
    wv = w_in[:, o:o + nv]; o += nv
    wo = w_in[:, o:o + nv]; o += nv
    wg = w_in[:, o:o + 2 * A_HEADS]; o += 2 * A_HEADS
    wxb = w_in[:, o:o + 1024]; o += 1024
    wgb = w_in[:, o:o + 1024]
    w = jnp.concatenate([wq, wk, wv, wo, _place_cols(wg, LANE, 0), wxb, wgb], axis=1).astype(BF16)
    bif = _place_cols(b_if[None, :], LANE, 0)
    return w, bif


def _head_tile_cols(w, n_heads, per_head, take, offset):
    k = w.shape[0]
    wh = w.reshape(k, n_heads, per_head)[:, :, take[0]:take[0] + take[1]]
    wh = jnp.pad(wh, ((0, 0), (0, 0), (offset, LANE - offset - take[1])))
    return wh.reshape(k, n_heads * LANE)


def _row128(parts):
    r = jnp.zeros((LANE,), F32)
    for off, v in parts:
        r = r.at[off:off + v.shape[0]].set(v.astype(F32))
    return r[None, :]


def _od_consts(p):
    c = {}
    w_in = p["w_in"]
    o = 0
    wcq = w_in[:, o:o + C_Q_LORA]; o += C_Q_LORA
    wckv = w_in[:, o:o + C_KV_LORA]; o += C_KV_LORA
    wkr = w_in[:, o:o + C_ROPE]; o += C_ROPE
    nf = D_HEADS * D_HD
    wfq = w_in[:, o:o + nf]; o += nf
    wfk = w_in[:, o:o + nf]; o += nf
    wfv = w_in[:, o:o + nf]; o += nf
    wff = w_in[:, o:o + D_HEADS]
    c["w"] = jnp.concatenate([wcq, wckv, _place_cols(wkr, LANE, ROPE_LO), wfq, wfk, wfv,
                              _place_cols(wff, LANE, 0)], axis=1).astype(BF16)
    c["ln"] = p["ln"][None, :]
    freqs = ROPE_THETA ** (-jnp.arange(HALF, dtype=F32) / HALF)
    n_grp = LANE // C_ROPE
    c["frq"] = jnp.concatenate([_row128([(C_ROPE * g, freqs), (C_ROPE * g + HALF, freqs)])
                                for g in range(n_grp)], axis=0)
    c["phs"] = _row128([(C_ROPE * g + HALF, jnp.full((HALF,), -0.5 * math.pi, F32)) for g in range(n_grp)])
    c["gqa"] = p["g_qa"][None, :]
    c["gkva"] = p["g_kva"][None, :]
    dq = C_NOPE + C_ROPE
    wuq, wukv = p["w_uq"], p["w_ukv"]
    c["wq"] = (_head_tile_cols(wuq, C_HEADS, dq, (0, dq), 0)).astype(BF16)
    c["wkn"] = _head_tile_cols(wukv, C_HEADS, C_NOPE + C_V, (0, C_NOPE), 0).astype(BF16)
    c["wv"] = wukv.reshape(-1, C_HEADS, C_NOPE + C_V)[:, :, C_NOPE:].reshape(-1, C_HEADS * C_V).astype(BF16)
    c["gq"] = _row128([(0, p["g_qn"]), (ROPE_LO, p["g_qr"])])
    c["gk"] = _row128([(0, p["g_kn"])])
    c["gkr"] = _row128([(ROPE_LO, p["g_kr"])])
    c["gfq"] = _row128([(0, p["g_fq"]), (D_HD, p["g_fq"])])
    c["gfk"] = _row128([(0, p["g_fk"]), (D_HD, p["g_fk"])])
    c["bf"] = _row128([(0, p["b_f"])])
    seg = np.where(np.arange(LANE) < C_NOPE, 0, np.where(np.arange(LANE) < ROPE_HI, 1, 2))
    inv_len = np.where(seg == 0, 1.0 / C_NOPE, 1.0 / C_ROPE)[None, :]
    c["gm"] = jnp.asarray((seg[:, None] == seg[None, :]) * inv_len, F32).astype(BF16)
    seg64 = np.arange(LANE) // D_HD
    c["g64"] = jnp.asarray((seg64[:, None] == seg64[None, :]) * (1.0 / D_HD), F32).astype(BF16)
    sel = np.zeros((3, LANE, (D_HEADS // 2) * LANE), np.float32)
    for h in range(D_HEADS):
        for part in range(3):
            sel[part, h, (h // 2) * LANE + 3 * (h % 2) + part] = 1.0
    c["sel"] = jnp.asarray(sel).astype(BF16)
    for nm in ("gq", "gk", "gkr", "gfq", "gfk"):
        c[nm] = jnp.tile(c[nm], (1, 2))
    for nm in ("gm", "g64"):
        z = jnp.zeros_like(c[nm])
        c[nm] = jnp.block([[c[nm], z], [z, c[nm]]])
    return c


def _time_tile(tp):
    return 3 * LANE if tp % (3 * LANE) == 0 else LANE


def kernel(x, positions, meta, ev_ln, ev_w_in, ev_b_if, ev_a_norm, ev_conv_w, ev_conv_b, ev_w_ra, ev_b_ra, ev_w_rx, ev_b_rx, ev_lam, ev_w_out, od_ln, od_w_in, od_b_f, od_g_qa, od_g_kva, od_w_uq, od_w_ukv, od_g_qn, od_g_qr, od_g_kn, od_g_kr, od_g_fq, od_g_fk, od_w_out, mlp_ln, w_ff1, w_ff2):
    bsz, seq, d = x.shape
    tp = LANE + seq
    r = bsz * tp
    tm = 512 if seq % 512 == 0 else LANE
    tt = _time_tile(tp)

    x2 = x.reshape(bsz * seq, d)
    mb = jnp.concatenate([jnp.zeros((PAD, d), x.dtype), meta.astype(x.dtype)], axis=0)
    pos_full = jnp.concatenate([jnp.zeros((bsz, PAD), jnp.int32),
                                jnp.broadcast_to(jnp.arange(N_META, dtype=jnp.int32)[None, :], (bsz, N_META)),
                                positions + N_META], axis=1)
    pos_col = pos_full.astype(F32)[:, :, None]

    e = 0
    w_a, bif = _ev_consts(ev_w_in[e], ev_b_if[e])
    q, k, v, o, g, xc, gate = _ev_in(x2, mb, ev_ln[e][None, :], w_a, ev_conv_w[e], ev_conv_b[e][None, :],
                                     bsz, tp, tt)
    ha = _mlstm(q, k, v, o, g, bif, ev_a_norm[e][None, :], bsz, tp, tt)
    hb = _lru(xc, gate, (0.5 * ev_w_ra[e]).astype(BF16), 0.5 * ev_b_ra[e][None, :],
              (0.5 * ev_w_rx[e]).astype(BF16), 0.5 * ev_b_rx[e][None, :], ev_lam[e][None, :], bsz, tp, tt)
    nv = A_HEADS * A_DV
    wo = ev_w_out[e].astype(BF16)
    h2 = _out_mlp((x2, mb), [ha, hb], [wo[:nv], wo[nv:]], mlp_ln[0][None, :],
                  w_ff1[0].astype(BF16), w_ff2[0].astype(BF16), tt, ("input", bsz, tp, seq))

    oi = 0
    consts = _od_consts(dict(ln=od_ln[oi], w_in=od_w_in[oi], b_f=od_b_f[oi], g_qa=od_g_qa[oi],
                             g_kva=od_g_kva[oi], w_uq=od_w_uq[oi], w_ukv=od_w_ukv[oi],
                             g_qn=od_g_qn[oi], g_qr=od_g_qr[oi], g_kn=od_g_kn[oi], g_kr=od_g_kr[oi],
                             g_fq=od_g_fq[oi], g_fk=od_g_fk[oi]))
    qm, km, vmt, qf, kf, vft = _od_in(h2.reshape(bsz, tp, d), pos_col, consts, tt)
    blk = 512 if seq % 512 == 0 else LANE
    hc = _attention(qm, km, vmt, False, blk, 4)
    hd = _attention(qf, kf, vft, True, blk, 4)
    nc = C_HEADS * C_V
    wo = od_w_out[oi].astype(BF16)
    out = _out_mlp(h2, [hc.reshape(r, nc), hd.reshape(r, nc)], [wo[:nc], wo[nc:]], mlp_ln[1][None, :],
                   w_ff1[1].astype(BF16), w_ff2[1].astype(BF16), tm, ("tokens", bsz, tp, seq))
    return out.reshape(bsz, seq, d)
```

```python
import functools
import math

import numpy as np
import jax
import jax.numpy as jnp
from jax import lax
from jax.experimental import pallas as pl
from jax.experimental.pallas import tpu as pltpu

F32 = jnp.float32
BF16 = jnp.bfloat16

N_META = 16
EPS = 1e-6
A_HEADS = 4
A_DQK = 128
A_DV = 256
B_BLOCKS = 8
B_BLOCK = 128
CONV_W = 4
LRU_C = 8.0
C_HEADS = 8
C_Q_LORA = 384
C_KV_LORA = 256
C_NOPE = 64
C_ROPE = 32
C_V = 64
ROPE_THETA = 10000.0
D_HEADS = 8
D_HD = 64

LANE = 128
SUBLANE = 8
VMEM_LIMIT = 56 * 1024 * 1024

PAD = LANE - N_META
NEG = -1e30
LOG2E = math.log2(math.e)

HALF = C_ROPE // 2
ROPE_LO = C_NOPE
ROPE_MID = C_NOPE + HALF
ROPE_HI = C_NOPE + C_ROPE
MASK_LANE = ROPE_HI


def _cparams(sem):
    return pltpu.CompilerParams(dimension_semantics=sem, vmem_limit_bytes=VMEM_LIMIT)


def _dot(a, b):
    return jnp.dot(a, b, preferred_element_type=F32)


def _dot_nt(a, b):
    return lax.dot_general(a, b, (((1,), (1,)), ((), ())), preferred_element_type=F32)


def _split2(x):
    hi = x.astype(BF16)
    lo = (x - hi.astype(F32)).astype(BF16)
    return hi, lo


def _split3(x):
    hi = x.astype(BF16)
    r = x - hi.astype(F32)
    mid = r.astype(BF16)
    lo = (r - mid.astype(F32)).astype(BF16)
    return hi, mid, lo


def _dot_sel_r(x, sel, parts):
    pieces = _split3(x) if parts == 3 else _split2(x)
    acc = _dot(pieces[0], sel)
    for p in pieces[1:]:
        acc = acc + _dot(p, sel)
    return acc


def _dot_sel_l(sel, x, parts):
    pieces = _split3(x) if parts == 3 else _split2(x)
    acc = _dot(sel, pieces[0])
    for p in pieces[1:]:
        acc = acc + _dot(sel, p)
    return acc


def _rms_rows(x, g):
    ms = jnp.mean(x * x, axis=-1, keepdims=True)
    return x * lax.rsqrt(ms + EPS) * g


def _log_sigmoid(x):
    return -(jnp.maximum(-x, 0.0) + jnp.log1p(jnp.exp(-jnp.abs(x))))


def _sigmoid(x):
    return 0.5 * jnp.tanh(0.5 * x) + 0.5


def _tril_bf16(n):
    r = lax.broadcasted_iota(jnp.int32, (n, n), 0)
    c = lax.broadcasted_iota(jnp.int32, (n, n), 1)
    return jnp.where(r >= c, 1.0, 0.0).astype(BF16)


EV_SLABS = (("q", 512, BF16), ("k", 512, BF16), ("v", 1024, BF16), ("o", 1024, BF16),
            ("g", 128, F32), ("xb", 1024, BF16), ("gb", 1024, BF16))


def _input_tile(x_ref, mb_ref):
    xt = x_ref[...]
    first = jnp.concatenate([mb_ref[...], xt[:xt.shape[0] - LANE]], axis=0)
    return jnp.where(pl.program_id(1) == 0, first, xt)


def _input_specs(seq, tt, d):
    x_spec = pl.BlockSpec(
        (pl.Element(tt), pl.Element(d)),
        lambda b, t: (pl.multiple_of(b * seq + jnp.maximum(t * tt - LANE, 0), LANE), 0))
    return [x_spec, pl.BlockSpec((LANE, d), lambda b, t: (0, 0))]


def _ev_in_kernel(x_ref, mb_ref, ln_ref, w_ref, cw_ref, cb_ref, *refs):
    out_refs, hist_ref = refs[:-1], refs[-1]
    tt = x_ref.shape[0]

    @pl.when(pl.program_id(1) == 0)
    def _():
        hist_ref[...] = jnp.zeros_like(hist_ref)

    xn = _rms_rows(_input_tile(x_ref, mb_ref), ln_ref[...]).astype(BF16)
    offsets = np.cumsum([0] + [width for _, width, _ in EV_SLABS])
    order = sorted(range(len(EV_SLABS)), key=lambda s: {"xb": 0, "q": 1, "gb": 2, "k": 3, "o": 4}.get(EV_SLABS[s][0], 5))
    for s in order:
        name, width, dt = EV_SLABS[s]
        ref = out_refs[s]
        z = _dot(xn, w_ref[:, int(offsets[s]):int(offsets[s]) + width])
        if name == "q":
            z = z * (A_DQK ** -0.5)
        elif name == "o":
            z = _sigmoid(z)
        elif name == "gb":
            z = _gelu_tanh(z)
        elif name == "xb":
            xfull = jnp.concatenate([hist_ref[...], z], axis=0)
            hist_ref[...] = z[tt - SUBLANE:, :]
            conv = cb_ref[...] + cw_ref[CONV_W - 1:CONV_W, :] * z
            for d in range(1, CONV_W):
                conv = conv + cw_ref[CONV_W - 1 - d:CONV_W - d, :] * pltpu.roll(xfull, d, 0)[SUBLANE:, :]
            z = conv
        ref[...] = z.astype(dt)


def _ev_in(x2, mb, ln, w, cw, cb, bsz, tp, tt):
    d = x2.shape[1]
    n = w.shape[1]
    nt = tp // tt
    const = lambda b, t: (0, 0)
    outs = [jax.ShapeDtypeStruct((bsz * tp, width), dt) for (_, width, dt) in EV_SLABS]
    return pl.pallas_call(
        _ev_in_kernel,
        out_shape=outs,
        grid=(bsz, nt),
        in_specs=_input_specs(x2.shape[0] // bsz, tt, d) + [
            pl.BlockSpec((1, d), const), pl.BlockSpec((d, n), const),
            pl.BlockSpec(cw.shape, const), pl.BlockSpec(cb.shape, const)],
        out_specs=[pl.BlockSpec((tt, width), lambda b, t: (b * nt + t, 0)) for (_, width, _) in EV_SLABS],
        scratch_shapes=[pltpu.VMEM((SUBLANE, cw.shape[1]), F32)],
        compiler_params=_cparams(("parallel", "arbitrary")),
        name="ev_in",
    )(x2, mb, ln, w, cw, cb)


def _mlstm_kernel(q_ref, k_ref, v_ref, o_ref, g_ref, bif_ref, an_ref, out_ref, cn_ref, m_ref):
    c = pl.program_id(1)
    L = q_ref.shape[0]
    dv_ext = A_DV + LANE

    @pl.when(c == 0)
    def _():
        cn_ref[...] = jnp.zeros_like(cn_ref)
        m_ref[...] = jnp.zeros_like(m_ref)

    lane = lax.broadcasted_iota(jnp.int32, (L, LANE), 1)
    row = lax.broadcasted_iota(jnp.int32, (L, LANE), 0)
    valid = (row >= PAD) | (c > 0)
    g = g_ref[...] + bif_ref[...]
    li = jnp.where(valid, g, NEG)
    lf = jnp.where(valid, _log_sigmoid(g), 0.0)
    is_f = (lane >= A_HEADS) & (lane < 2 * A_HEADS)
    lf = jnp.where(is_f, lf, 0.0)
    tril = _tril_bf16(L)
    bcum = _dot_sel_l(tril, lf, 2)
    li_t = li.T
    bcum_t = bcum.T

    rr = lax.broadcasted_iota(jnp.int32, (L, L), 0)
    cc = lax.broadcasted_iota(jnp.int32, (L, L), 1)
    causal = rr >= cc
    e0 = jnp.where(lane == 0, 1.0, 0.0).astype(BF16)

    heads = []
    for h in range(A_HEADS):
        qs = q_ref[:, h * A_DQK:(h + 1) * A_DQK]
        kk = k_ref[:, h * A_DQK:(h + 1) * A_DQK]
        vext = jnp.concatenate([v_ref[:, h * A_DV:(h + 1) * A_DV], e0], axis=1)
        i_col = li[:, h:h + 1]
        b_col = bcum[:, A_HEADS + h:A_HEADS + h + 1]
        i_row = li_t[h:h + 1, :]
        b_row = bcum_t[A_HEADS + h:A_HEADS + h + 1, :]
        m_state = m_ref[:, h:h + 1]
        cn = cn_ref[h]
        qk = _dot_nt(qs, kk)
        qc = _dot(qs, cn.astype(BF16))

        gsum = b_col[L - 1:L, :]
        dk_col = gsum - b_col + i_col
        dk_row = gsum - b_row + i_row
        m_new = jnp.maximum(gsum + m_state, jnp.max(dk_row, axis=1, keepdims=True))
        wk = jnp.exp(dk_col - m_new)
        sc = jnp.exp(gsum + m_state - m_new)
        kw_t = (kk.astype(F32) * wk).T.astype(BF16)
        cn_ref[h] = sc * cn + _dot(kw_t, vext)
        m_ref[:, h:h + 1] = m_new
        heads.append((qk, qc, vext, b_col, b_row, i_row, m_state))

    weighted = []
    for qk, qc, vext, b_col, b_row, i_row, m_state in heads:
        dmat = jnp.where(causal, b_col - b_row + i_row, NEG)
        inter = b_col + m_state
        m_row = jnp.maximum(inter, jnp.max(dmat, axis=1, keepdims=True))
        w_intra = jnp.exp(dmat - m_row)
        w_inter = jnp.exp(inter - m_row)
        s = (qk * w_intra).astype(BF16)
        weighted.append((w_inter * qc + _dot(s, vext), m_row))

    for h, (numden, m_row) in enumerate(weighted):
        num = numden[:, :A_DV]
        den = numden[:, A_DV:A_DV + 1]
        hh = num / jnp.maximum(jnp.abs(den), jnp.exp(-m_row))
        hn = _rms_rows(hh, an_ref[...])
        og = o_ref[:, h * A_DV:(h + 1) * A_DV].astype(F32)
        out_ref[:, h * A_DV:(h + 1) * A_DV] = (hn * og).astype(out_ref.dtype)


def _mlstm(q, k, v, o, g, bif, an, bsz, tp, L):
    nq, nv = A_HEADS * A_DQK, A_HEADS * A_DV
    nc = tp // L
    row_map = lambda b, c: (b * nc + c, 0)
    const = lambda b, c: (0, 0)
    return pl.pallas_call(
        _mlstm_kernel,
        out_shape=jax.ShapeDtypeStruct((bsz * tp, nv), BF16),
        grid=(bsz, nc),
        in_specs=[pl.BlockSpec((L, nq), row_map), pl.BlockSpec((L, nq), row_map),
                  pl.BlockSpec((L, nv), row_map), pl.BlockSpec((L, nv), row_map),
                  pl.BlockSpec((L, LANE), row_map),
                  pl.BlockSpec((1, LANE), const), pl.BlockSpec((1, A_DV), const)],
        out_specs=pl.BlockSpec((L, nv), row_map),
        scratch_shapes=[pltpu.VMEM((A_HEADS, A_DQK, A_DV + LANE), F32),
                        pltpu.VMEM((1, LANE), F32)],
        compiler_params=_cparams(("parallel", "arbitrary")),
        name="mlstm",
    )(q, k, v, o, g, bif, an)


def _gelu_tanh(x):
    c0 = math.sqrt(2.0 / math.pi)
    return x * (0.5 * jnp.tanh(x * (c0 + (c0 * 0.044715) * (x * x))) + 0.5)


def _lru_kernel(xc_ref, gate_ref, wra_ref, bra_ref, wrx_ref, brx_ref, lam_ref,
                out_ref, hc_ref, a_s, u_s):
    t = pl.program_id(1)
    tt, w = xc_ref.shape

    @pl.when(t == 0)
    def _():
        hc_ref[...] = jnp.zeros_like(hc_ref)

    row = lax.broadcasted_iota(jnp.int32, (tt, w), 0)
    valid = (row >= PAD) | (t > 0)
    xcb = xc_ref[...]
    conv = xcb.astype(F32)
    r_parts, i_parts = [], []
    for blk in range(B_BLOCKS):
        xs = xcb[:, blk * B_BLOCK:(blk + 1) * B_BLOCK]
        r_parts.append(_dot(xs, wra_ref[blk]))
        i_parts.append(_dot(xs, wrx_ref[blk]))
    r = 0.5 * jnp.tanh(jnp.concatenate(r_parts, axis=1) + bra_ref[...]) + 0.5
    ig = 0.5 * jnp.tanh(jnp.concatenate(i_parts, axis=1) + brx_ref[...]) + 0.5
    nlam = -lam_ref[...]
    softplus = jnp.maximum(nlam, 0.0) + jnp.log1p(jnp.exp(-jnp.abs(nlam)))
    a = jnp.exp2(((-LRU_C * LOG2E) * softplus) * r)
    y = 1.0 - a * a
    u = jnp.where(y > 0.0, y * lax.rsqrt(y), 0.0) * (ig * conv)
    a_s[...] = a
    u_s[...] = jnp.where(valid, u, 0.0)

    srow = lax.broadcasted_iota(jnp.int32, (SUBLANE, w), 0)

    def group(gi, hprev):
        sl = pl.ds(pl.multiple_of(gi * SUBLANE, SUBLANE), SUBLANE)
        ag = a_s[sl, :]
        ug = u_s[sl, :]
        for d in (1, 2, 4):
            ok = srow >= d
            a_sh = jnp.where(ok, pltpu.roll(ag, d, 0), 1.0)
            u_sh = jnp.where(ok, pltpu.roll(ug, d, 0), 0.0)
            ug = ag * u_sh + ug
            ag = ag * a_sh
        hg = ug + ag * hprev
        u_s[sl, :] = hg
        return jnp.broadcast_to(hg[SUBLANE - 1:SUBLANE, :], (SUBLANE, w))

    hc_ref[...] = lax.fori_loop(0, tt // SUBLANE, group, hc_ref[...], unroll=4)
    out_ref[...] = (u_s[...] * gate_ref[...].astype(F32)).astype(out_ref.dtype)


def _lru(xc, gate, wra, bra, wrx, brx, lam, bsz, tp, tt):
    w = xc.shape[1]
    nt = tp // tt
    row_map = lambda b, t: (b * nt + t, 0)
    const2 = lambda b, t: (0, 0)
    const3 = lambda b, t: (0, 0, 0)
    return pl.pallas_call(
        _lru_kernel,
        out_shape=jax.ShapeDtypeStruct(xc.shape, BF16),
        grid=(bsz, nt),
        in_specs=[pl.BlockSpec((tt, w), row_map), pl.BlockSpec((tt, w), row_map),
                  pl.BlockSpec((B_BLOCKS, B_BLOCK, B_BLOCK), const3), pl.BlockSpec((1, w), const2),
                  pl.BlockSpec((B_BLOCKS, B_BLOCK, B_BLOCK), const3), pl.BlockSpec((1, w), const2),
                  pl.BlockSpec((1, w), const2)],
        out_specs=pl.BlockSpec((tt, w), row_map),
        scratch_shapes=[pltpu.VMEM((SUBLANE, w), F32),
                        pltpu.VMEM((tt, w), F32), pltpu.VMEM((tt, w), F32)],
        compiler_params=_cparams(("parallel", "arbitrary")),
        name="rglru",
    )(xc, gate, wra, bra, wrx, brx, lam)


def _out_mlp_kernel(n_mix, from_input, *refs):
    n_h = 2 if from_input else 1
    h_refs, refs = refs[:n_h], refs[n_h:]
    mix_refs = refs[:n_mix]
    ln_ref = refs[n_mix]
    n_w = n_mix + 2
    w_hbm = refs[n_mix + 1:n_mix + 1 + n_w]
    out_ref = refs[n_mix + 1 + n_w]
    w_vmem = refs[n_mix + 2 + n_w:n_mix + 2 + 2 * n_w]
    sem = refs[n_mix + 2 + 2 * n_w]

    @pl.when((pl.program_id(0) == 0) & (pl.program_id(1) == 0))
    def _():
        copies = [pltpu.make_async_copy(src, dst, sem.at[i])
                  for i, (src, dst) in enumerate(zip(w_hbm, w_vmem))]
        for cp in copies:
            cp.start()
        for cp in copies:
            cp.wait()

    wo_refs, w1_ref, w2_ref = w_vmem[:n_mix], w_vmem[n_mix], w_vmem[n_mix + 1]
    res = _input_tile(*h_refs) if from_input else h_refs[0][...]
    for m_ref, w_ref in zip(mix_refs, wo_refs):
        res = res + _dot(m_ref[...], w_ref[...])
    hn = _rms_rows(res, ln_ref[...]).astype(BF16)
    a = jnp.maximum(_dot(hn, w1_ref[...]), 0.0)
    out_ref[...] = res + _dot((a * a).astype(BF16), w2_ref[...])


def _out_mlp(h_in, mixes, wouts, ln, w1, w2, tm, rows):
    d = w2.shape[1]
    n_mix = len(mixes)
    kind = rows[0]
    if kind == "all":
        r = rows[1]
        grid, out_rows = (r // tm, 1), r
        mix_spec = lambda n: pl.BlockSpec((tm, n), lambda i, j: (i, 0))
        h_specs, h_args = [mix_spec(d)], [h_in]
        out_spec = pl.BlockSpec((tm, d), lambda i, j: (i, 0))
    elif kind == "tokens":
        _, bsz, tp, seq = rows
        per_seq = seq // tm
        grid, out_rows = (bsz, per_seq), bsz * seq
        mix_spec = lambda n: pl.BlockSpec(
            (pl.Element(tm), pl.Element(n)),
            lambda b, i: (pl.multiple_of(b * tp + (tp - seq) + i * tm, LANE), 0))
        h_specs, h_args = [mix_spec(d)], [h_in]
        out_spec = pl.BlockSpec((tm, d), lambda b, i: (b * per_seq + i, 0))
    else:
        _, bsz, tp, seq = rows
        nt = tp // tm
        grid, out_rows = (bsz, nt), bsz * tp
        mix_spec = lambda n: pl.BlockSpec((tm, n), lambda b, t: (b * nt + t, 0))
        h_specs, h_args = _input_specs(seq, tm, d), list(h_in)
        out_spec = mix_spec(d)
    weights = [*wouts, w1, w2]
    in_specs = h_specs + [mix_spec(m.shape[1]) for m in mixes]
    in_specs += [pl.BlockSpec((1, d), lambda i, j: (0, 0))]
    in_specs += [pl.BlockSpec(memory_space=pl.ANY) for _ in weights]
    scratch = [pltpu.VMEM(w.shape, w.dtype) for w in weights]
    scratch.append(pltpu.SemaphoreType.DMA((len(weights),)))
    return pl.pallas_call(
        functools.partial(_out_mlp_kernel, n_mix, kind == "input"),
        out_shape=jax.ShapeDtypeStruct((out_rows, d), F32),
        grid=grid,
        in_specs=in_specs,
        out_specs=out_spec,
        scratch_shapes=scratch,
        compiler_params=_cparams(("arbitrary", "arbitrary")),
        name="out_mlp",
    )(*h_args, *mixes, ln, *weights)


OD_COLS = dict(cq=(0, 384), ckv=(384, 640), kr=(640, 768), fq=(768, 1280), fk=(1280, 1792),
               fv=(1792, 2304), ff=(2304, 2432))
OD_N = 2432


def _od_in_kernel(h_ref, ln_ref, w_ref, pos_ref, frq_ref, phs_ref, gqa_ref, gkva_ref,
                  wq_ref, wkn_ref, wv_ref, gq_ref, gk_ref, gkr_ref, gfq_ref, gfk_ref, bf_ref,
                  gm_ref, g64_ref, sel_ref,
                  qm_ref, km_ref, vmt_ref, qf_ref, kf_ref, vft_ref, fc_ref):
    t = pl.program_id(1)
    tm = h_ref.shape[1]

    @pl.when(t == 0)
    def _():
        fc_ref[...] = jnp.zeros_like(fc_ref)

    xn = _rms_rows(h_ref[0], ln_ref[...]).astype(BF16)

    def proj(name):
        a, b = OD_COLS[name]
        return _dot(xn, w_ref[:, a:b])

    z_cq, z_ckv, z_kr = proj("cq"), proj("ckv"), proj("kr")

    wide = 2 * LANE
    lane2 = lax.broadcasted_iota(jnp.int32, (tm, wide), 1) % LANE
    row2 = lax.broadcasted_iota(jnp.int32, (tm, wide), 0)
    unused_row2 = (row2 < PAD) & (t == 0)
    mask_lane = lane2 == MASK_LANE
    n_grp = LANE // C_ROPE
    tq = tm // n_grp
    pos = pos_ref[0]
    angc = phs_ref[...]
    for g in range(n_grp):
        angc = angc + pos[g * tq:(g + 1) * tq, :] * frq_ref[g:g + 1, :]
    cc = jnp.cos(angc)
    lane1 = lax.broadcasted_iota(jnp.int32, (tq, LANE), 1)
    in_lo = (lane1 >= ROPE_LO) & (lane1 < ROPE_MID)
    in_hi = (lane1 >= ROPE_MID) & (lane1 < ROPE_HI)
    cs_parts, sn_parts = [], []
    for g in range(n_grp):
        shift = (ROPE_LO - C_ROPE * g) % LANE
        c = pltpu.roll(cc, shift, 1) if shift else cc
        cs_parts.append(jnp.where(in_hi, pltpu.roll(c, HALF, 1), jnp.where(in_lo, c, 1.0)))
        sn_parts.append(jnp.where(in_lo, -pltpu.roll(c, LANE - HALF, 1), jnp.where(in_hi, c, 0.0)))
    cs = jnp.concatenate(cs_parts, axis=0)
    sn = jnp.concatenate(sn_parts, axis=0)
    cs = jnp.concatenate([cs, cs], axis=1)
    sn = jnp.concatenate([sn, sn], axis=1)
    first_half = lane2 < ROPE_MID

    def seg_rms(x, sel, gain):
        ms = _dot((x * x).astype(BF16), sel)
        return x * lax.rsqrt(ms + EPS) * gain

    def rope(y):
        sw = jnp.where(first_half, pltpu.roll(y, wide - HALF, 1), pltpu.roll(y, HALF, 1))
        return y * cs + sw * sn

    gm = gm_ref[...]
    g64 = g64_ref[...]

    cqn = _rms_rows(z_cq, gqa_ref[...]).astype(BF16)
    ckvn = _rms_rows(z_ckv, gkva_ref[...]).astype(BF16)
    qall = _dot(cqn, wq_ref[...])
    knall = _dot(ckvn, wkn_ref[...])
    scale_m = (C_NOPE + C_ROPE) ** -0.5 * LOG2E
    for h2 in range(C_HEADS // 2):
        cols = slice(h2 * wide, (h2 + 1) * wide)
        y = rope(seg_rms(qall[:, cols], gm,gq_ref[...]))
        qm_ref[0, :, cols] = jnp.where(mask_lane, 1.0, y * scale_m).astype(BF16)
    krt = rope(seg_rms(jnp.concatenate([z_kr, z_kr], axis=1), gm,gkr_ref[...]))
    krt = jnp.where(mask_lane & unused_row2, NEG, krt)
    for h2 in range(C_HEADS // 2):
        cols = slice(h2 * wide, (h2 + 1) * wide)
        y = seg_rms(knall[:, cols], gm,gk_ref[...]) + krt
        km_ref[0, :, cols] = y.astype(BF16)

    lane = lax.broadcasted_iota(jnp.int32, (tm, LANE), 1)
    unused_row = (lax.broadcasted_iota(jnp.int32, (tm, LANE), 0) < PAD) & (t == 0)
    valid = jnp.logical_not(unused_row) & (lane < D_HEADS)
    lf = jnp.where(valid, _log_sigmoid(proj("ff") + bf_ref[...]), 0.0)
    fc = _dot_sel_l(_tril_bf16(tm), lf, 2) + fc_ref[...]
    fc_ref[...] = fc[tm - 1:tm, :]
    hi, mid, lo = _split3(jnp.where(unused_row, NEG, fc * (-LOG2E)))
    bias = _dot(jnp.concatenate([hi, mid, lo], axis=1), sel_ref[...])
    feat = lane < D_HD
    ones3 = jnp.where((lane >= D_HD) & (lane < D_HD + 3), 1.0, 0.0)

    fq = proj("fq")
    fk = proj("fk")
    scale_f = D_HD ** -0.5 * LOG2E
    for p2 in range(D_HEADS // 4):
        cols = slice(p2 * wide, (p2 + 1) * wide)
        yq = seg_rms(fq[:, cols], g64,gfq_ref[...]) * scale_f
        yk = seg_rms(fk[:, cols], g64,gfk_ref[...])
        for j in range(4):
            h = 4 * p2 + j
            tq = yq[:, (j // 2) * LANE:(j // 2 + 1) * LANE]
            tk = yk[:, (j // 2) * LANE:(j // 2 + 1) * LANE]
            if j % 2:
                tq, tk = pltpu.roll(tq, D_HD, 1), pltpu.roll(tk, D_HD, 1)
            hs = slice(h * LANE, (h + 1) * LANE)
            qf_ref[0, :, hs] = jnp.where(feat, tq, ones3).astype(BF16)
            kf_ref[0, :, hs] = jnp.where(feat, tk, bias[:, hs]).astype(BF16)

    vft_ref[0] = proj("fv").T.astype(BF16)
    vmt_ref[0] = _dot(ckvn, wv_ref[...]).T.astype(BF16)


def _od_in(h3, pos_col, consts, tm):
    bsz, tp, d = h3.shape
    nt = tp // tm
    names = ["ln", "w", None, "frq", "phs", "gqa", "gkva", "wq", "wkn", "wv", "gq", "gk", "gkr",
             "gfq", "gfk", "bf", "gm", "g64", "sel"]
    args, in_specs = [h3], [pl.BlockSpec((1, tm, d), lambda b, t: (b, t, 0))]
    for nm in names:
        if nm is None:
            args.append(pos_col)
            in_specs.append(pl.BlockSpec((1, tm, 1), lambda b, t: (b, t, 0)))
            continue
        a = consts[nm]
        args.append(a)
        in_specs.append(pl.BlockSpec(a.shape, (lambda nd: (lambda b, t: (0,) * nd))(a.ndim)))
    hl = C_HEADS * LANE
    pk = D_HEADS * D_HD
    outs = [jax.ShapeDtypeStruct((bsz, tp, hl), BF16), jax.ShapeDtypeStruct((bsz, tp, hl), BF16),
            jax.ShapeDtypeStruct((bsz, pk, tp), BF16), jax.ShapeDtypeStruct((bsz, tp, hl), BF16),
            jax.ShapeDtypeStruct((bsz, tp, 2 * pk), BF16), jax.ShapeDtypeStruct((bsz, pk, tp), BF16)]
    rowspec = lambda n: pl.BlockSpec((1, tm, n), lambda b, t: (b, t, 0))
    tspec = pl.BlockSpec((1, pk, tm), lambda b, t: (b, 0, t))
    return pl.pallas_call(
        _od_in_kernel,
        out_shape=outs,
        grid=(bsz, nt),
        in_specs=in_specs,
        out_specs=[rowspec(hl), rowspec(hl), tspec, rowspec(hl), rowspec(hl), tspec],
        scratch_shapes=[pltpu.VMEM((1, LANE), F32)],
        compiler_params=_cparams(("parallel", "arbitrary")),
        name="od_in",
    )(*args)


def _attn_kernel(blk, nh, q_ref, k_ref, vt_ref, out_ref, st_ref, acc_ref, buf0, buf1, bufm,
                 mx0, mx1, mxm):
    tp = k_ref.shape[1]
    n_blocks = (tp - LANE) // blk
    half = LANE // 2

    def start_qblock(q0, nq):
        st_ref[0:nh, 0:nq] = jnp.full((nh, nq), NEG, F32)
        st_ref[nh:2 * nh, 0:nq] = jnp.zeros((nh, nq), F32)
        acc_ref[:, :, 0:nq] = jnp.zeros((nh, LANE, nq), F32)

    def logits_to(buf, mx, q0, nq, k0, nk, diagonal):
        if diagonal:
            ok = (lax.broadcasted_iota(jnp.int32, (nk, nq), 0)
                  <= lax.broadcasted_iota(jnp.int32, (nk, nq), 1))
        for hd in range(nh):
            s = _dot_nt(k_ref[0, pl.ds(k0, nk), hd * LANE:(hd + 1) * LANE],
                        q_ref[0, pl.ds(q0, nq), hd * LANE:(hd + 1) * LANE])
            if diagonal:
                s = jnp.where(ok, s, NEG)
            buf[hd, 0:nk, 0:nq] = s
            mx[hd:hd + 1, 0:nq] = jnp.max(s, axis=0, keepdims=True)

    def consume(buf, mx, nq, k0, nk):
        for hd in range(nh):
            pair = hd // 2
            vt = vt_ref[0, pair * LANE:(pair + 1) * LANE, pl.ds(k0, nk)]
            m_old = st_ref[hd:hd + 1, 0:nq]
            l_old = st_ref[nh + hd:nh + hd + 1, 0:nq]
            m_new = jnp.maximum(m_old, mx[hd:hd + 1, 0:nq])
            alpha = jnp.exp2(m_old - m_new)
            p = jnp.exp2(buf[hd, 0:nk, 0:nq] - m_new)
            st_ref[hd:hd + 1, 0:nq] = m_new
            st_ref[nh + hd:nh + hd + 1, 0:nq] = alpha * l_old + jnp.sum(p, axis=0, keepdims=True)
            acc_ref[hd, :, 0:nq] = acc_ref[hd, :, 0:nq] * alpha + _dot(vt, p.astype(BF16))

    def finish_qblock(q0, nq):
        for pair in range(nh // 2):
            a, b = 2 * pair, 2 * pair + 1
            o_a = acc_ref[a, 0:half, 0:nq] / st_ref[nh + a:nh + a + 1, 0:nq]
            o_b = acc_ref[b, half:LANE, 0:nq] / st_ref[nh + b:nh + b + 1, 0:nq]
            out_ref[0, pl.ds(q0, nq), pair * LANE:(pair + 1) * LANE] = (
                jnp.concatenate([o_a, o_b], axis=0).T.astype(out_ref.dtype))

    start_qblock(0, LANE)
    logits_to(bufm, mxm, 0, LANE, 0, LANE, True)
    consume(bufm, mxm, LANE, 0, LANE)
    finish_qblock(0, LANE)

    n_static = min(3, n_blocks)
    for i in range(n_static):
        q0 = LANE + i * blk
        tiles = [(bufm, mxm, 0, LANE, False)]
        for n in range(i):
            tiles.append(((buf0, buf1)[n % 2], (mx0, mx1)[n % 2], LANE + n * blk, blk, False))
        tiles.append(((buf0, buf1)[i % 2], (mx0, mx1)[i % 2], q0, blk, True))
        start_qblock(q0, blk)
        logits_to(tiles[0][0], tiles[0][1], q0, blk, tiles[0][2], tiles[0][3], tiles[0][4])
        for n, (buf, mx, k0, nk, _) in enumerate(tiles):
            if n + 1 < len(tiles):
                nbuf, nmx, nk0, nnk, ndiag = tiles[n + 1]
                logits_to(nbuf, nmx, q0, blk, nk0, nnk, ndiag)
            consume(buf, mx, blk, k0, nk)
        finish_qblock(q0, blk)

    def qblock(i, carry):
        q0 = pl.multiple_of(LANE + i * blk, LANE)

        def k_of(n):
            return pl.multiple_of(LANE + n * blk, LANE)

        def pair(m, c):
            logits_to(buf1, mx1, q0, blk, k_of(2 * m + 1), blk, False)
            consume(buf0, mx0, blk, k_of(2 * m), blk)
            logits_to(buf0, mx0, q0, blk, k_of(2 * m + 2), blk, False)
            consume(buf1, mx1, blk, k_of(2 * m + 1), blk)
            return c

        start_qblock(q0, blk)
        logits_to(bufm, mxm, q0, blk, 0, LANE, False)
        logits_to(buf0, mx0, q0, blk, k_of(0), blk, False)
        consume(bufm, mxm, blk, 0, LANE)
        pair(0, 0)
        n_pairs = (i - 1) // 2
        lax.fori_loop(1, n_pairs, pair, 0)
        held = 2 * n_pairs

        @pl.when(i % 2 == 1)
        def _():
            logits_to(buf1, mx1, q0, blk, q0, blk, True)
            consume(buf0, mx0, blk, k_of(held), blk)
            consume(buf1, mx1, blk, q0, blk)
            finish_qblock(q0, blk)

        @pl.when(i % 2 == 0)
        def _():
            logits_to(buf1, mx1, q0, blk, k_of(held + 1), blk, False)
            consume(buf0, mx0, blk, k_of(held), blk)
            logits_to(buf0, mx0, q0, blk, q0, blk, True)
            consume(buf1, mx1, blk, k_of(held + 1), blk)
            consume(buf0, mx0, blk, q0, blk)
            finish_qblock(q0, blk)

        return carry

    lax.fori_loop(n_static, n_blocks, qblock, 0)


def _attention(q, k, vt, blk, nh, name):
    bsz, tp, _ = q.shape
    npair = vt.shape[1] // LANE
    g = nh // 2
    assert 2 * nh <= SUBLANE and npair % g == 0
    scratch = [pltpu.VMEM((SUBLANE, blk), F32), pltpu.VMEM((nh, LANE, blk), F32),
               pltpu.VMEM((nh, blk, blk), F32), pltpu.VMEM((nh, blk, blk), F32),
               pltpu.VMEM((nh, LANE, blk), F32),
               pltpu.VMEM((SUBLANE, blk), F32), pltpu.VMEM((SUBLANE, blk), F32),
               pltpu.VMEM((SUBLANE, blk), F32)]
    return pl.pallas_call(
        functools.partial(_attn_kernel, blk, nh),
        out_shape=jax.ShapeDtypeStruct((bsz, tp, npair * LANE), BF16),
        grid=(bsz, npair // g),
        in_specs=[pl.BlockSpec((1, tp, nh * LANE), lambda b, p: (b, 0, p)),
                  pl.BlockSpec((1, tp, nh * LANE), lambda b, p: (b, 0, p)),
                  pl.BlockSpec((1, LANE * g, tp), lambda b, p: (b, p, 0))],
        out_specs=pl.BlockSpec((1, tp, LANE * g), lambda b, p: (b, 0, p)),
        scratch_shapes=scratch,
        compiler_params=_cparams(("parallel", "parallel")),
        name=name,
    )(q, k, vt)


def _place_cols(w, width, offset):
    k, n = w.shape
    return jnp.pad(w, ((0, 0), (offset, width - offset - n)))


def _ev_consts(w_in, b_if):
    nq, nv = A_HEADS * A_DQK, A_HEADS * A_DV
    o = 0
    wq = w_in[:, o:o + nq]; o += nq
    wk = w_in[:, o:o + nq]; o += nq
    wv = w_in[:, o:o + nv]; o += nv
    wo = w_in[:, o:o + nv]; o += nv
    wg = w_in[:, o:o + 2 * A_HEADS]; o += 2 * A_HEADS
    wxb = w_in[:, o:o + 1024]; o += 1024
    wgb = w_in[:, o:o + 1024]
    w = jnp.concatenate([wq, wk, wv, wo, _place_cols(wg, LANE, 0), wxb, wgb], axis=1).astype(BF16)
    bif = _place_cols(b_if[None, :], LANE, 0)
    return w, bif


def _head_tile_cols(w, n_heads, per_head, take, offset):
    k = w.shape[0]
    wh = w.reshape(k, n_heads, per_head)[:, :, take[0]:take[0] + take[1]]
    wh = jnp.pad(wh, ((0, 0), (0, 0), (offset, LANE - offset - take[1])))
    return wh.reshape(k, n_heads * LANE)


def _row128(parts):
    r = jnp.zeros((LANE,), F32)
    for off, v in parts:
        r = r.at[off:off + v.shape[0]].set(v.astype(F32))
    return r[None, :]


def _od_consts(p):
    c = {}
    w_in = p["w_in"]
    o = 0
    wcq = w_in[:, o:o + C_Q_LORA]; o += C_Q_LORA
    wckv = w_in[:, o:o + C_KV_LORA]; o += C_KV_LORA
    wkr = w_in[:, o:o + C_ROPE]; o += C_ROPE
    nf = D_HEADS * D_HD
    wfq = w_in[:, o:o + nf]; o += nf
    wfk = w_in[:, o:o + nf]; o += nf
    wfv = w_in[:, o:o + nf]; o += nf
    wff = w_in[:, o:o + D_HEADS]
    c["w"] = jnp.concatenate([wcq, wckv, _place_cols(wkr, LANE, ROPE_LO), wfq, wfk, wfv,
                              _place_cols(wff, LANE, 0)], axis=1).astype(BF16)
    c["ln"] = p["ln"][None, :]
    freqs = ROPE_THETA ** (-jnp.arange(HALF, dtype=F32) / HALF)
    n_grp = LANE // C_ROPE
    c["frq"] = jnp.concatenate([_row128([(C_ROPE * g, freqs), (C_ROPE * g + HALF, freqs)])
                                for g in range(n_grp)], axis=0)
    c["phs"] = _row128([(C_ROPE * g + HALF, jnp.full((HALF,), -0.5 * math.pi, F32)) for g in range(n_grp)])
    c["gqa"] = p["g_qa"][None, :]
    c["gkva"] = p["g_kva"][None, :]
    dq = C_NOPE + C_ROPE
    wuq, wukv = p["w_uq"], p["w_ukv"]
    c["wq"] = (_head_tile_cols(wuq, C_HEADS, dq, (0, dq), 0)).astype(BF16)
    c["wkn"] = _head_tile_cols(wukv, C_HEADS, C_NOPE + C_V, (0, C_NOPE), 0).astype(BF16)
    c["wv"] = wukv.reshape(-1, C_HEADS, C_NOPE + C_V)[:, :, C_NOPE:].reshape(-1, C_HEADS * C_V).astype(BF16)
    c["gq"] = _row128([(0, p["g_qn"]), (ROPE_LO, p["g_qr"])])
    c["gk"] = _row128([(0, p["g_kn"])])
    c["gkr"] = _row128([(ROPE_LO, p["g_kr"])])
    c["gfq"] = _row128([(0, p["g_fq"]), (D_HD, p["g_fq"])])
    c["gfk"] = _row128([(0, p["g_fk"]), (D_HD, p["g_fk"])])
    c["bf"] = _row128([(0, p["b_f"])])
    seg = np.where(np.arange(LANE) < C_NOPE, 0, np.where(np.arange(LANE) < ROPE_HI, 1, 2))
    inv_len = np.where(seg == 0, 1.0 / C_NOPE, 1.0 / C_ROPE)[None, :]
    c["gm"] = jnp.asarray((seg[:, None] == seg[None, :]) * inv_len, F32).astype(BF16)
    seg64 = np.arange(LANE) // D_HD
    c["g64"] = jnp.asarray((seg64[:, None] == seg64[None, :]) * (1.0 / D_HD), F32).astype(BF16)
    sel = np.zeros((3 * LANE, D_HEADS * LANE), np.float32)
    for h in range(D_HEADS):
        for part in range(3):
            sel[part * LANE + h, h * LANE + D_HD + part] = 1.0
    c["sel"] = jnp.asarray(sel).astype(BF16)
    for nm in ("gq", "gk", "gkr", "gfq", "gfk"):
        c[nm] = jnp.tile(c[nm], (1, 2))
    for nm in ("gm", "g64"):
        z = jnp.zeros_like(c[nm])
        c[nm] = jnp.block([[c[nm], z], [z, c[nm]]])
    return c


def _time_tile(tp):
    return 3 * LANE if tp % (3 * LANE) == 0 else LANE


def kernel(x, positions, meta, ev_ln, ev_w_in, ev_b_if, ev_a_norm, ev_conv_w, ev_conv_b, ev_w_ra, ev_b_ra, ev_w_rx, ev_b_rx, ev_lam, ev_w_out, od_ln, od_w_in, od_b_f, od_g_qa, od_g_kva, od_w_uq, od_w_ukv, od_g_qn, od_g_qr, od_g_kn, od_g_kr, od_g_fq, od_g_fk, od_w_out, mlp_ln, w_ff1, w_ff2):
    bsz, seq, d = x.shape
    tp = LANE + seq
    r = bsz * tp
    tm = 512 if seq % 512 == 0 else LANE
    tt = _time_tile(tp)

    x2 = x.reshape(bsz * seq, d)
    mb = jnp.concatenate([jnp.zeros((PAD, d), x.dtype), meta.astype(x.dtype)], axis=0)
    pos_full = jnp.concatenate([jnp.zeros((bsz, PAD), jnp.int32),
                                jnp.broadcast_to(jnp.arange(N_META, dtype=jnp.int32)[None, :], (bsz, N_META)),
                                positions + N_META], axis=1)
    pos_col = pos_full.astype(F32)[:, :, None]

    e = 0
    w_a, bif = _ev_consts(ev_w_in[e], ev_b_if[e])
    q, k, v, o, g, xc, gate = _ev_in(x2, mb, ev_ln[e][None, :], w_a, ev_conv_w[e], ev_conv_b[e][None, :],
                                     bsz, tp, tt)
    ha = _mlstm(q, k, v, o, g, bif, ev_a_norm[e][None, :], bsz, tp, tt)
    hb = _lru(xc, gate, (0.5 * ev_w_ra[e]).astype(BF16), 0.5 * ev_b_ra[e][None, :],
              (0.5 * ev_w_rx[e]).astype(BF16), 0.5 * ev_b_rx[e][None, :], ev_lam[e][None, :], bsz, tp, tt)
    nv = A_HEADS * A_DV
    wo = ev_w_out[e].astype(BF16)
    h2 = _out_mlp((x2, mb), [ha, hb], [wo[:nv], wo[nv:]], mlp_ln[0][None, :],
                  w_ff1[0].astype(BF16), w_ff2[0].astype(BF16), tt, ("input", bsz, tp, seq))

    oi = 0
    consts = _od_consts(dict(ln=od_ln[oi], w_in=od_w_in[oi], b_f=od_b_f[oi], g_qa=od_g_qa[oi],
                             g_kva=od_g_kva[oi], w_uq=od_w_uq[oi], w_ukv=od_w_ukv[oi],
                             g_qn=od_g_qn[oi], g_qr=od_g_qr[oi], g_kn=od_g_kn[oi], g_kr=od_g_kr[oi],
                             g_fq=od_g_fq[oi], g_fk=od_g_fk[oi]))
    qm, km, vmt, qf, kf, vft = _od_in(h2.reshape(bsz, tp, d), pos_col, consts, tt)
    blk = 512 if seq % 512 == 0 else LANE
    hc = _attention(qm, km, vmt, blk, 4, "mla_attn")
    hd = _attention(qf, kf, vft, blk, 4, "fox_attn")
    nc = C_HEADS * C_V
    wo = od_w_out[oi].astype(BF16)
    out = _out_mlp(h2, [hc.reshape(r, nc), hd.reshape(r, nc)], [wo[:nc], wo[nc:]], mlp_ln[1][None, :],
                   w_ff1[1].astype(BF16), w_ff2[1].astype(BF16), tm, ("tokens", bsz, tp, seq))
    return out.reshape(bsz, seq, d)
```

```python
import functools
import math

import numpy as np
import jax
import jax.numpy as jnp
from jax import lax
from jax.experimental import pallas as pl
from jax.experimental.pallas import tpu as pltpu

F32 = jnp.float32
BF16 = jnp.bfloat16

N_META = 16
EPS = 1e-6
A_HEADS = 4
A_DQK = 128
A_DV = 256
B_BLOCKS = 8
B_BLOCK = 128
CONV_W = 4
LRU_C = 8.0
C_HEADS = 8
C_Q_LORA = 384
C_KV_LORA = 256
C_NOPE = 64
C_ROPE = 32
C_V = 64
ROPE_THETA = 10000.0
D_HEADS = 8
D_HD = 64

LANE = 128
SUBLANE = 8
VMEM_LIMIT = 56 * 1024 * 1024

PAD = LANE - N_META
NEG = -1e30
LOG2E = math.log2(math.e)

HALF = C_ROPE // 2
ROPE_LO = C_NOPE
ROPE_MID = C_NOPE + HALF
ROPE_HI = C_NOPE + C_ROPE
MASK_LANE = ROPE_HI


def _cparams(sem):
    return pltpu.CompilerParams(dimension_semantics=sem, vmem_limit_bytes=VMEM_LIMIT)


def _dot(a, b):
    return jnp.dot(a, b, preferred_element_type=F32)


def _dot_nt(a, b):
    return lax.dot_general(a, b, (((1,), (1,)), ((), ())), preferred_element_type=F32)


def _split2(x):
    hi = x.astype(BF16)
    lo = (x - hi.astype(F32)).astype(BF16)
    return hi, lo


def _split3(x):
    hi = x.astype(BF16)
    r = x - hi.astype(F32)
    mid = r.astype(BF16)
    lo = (r - mid.astype(F32)).astype(BF16)
    return hi, mid, lo


def _dot_sel_l(sel, x, parts):
    pieces = _split3(x) if parts == 3 else _split2(x)
    acc = _dot(sel, pieces[0])
    for p in pieces[1:]:
        acc = acc + _dot(sel, p)
    return acc


def _rms_rows(x, g):
    ms = jnp.mean(x * x, axis=-1, keepdims=True)
    return x * lax.rsqrt(ms + EPS) * g


def _log_sigmoid(x):
    return -(jnp.maximum(-x, 0.0) + jnp.log1p(jnp.exp(-jnp.abs(x))))


def _sigmoid(x):
    return 0.5 * jnp.tanh(0.5 * x) + 0.5


def _tril_bf16(n):
    r = lax.broadcasted_iota(jnp.int32, (n, n), 0)
    c = lax.broadcasted_iota(jnp.int32, (n, n), 1)
    return jnp.where(r >= c, 1.0, 0.0).astype(BF16)


EV_SLABS = (("q", 512, BF16), ("k", 512, BF16), ("v", 1024, BF16), ("o", 1024, BF16),
            ("g", 128, F32), ("xb", 1024, BF16), ("gb", 1024, BF16))


def _input_tile(x_ref, mb_ref):
    xt = x_ref[...]
    first = jnp.concatenate([mb_ref[...], xt[:xt.shape[0] - LANE]], axis=0)
    return jnp.where(pl.program_id(1) == 0, first, xt)


def _input_specs(seq, tt, d):
    x_spec = pl.BlockSpec(
        (pl.Element(tt), pl.Element(d)),
        lambda b, t: (pl.multiple_of(b * seq + jnp.maximum(t * tt - LANE, 0), LANE), 0))
    return [x_spec, pl.BlockSpec((LANE, d), lambda b, t: (0, 0))]


def _ev_in_kernel(x_ref, mb_ref, ln_ref, w_ref, cw_ref, cb_ref, *refs):
    out_refs, hist_ref = refs[:-1], refs[-1]
    tt = x_ref.shape[0]

    @pl.when(pl.program_id(1) == 0)
    def _():
        hist_ref[...] = jnp.zeros_like(hist_ref)

    xn = _rms_rows(_input_tile(x_ref, mb_ref), ln_ref[...]).astype(BF16)
    offsets = np.cumsum([0] + [width for _, width, _ in EV_SLABS])
    order = sorted(range(len(EV_SLABS)), key=lambda s: {"xb": 0, "q": 1, "gb": 2, "k": 3, "o": 4}.get(EV_SLABS[s][0], 5))
    for s in order:
        name, width, dt = EV_SLABS[s]
        ref = out_refs[s]
        z = _dot(xn, w_ref[:, int(offsets[s]):int(offsets[s]) + width])
        if name == "q":
            z = z * (A_DQK ** -0.5)
        elif name == "o":
            z = _sigmoid(z)
        elif name == "gb":
            z = _gelu_tanh(z)
        elif name == "xb":
            xfull = jnp.concatenate([hist_ref[...], z], axis=0)
            hist_ref[...] = z[tt - SUBLANE:, :]
            conv = cb_ref[...] + cw_ref[CONV_W - 1:CONV_W, :] * z
            for d in range(1, CONV_W):
                conv = conv + cw_ref[CONV_W - 1 - d:CONV_W - d, :] * pltpu.roll(xfull, d, 0)[SUBLANE:, :]
            z = conv
        ref[...] = z.astype(dt)


def _ev_in(x2, mb, ln, w, cw, cb, bsz, tp, tt):
    d = x2.shape[1]
    n = w.shape[1]
    nt = tp // tt
    const = lambda b, t: (0, 0)
    outs = [jax.ShapeDtypeStruct((bsz * tp, width), dt) for (_, width, dt) in EV_SLABS]
    return pl.pallas_call(
        _ev_in_kernel,
        out_shape=outs,
        grid=(bsz, nt),
        in_specs=_input_specs(x2.shape[0] // bsz, tt, d) + [
            pl.BlockSpec((1, d), const), pl.BlockSpec((d, n), const),
            pl.BlockSpec(cw.shape, const), pl.BlockSpec(cb.shape, const)],
        out_specs=[pl.BlockSpec((tt, width), lambda b, t: (b * nt + t, 0)) for (_, width, _) in EV_SLABS],
        scratch_shapes=[pltpu.VMEM((SUBLANE, cw.shape[1]), F32)],
        compiler_params=_cparams(("parallel", "arbitrary")),
        name="ev_in",
    )(x2, mb, ln, w, cw, cb)


def _mlstm_kernel(q_ref, k_ref, v_ref, o_ref, g_ref, bif_ref, an_ref, out_ref, cn_ref, m_ref):
    c = pl.program_id(1)
    L = q_ref.shape[0]

    @pl.when(c == 0)
    def _():
        cn_ref[...] = jnp.zeros_like(cn_ref)
        m_ref[...] = jnp.zeros_like(m_ref)

    lane = lax.broadcasted_iota(jnp.int32, (L, LANE), 1)
    row = lax.broadcasted_iota(jnp.int32, (L, LANE), 0)
    valid = (row >= PAD) | (c > 0)
    g = g_ref[...] + bif_ref[...]
    li = jnp.where(valid, g, NEG)
    lf = jnp.where(valid, _log_sigmoid(g), 0.0)
    is_f = (lane >= A_HEADS) & (lane < 2 * A_HEADS)
    lf = jnp.where(is_f, lf, 0.0)
    tril = _tril_bf16(L)
    bcum = _dot_sel_l(tril, lf, 2)
    li_t = li.T
    bcum_t = bcum.T

    rr = lax.broadcasted_iota(jnp.int32, (L, L), 0)
    cc = lax.broadcasted_iota(jnp.int32, (L, L), 1)
    causal = rr >= cc
    e0 = jnp.where(lane == 0, 1.0, 0.0).astype(BF16)

    heads = []
    for h in range(A_HEADS):
        qs = q_ref[:, h * A_DQK:(h + 1) * A_DQK]
        kk = k_ref[:, h * A_DQK:(h + 1) * A_DQK]
        vext = jnp.concatenate([v_ref[:, h * A_DV:(h + 1) * A_DV], e0], axis=1)
        i_col = li[:, h:h + 1]
        b_col = bcum[:, A_HEADS + h:A_HEADS + h + 1]
        i_row = li_t[h:h + 1, :]
        b_row = bcum_t[A_HEADS + h:A_HEADS + h + 1, :]
        m_state = m_ref[:, h:h + 1]
        cn = cn_ref[h]
        qk = _dot_nt(qs, kk)
        qc = _dot(qs, cn.astype(BF16))

        gsum = b_col[L - 1:L, :]
        dk_col = gsum - b_col + i_col
        dk_row = gsum - b_row + i_row
        m_new = jnp.maximum(gsum + m_state, jnp.max(dk_row, axis=1, keepdims=True))
        wk = jnp.exp(dk_col - m_new)
        sc = jnp.exp(gsum + m_state - m_new)
        kw_t = (kk.astype(F32) * wk).T.astype(BF16)
        cn_ref[h] = sc * cn + _dot(kw_t, vext)
        m_ref[:, h:h + 1] = m_new
        heads.append((qk, qc, vext, b_col, b_row, i_row, m_state))

    weighted = []
    for qk, qc, vext, b_col, b_row, i_row, m_state in heads:
        dmat = jnp.where(causal, b_col - b_row + i_row, NEG)
        inter = b_col + m_state
        m_row = jnp.maximum(inter, jnp.max(dmat, axis=1, keepdims=True))
        w_intra = jnp.exp(dmat - m_row)
        w_inter = jnp.exp(inter - m_row)
        s = (qk * w_intra).astype(BF16)
        weighted.append((w_inter * qc + _dot(s, vext), m_row))

    for h, (numden, m_row) in enumerate(weighted):
        num = numden[:, :A_DV]
        den = numden[:, A_DV:A_DV + 1]
        hh = num / jnp.maximum(jnp.abs(den), jnp.exp(-m_row))
        hn = _rms_rows(hh, an_ref[...])
        og = o_ref[:, h * A_DV:(h + 1) * A_DV].astype(F32)
        out_ref[:, h * A_DV:(h + 1) * A_DV] = (hn * og).astype(out_ref.dtype)


def _mlstm(q, k, v, o, g, bif, an, bsz, tp, L):
    nq, nv = A_HEADS * A_DQK, A_HEADS * A_DV
    nc = tp // L
    row_map = lambda b, c: (b * nc + c, 0)
    const = lambda b, c: (0, 0)
    return pl.pallas_call(
        _mlstm_kernel,
        out_shape=jax.ShapeDtypeStruct((bsz * tp, nv), BF16),
        grid=(bsz, nc),
        in_specs=[pl.BlockSpec((L, nq), row_map), pl.BlockSpec((L, nq), row_map),
                  pl.BlockSpec((L, nv), row_map), pl.BlockSpec((L, nv), row_map),
                  pl.BlockSpec((L, LANE), row_map),
                  pl.BlockSpec((1, LANE), const), pl.BlockSpec((1, A_DV), const)],
        out_specs=pl.BlockSpec((L, nv), row_map),
        scratch_shapes=[pltpu.VMEM((A_HEADS, A_DQK, A_DV + LANE), F32),
                        pltpu.VMEM((1, LANE), F32)],
        compiler_params=_cparams(("parallel", "arbitrary")),
        name="mlstm",
    )(q, k, v, o, g, bif, an)


def _gelu_tanh(x):
    c0 = math.sqrt(2.0 / math.pi)
    return x * (0.5 * jnp.tanh(x * (c0 + (c0 * 0.044715) * (x * x))) + 0.5)


def _lru_kernel(xc_ref, gate_ref, wra_ref, bra_ref, wrx_ref, brx_ref, lam_ref,
                out_ref, hc_ref, a_s, u_s):
    t = pl.program_id(1)
    tt, w = xc_ref.shape

    @pl.when(t == 0)
    def _():
        hc_ref[...] = jnp.zeros_like(hc_ref)

    row = lax.broadcasted_iota(jnp.int32, (tt, w), 0)
    valid = (row >= PAD) | (t > 0)
    xcb = xc_ref[...]
    conv = xcb.astype(F32)
    r_parts, i_parts = [], []
    for blk in range(B_BLOCKS):
        xs = xcb[:, blk * B_BLOCK:(blk + 1) * B_BLOCK]
        r_parts.append(_dot(xs, wra_ref[blk]))
        i_parts.append(_dot(xs, wrx_ref[blk]))
    r = 0.5 * jnp.tanh(jnp.concatenate(r_parts, axis=1) + bra_ref[...]) + 0.5
    ig = 0.5 * jnp.tanh(jnp.concatenate(i_parts, axis=1) + brx_ref[...]) + 0.5
    nlam = -lam_ref[...]
    softplus = jnp.maximum(nlam, 0.0) + jnp.log1p(jnp.exp(-jnp.abs(nlam)))
    a = jnp.exp2(((-LRU_C * LOG2E) * softplus) * r)
    y = 1.0 - a * a
    u = jnp.where(y > 0.0, y * lax.rsqrt(y), 0.0) * (ig * conv)
    a_s[...] = a
    u_s[...] = jnp.where(valid, u, 0.0)

    srow = lax.broadcasted_iota(jnp.int32, (SUBLANE, w), 0)

    def group(gi, hprev):
        sl = pl.ds(pl.multiple_of(gi * SUBLANE, SUBLANE), SUBLANE)
        ag = a_s[sl, :]
        ug = u_s[sl, :]
        for d in (1, 2, 4):
            ok = srow >= d
            a_sh = jnp.where(ok, pltpu.roll(ag, d, 0), 1.0)
            u_sh = jnp.where(ok, pltpu.roll(ug, d, 0), 0.0)
            ug = ag * u_sh + ug
            ag = ag * a_sh
        hg = ug + ag * hprev
        u_s[sl, :] = hg
        return jnp.broadcast_to(hg[SUBLANE - 1:SUBLANE, :], (SUBLANE, w))

    hc_ref[...] = lax.fori_loop(0, tt // SUBLANE, group, hc_ref[...], unroll=4)
    out_ref[...] = (u_s[...] * gate_ref[...].astype(F32)).astype(out_ref.dtype)


def _lru(xc, gate, wra, bra, wrx, brx, lam, bsz, tp, tt):
    w = xc.shape[1]
    nt = tp // tt
    row_map = lambda b, t: (b * nt + t, 0)
    const2 = lambda b, t: (0, 0)
    const3 = lambda b, t: (0, 0, 0)
    return pl.pallas_call(
        _lru_kernel,
        out_shape=jax.ShapeDtypeStruct(xc.shape, BF16),
        grid=(bsz, nt),
        in_specs=[pl.BlockSpec((tt, w), row_map), pl.BlockSpec((tt, w), row_map),
                  pl.BlockSpec((B_BLOCKS, B_BLOCK, B_BLOCK), const3), pl.BlockSpec((1, w), const2),
                  pl.BlockSpec((B_BLOCKS, B_BLOCK, B_BLOCK), const3), pl.BlockSpec((1, w), const2),
                  pl.BlockSpec((1, w), const2)],
        out_specs=pl.BlockSpec((tt, w), row_map),
        scratch_shapes=[pltpu.VMEM((SUBLANE, w), F32),
                        pltpu.VMEM((tt, w), F32), pltpu.VMEM((tt, w), F32)],
        compiler_params=_cparams(("parallel", "arbitrary")),
        name="rglru",
    )(xc, gate, wra, bra, wrx, brx, lam)


def _out_mlp_kernel(n_mix, from_input, *refs):
    n_h = 2 if from_input else 1
    h_refs, refs = refs[:n_h], refs[n_h:]
    mix_refs = refs[:n_mix]
    ln_ref = refs[n_mix]
    n_w = n_mix + 2
    w_hbm = refs[n_mix + 1:n_mix + 1 + n_w]
    out_ref = refs[n_mix + 1 + n_w]
    w_vmem = refs[n_mix + 2 + n_w:n_mix + 2 + 2 * n_w]
    sem = refs[n_mix + 2 + 2 * n_w]

    @pl.when((pl.program_id(0) == 0) & (pl.program_id(1) == 0))
    def _():
        copies = [pltpu.make_async_copy(src, dst, sem.at[i])
                  for i, (src, dst) in enumerate(zip(w_hbm, w_vmem))]
        for cp in copies:
            cp.start()
        for cp in copies:
            cp.wait()

    wo_refs, w1_ref, w2_ref = w_vmem[:n_mix], w_vmem[n_mix], w_vmem[n_mix + 1]
    res = _input_tile(*h_refs) if from_input else h_refs[0][...]
    for m_ref, w_ref in zip(mix_refs, wo_refs):
        res = res + _dot(m_ref[...], w_ref[...])
    hn = _rms_rows(res, ln_ref[...]).astype(BF16)
    a = jnp.maximum(_dot(hn, w1_ref[...]), 0.0)
    out_ref[...] = res + _dot((a * a).astype(BF16), w2_ref[...])


def _out_mlp(h_in, mixes, wouts, ln, w1, w2, tm, rows):
    d = w2.shape[1]
    n_mix = len(mixes)
    kind = rows[0]
    if kind == "all":
        r = rows[1]
        grid, out_rows = (r // tm, 1), r
        mix_spec = lambda n: pl.BlockSpec((tm, n), lambda i, j: (i, 0))
        h_specs, h_args = [mix_spec(d)], [h_in]
        out_spec = pl.BlockSpec((tm, d), lambda i, j: (i, 0))
    elif kind == "tokens":
        _, bsz, tp, seq = rows
        per_seq = seq // tm
        grid, out_rows = (bsz, per_seq), bsz * seq
        mix_spec = lambda n: pl.BlockSpec(
            (pl.Element(tm), pl.Element(n)),
            lambda b, i: (pl.multiple_of(b * tp + (tp - seq) + i * tm, LANE), 0))
        h_specs, h_args = [mix_spec(d)], [h_in]
        out_spec = pl.BlockSpec((tm, d), lambda b, i: (b * per_seq + i, 0))
    else:
        _, bsz, tp, seq = rows
        nt = tp // tm
        grid, out_rows = (bsz, nt), bsz * tp
        mix_spec = lambda n: pl.BlockSpec((tm, n), lambda b, t: (b * nt + t, 0))
        h_specs, h_args = _input_specs(seq, tm, d), list(h_in)
        out_spec = mix_spec(d)
    weights = [*wouts, w1, w2]
    in_specs = h_specs + [mix_spec(m.shape[1]) for m in mixes]
    in_specs += [pl.BlockSpec((1, d), lambda i, j: (0, 0))]
    in_specs += [pl.BlockSpec(memory_space=pl.ANY) for _ in weights]
    scratch = [pltpu.VMEM(w.shape, w.dtype) for w in weights]
    scratch.append(pltpu.SemaphoreType.DMA((len(weights),)))
    return pl.pallas_call(
        functools.partial(_out_mlp_kernel, n_mix, kind == "input"),
        out_shape=jax.ShapeDtypeStruct((out_rows, d), F32),
        grid=grid,
        in_specs=in_specs,
        out_specs=out_spec,
        scratch_shapes=scratch,
        compiler_params=_cparams(("arbitrary", "arbitrary")),
        name="out_mlp",
    )(*h_args, *mixes, ln, *weights)


OD_COLS = dict(cq=(0, 384), ckv=(384, 640), kr=(640, 768), fq=(768, 1280), fk=(1280, 1792),
               fv=(1792, 2304), ff=(2304, 2432))


def _od_in_kernel(h_ref, ln_ref, w_ref, pos_ref, frq_ref, phs_ref, gqa_ref, gkva_ref,
                  wq_ref, wkn_ref, wv_ref, gq_ref, gk_ref, gkr_ref, gfq_ref, gfk_ref, bf_ref,
                  gm_ref, g64_ref, sel_ref,
                  qm_ref, km_ref, vmt_ref, qf_ref, kf_ref, vft_ref, fc_ref):
    t = pl.program_id(1)
    tm = h_ref.shape[1]

    @pl.when(t == 0)
    def _():
        fc_ref[...] = jnp.zeros_like(fc_ref)

    xn = _rms_rows(h_ref[0], ln_ref[...]).astype(BF16)

    def proj(name):
        a, b = OD_COLS[name]
        return _dot(xn, w_ref[:, a:b])

    z_cq, z_ckv, z_kr = proj("cq"), proj("ckv"), proj("kr")

    wide = 2 * LANE
    lane2 = lax.broadcasted_iota(jnp.int32, (tm, wide), 1) % LANE
    row2 = lax.broadcasted_iota(jnp.int32, (tm, wide), 0)
    unused_row2 = (row2 < PAD) & (t == 0)
    mask_lane = lane2 == MASK_LANE
    n_grp = LANE // C_ROPE
    tq = tm // n_grp
    pos = pos_ref[0]
    angc = phs_ref[...]
    for g in range(n_grp):
        angc = angc + pos[g * tq:(g + 1) * tq, :] * frq_ref[g:g + 1, :]
    cc = jnp.cos(angc)
    lane1 = lax.broadcasted_iota(jnp.int32, (tq, LANE), 1)
    in_lo = (lane1 >= ROPE_LO) & (lane1 < ROPE_MID)
    in_hi = (lane1 >= ROPE_MID) & (lane1 < ROPE_HI)
    cs_parts, sn_parts = [], []
    for g in range(n_grp):
        shift = (ROPE_LO - C_ROPE * g) % LANE
        c = pltpu.roll(cc, shift, 1) if shift else cc
        cs_parts.append(jnp.where(in_hi, pltpu.roll(c, HALF, 1), jnp.where(in_lo, c, 1.0)))
        sn_parts.append(jnp.where(in_lo, -pltpu.roll(c, LANE - HALF, 1), jnp.where(in_hi, c, 0.0)))
    cs = jnp.concatenate(cs_parts, axis=0)
    sn = jnp.concatenate(sn_parts, axis=0)
    cs = jnp.concatenate([cs, cs], axis=1)
    sn = jnp.concatenate([sn, sn], axis=1)
    first_half = lane2 < ROPE_MID

    def seg_rms(x, sel, gain):
        ms = _dot((x * x).astype(BF16), sel)
        return x * lax.rsqrt(ms + EPS) * gain

    def rope(y):
        sw = jnp.where(first_half, pltpu.roll(y, wide - HALF, 1), pltpu.roll(y, HALF, 1))
        return y * cs + sw * sn

    gm = gm_ref[...]
    g64 = g64_ref[...]

    cqn = _rms_rows(z_cq, gqa_ref[...]).astype(BF16)
    ckvn = _rms_rows(z_ckv, gkva_ref[...]).astype(BF16)
    qall = _dot(cqn, wq_ref[...])
    knall = _dot(ckvn, wkn_ref[...])
    scale_m = (C_NOPE + C_ROPE) ** -0.5 * LOG2E
    for h2 in range(C_HEADS // 2):
        cols = slice(h2 * wide, (h2 + 1) * wide)
        y = rope(seg_rms(qall[:, cols], gm,gq_ref[...]))
        qm_ref[0, :, cols] = jnp.where(mask_lane, 1.0, y * scale_m).astype(BF16)
    krt = rope(seg_rms(jnp.concatenate([z_kr, z_kr], axis=1), gm,gkr_ref[...]))
    krt = jnp.where(mask_lane & unused_row2, NEG, krt)
    for h2 in range(C_HEADS // 2):
        cols = slice(h2 * wide, (h2 + 1) * wide)
        y = seg_rms(knall[:, cols], gm,gk_ref[...]) + krt
        km_ref[0, :, cols] = y.astype(BF16)

    lane = lax.broadcasted_iota(jnp.int32, (tm, LANE), 1)
    unused_row = (lax.broadcasted_iota(jnp.int32, (tm, LANE), 0) < PAD) & (t == 0)
    valid = jnp.logical_not(unused_row) & (lane < D_HEADS)
    lf = jnp.where(valid, _log_sigmoid(proj("ff") + bf_ref[...]), 0.0)
    fc = _dot_sel_l(_tril_bf16(tm), lf, 2) + fc_ref[...]
    fc_ref[...] = fc[tm - 1:tm, :]
    hi, mid, lo = _split3(jnp.where(unused_row, NEG, fc * (-LOG2E)))
    bias = _dot(jnp.concatenate([hi, mid, lo], axis=1), sel_ref[...])
    feat = lane < D_HD
    ones3 = jnp.where((lane >= D_HD) & (lane < D_HD + 3), 1.0, 0.0)

    fq = proj("fq")
    fk = proj("fk")
    scale_f = D_HD ** -0.5 * LOG2E
    for p2 in range(D_HEADS // 4):
        cols = slice(p2 * wide, (p2 + 1) * wide)
        yq = seg_rms(fq[:, cols], g64,gfq_ref[...]) * scale_f
        yk = seg_rms(fk[:, cols], g64,gfk_ref[...])
        for j in range(4):
            h = 4 * p2 + j
            tq = yq[:, (j // 2) * LANE:(j // 2 + 1) * LANE]
            tk = yk[:, (j // 2) * LANE:(j // 2 + 1) * LANE]
            if j % 2:
                tq, tk = pltpu.roll(tq, D_HD, 1), pltpu.roll(tk, D_HD, 1)
            hs = slice(h * LANE, (h + 1) * LANE)
            qf_ref[0, :, hs] = jnp.where(feat, tq, ones3).astype(BF16)
            kf_ref[0, :, hs] = jnp.where(feat, tk, bias[:, hs]).astype(BF16)

    vft_ref[0] = proj("fv").T.astype(BF16)
    vmt_ref[0] = _dot(ckvn, wv_ref[...]).T.astype(BF16)


def _od_in(h3, pos_col, consts, tm):
    bsz, tp, d = h3.shape
    nt = tp // tm
    names = ["ln", "w", None, "frq", "phs", "gqa", "gkva", "wq", "wkn", "wv", "gq", "gk", "gkr",
             "gfq", "gfk", "bf", "gm", "g64", "sel"]
    args, in_specs = [h3], [pl.BlockSpec((1, tm, d), lambda b, t: (b, t, 0))]
    for nm in names:
        if nm is None:
            args.append(pos_col)
            in_specs.append(pl.BlockSpec((1, tm, 1), lambda b, t: (b, t, 0)))
            continue
        a = consts[nm]
        args.append(a)
        in_specs.append(pl.BlockSpec(a.shape, (lambda nd: (lambda b, t: (0,) * nd))(a.ndim)))
    hl = C_HEADS * LANE
    pk = D_HEADS * D_HD
    outs = [jax.ShapeDtypeStruct((bsz, tp, hl), BF16), jax.ShapeDtypeStruct((bsz, tp, hl), BF16),
            jax.ShapeDtypeStruct((bsz, pk, tp), BF16), jax.ShapeDtypeStruct((bsz, tp, hl), BF16),
            jax.ShapeDtypeStruct((bsz, tp, 2 * pk), BF16), jax.ShapeDtypeStruct((bsz, pk, tp), BF16)]
    rowspec = lambda n: pl.BlockSpec((1, tm, n), lambda b, t: (b, t, 0))
    tspec = pl.BlockSpec((1, pk, tm), lambda b, t: (b, 0, t))
    return pl.pallas_call(
        _od_in_kernel,
        out_shape=outs,
        grid=(bsz, nt),
        in_specs=in_specs,
        out_specs=[rowspec(hl), rowspec(hl), tspec, rowspec(hl), rowspec(hl), tspec],
        scratch_shapes=[pltpu.VMEM((1, LANE), F32)],
        compiler_params=_cparams(("parallel", "arbitrary")),
        name="od_in",
    )(*args)


def _attn_kernel(blk, nh, q_ref, k_ref, vt_ref, out_ref, st_ref, acc_ref, buf0, buf1, bufm,
                 mx0, mx1, mxm):
    tp = k_ref.shape[1]
    n_blocks = (tp - LANE) // blk
    half = LANE // 2

    def start_qblock(q0, nq):
        st_ref[0:nh, 0:nq] = jnp.full((nh, nq), NEG, F32)
        st_ref[nh:2 * nh, 0:nq] = jnp.zeros((nh, nq), F32)
        acc_ref[:, :, 0:nq] = jnp.zeros((nh, LANE, nq), F32)

    def logits_to(buf, mx, q0, nq, k0, nk, diagonal):
        if diagonal:
            ok = (lax.broadcasted_iota(jnp.int32, (nk, nq), 0)
                  <= lax.broadcasted_iota(jnp.int32, (nk, nq), 1))
        for hd in range(nh):
            s = _dot_nt(k_ref[0, pl.ds(k0, nk), hd * LANE:(hd + 1) * LANE],
                        q_ref[0, pl.ds(q0, nq), hd * LANE:(hd + 1) * LANE])
            if diagonal:
                s = jnp.where(ok, s, NEG)
            buf[hd, 0:nk, 0:nq] = s
            mx[hd:hd + 1, 0:nq] = jnp.max(s, axis=0, keepdims=True)

    def consume(buf, mx, nq, k0, nk):
        for hd in range(nh):
            pair = hd // 2
            vt = vt_ref[0, pair * LANE:(pair + 1) * LANE, pl.ds(k0, nk)]
            m_old = st_ref[hd:hd + 1, 0:nq]
            l_old = st_ref[nh + hd:nh + hd + 1, 0:nq]
            m_new = jnp.maximum(m_old, mx[hd:hd + 1, 0:nq])
            alpha = jnp.exp2(m_old - m_new)
            p = jnp.exp2(buf[hd, 0:nk, 0:nq] - m_new)
            st_ref[hd:hd + 1, 0:nq] = m_new
            st_ref[nh + hd:nh + hd + 1, 0:nq] = alpha * l_old + jnp.sum(p, axis=0, keepdims=True)
            acc_ref[hd, :, 0:nq] = acc_ref[hd, :, 0:nq] * alpha + _dot(vt, p.astype(BF16))

    def finish_qblock(q0, nq):
        for pair in range(nh // 2):
            a, b = 2 * pair, 2 * pair + 1
            o_a = acc_ref[a, 0:half, 0:nq] / st_ref[nh + a:nh + a + 1, 0:nq]
            o_b = acc_ref[b, half:LANE, 0:nq] / st_ref[nh + b:nh + b + 1, 0:nq]
            out_ref[0, pl.ds(q0, nq), pair * LANE:(pair + 1) * LANE] = (
                jnp.concatenate([o_a, o_b], axis=0).T.astype(out_ref.dtype))

    start_qblock(0, LANE)
    logits_to(bufm, mxm, 0, LANE, 0, LANE, True)
    consume(bufm, mxm, LANE, 0, LANE)
    finish_qblock(0, LANE)

    n_static = min(3, n_blocks)
    for i in range(n_static):
        q0 = LANE + i * blk
        tiles = [(bufm, mxm, 0, LANE, False)]
        for n in range(i):
            tiles.append(((buf0, buf1)[n % 2], (mx0, mx1)[n % 2], LANE + n * blk, blk, False))
        tiles.append(((buf0, buf1)[i % 2], (mx0, mx1)[i % 2], q0, blk, True))
        start_qblock(q0, blk)
        logits_to(tiles[0][0], tiles[0][1], q0, blk, tiles[0][2], tiles[0][3], tiles[0][4])
        for n, (buf, mx, k0, nk, _) in enumerate(tiles):
            if n + 1 < len(tiles):
                nbuf, nmx, nk0, nnk, ndiag = tiles[n + 1]
                logits_to(nbuf, nmx, q0, blk, nk0, nnk, ndiag)
            consume(buf, mx, blk, k0, nk)
        finish_qblock(q0, blk)

    def qblock(i, carry):
        q0 = pl.multiple_of(LANE + i * blk, LANE)

        def k_of(n):
            return pl.multiple_of(LANE + n * blk, LANE)

        def pair(m, c):
            logits_to(buf1, mx1, q0, blk, k_of(2 * m + 1), blk, False)
            consume(buf0, mx0, blk, k_of(2 * m), blk)
            logits_to(buf0, mx0, q0, blk, k_of(2 * m + 2), blk, False)
            consume(buf1, mx1, blk, k_of(2 * m + 1), blk)
            return c

        start_qblock(q0, blk)
        logits_to(bufm, mxm, q0, blk, 0, LANE, False)
        logits_to(buf0, mx0, q0, blk, k_of(0), blk, False)
        consume(bufm, mxm, blk, 0, LANE)
        pair(0, 0)
        n_pairs = (i - 1) // 2
        lax.fori_loop(1, n_pairs, pair, 0)
        held = 2 * n_pairs

        @pl.when(i % 2 == 1)
        def _():
            logits_to(buf1, mx1, q0, blk, q0, blk, True)
            consume(buf0, mx0, blk, k_of(held), blk)
            consume(buf1, mx1, blk, q0, blk)
            finish_qblock(q0, blk)

        @pl.when(i % 2 == 0)
        def _():
            logits_to(buf1, mx1, q0, blk, k_of(held + 1), blk, False)
            consume(buf0, mx0, blk, k_of(held), blk)
            logits_to(buf0, mx0, q0, blk, q0, blk, True)
            consume(buf1, mx1, blk, k_of(held + 1), blk)
            consume(buf0, mx0, blk, q0, blk)
            finish_qblock(q0, blk)

        return carry

    lax.fori_loop(n_static, n_blocks, qblock, 0)


def _attention(q, k, vt, blk, nh, name):
    bsz, tp, _ = q.shape
    npair = vt.shape[1] // LANE
    g = nh // 2
    assert 2 * nh <= SUBLANE and npair % g == 0
    scratch = [pltpu.VMEM((SUBLANE, blk), F32), pltpu.VMEM((nh, LANE, blk), F32),
               pltpu.VMEM((nh, blk, blk), F32), pltpu.VMEM((nh, blk, blk), F32),
               pltpu.VMEM((nh, LANE, blk), F32),
               pltpu.VMEM((SUBLANE, blk), F32), pltpu.VMEM((SUBLANE, blk), F32),
               pltpu.VMEM((SUBLANE, blk), F32)]
    return pl.pallas_call(
        functools.partial(_attn_kernel, blk, nh),
        out_shape=jax.ShapeDtypeStruct((bsz, tp, npair * LANE), BF16),
        grid=(bsz, npair // g),
        in_specs=[pl.BlockSpec((1, tp, nh * LANE), lambda b, p: (b, 0, p)),
                  pl.BlockSpec((1, tp, nh * LANE), lambda b, p: (b, 0, p)),
                  pl.BlockSpec((1, LANE * g, tp), lambda b, p: (b, p, 0))],
        out_specs=pl.BlockSpec((1, tp, LANE * g), lambda b, p: (b, 0, p)),
        scratch_shapes=scratch,
        compiler_params=_cparams(("parallel", "parallel")),
        name=name,
    )(q, k, vt)


def _place_cols(w, width, offset):
    return jnp.pad(w, ((0, 0), (offset, width - offset - w.shape[1])))


def _ev_consts(w_in, b_if):
    nq, nv = A_HEADS * A_DQK, A_HEADS * A_DV
    o = 0
    wq = w_in[:, o:o + nq]; o += nq
    wk = w_in[:, o:o + nq]; o += nq
    wv = w_in[:, o:o + nv]; o += nv
    wo = w_in[:, o:o + nv]; o += nv
    wg = w_in[:, o:o + 2 * A_HEADS]; o += 2 * A_HEADS
    wxb = w_in[:, o:o + 1024]; o += 1024
    wgb = w_in[:, o:o + 1024]
    w = jnp.concatenate([wq, wk, wv, wo, _place_cols(wg, LANE, 0), wxb, wgb], axis=1).astype(BF16)
    bif = _place_cols(b_if[None, :], LANE, 0)
    return w, bif


def _head_tile_cols(w, n_heads, per_head, take, offset):
    k = w.shape[0]
    wh = w.reshape(k, n_heads, per_head)[:, :, take[0]:take[0] + take[1]]
    wh = jnp.pad(wh, ((0, 0), (0, 0), (offset, LANE - offset - take[1])))
    return wh.reshape(k, n_heads * LANE)


def _row128(parts):
    r = jnp.zeros((LANE,), F32)
    for off, v in parts:
        r = r.at[off:off + v.shape[0]].set(v.astype(F32))
    return r[None, :]


def _od_consts(p):
    c = {}
    w_in = p["w_in"]
    o = 0
    wcq = w_in[:, o:o + C_Q_LORA]; o += C_Q_LORA
    wckv = w_in[:, o:o + C_KV_LORA]; o += C_KV_LORA
    wkr = w_in[:, o:o + C_ROPE]; o += C_ROPE
    nf = D_HEADS * D_HD
    wfq = w_in[:, o:o + nf]; o += nf
    wfk = w_in[:, o:o + nf]; o += nf
    wfv = w_in[:, o:o + nf]; o += nf
    wff = w_in[:, o:o + D_HEADS]
    c["w"] = jnp.concatenate([wcq, wckv, _place_cols(wkr, LANE, ROPE_LO), wfq, wfk, wfv,
                              _place_cols(wff, LANE, 0)], axis=1).astype(BF16)
    c["ln"] = p["ln"][None, :]
    freqs = ROPE_THETA ** (-jnp.arange(HALF, dtype=F32) / HALF)
    n_grp = LANE // C_ROPE
    c["frq"] = jnp.concatenate([_row128([(C_ROPE * g, freqs), (C_ROPE * g + HALF, freqs)])
                                for g in range(n_grp)], axis=0)
    c["phs"] = _row128([(C_ROPE * g + HALF, jnp.full((HALF,), -0.5 * math.pi, F32)) for g in range(n_grp)])
    c["gqa"] = p["g_qa"][None, :]
    c["gkva"] = p["g_kva"][None, :]
    dq = C_NOPE + C_ROPE
    wuq, wukv = p["w_uq"], p["w_ukv"]
    c["wq"] = (_head_tile_cols(wuq, C_HEADS, dq, (0, dq), 0)).astype(BF16)
    c["wkn"] = _head_tile_cols(wukv, C_HEADS, C_NOPE + C_V, (0, C_NOPE), 0).astype(BF16)
    c["wv"] = wukv.reshape(-1, C_HEADS, C_NOPE + C_V)[:, :, C_NOPE:].reshape(-1, C_HEADS * C_V).astype(BF16)
    c["gq"] = _row128([(0, p["g_qn"]), (ROPE_LO, p["g_qr"])])
    c["gk"] = _row128([(0, p["g_kn"])])
    c["gkr"] = _row128([(ROPE_LO, p["g_kr"])])
    c["gfq"] = _row128([(0, p["g_fq"]), (D_HD, p["g_fq"])])
    c["gfk"] = _row128([(0, p["g_fk"]), (D_HD, p["g_fk"])])
    c["bf"] = _row128([(0, p["b_f"])])
    seg = np.where(np.arange(LANE) < C_NOPE, 0, np.where(np.arange(LANE) < ROPE_HI, 1, 2))
    inv_len = np.where(seg == 0, 1.0 / C_NOPE, 1.0 / C_ROPE)[None, :]
    c["gm"] = jnp.asarray((seg[:, None] == seg[None, :]) * inv_len, F32).astype(BF16)
    seg64 = np.arange(LANE) // D_HD
    c["g64"] = jnp.asarray((seg64[:, None] == seg64[None, :]) * (1.0 / D_HD), F32).astype(BF16)
    sel = np.zeros((3 * LANE, D_HEADS * LANE), np.float32)
    for h in range(D_HEADS):
        for part in range(3):
            sel[part * LANE + h, h * LANE + D_HD + part] = 1.0
    c["sel"] = jnp.asarray(sel).astype(BF16)
    for nm in ("gq", "gk", "gkr", "gfq", "gfk"):
        c[nm] = jnp.tile(c[nm], (1, 2))
    for nm in ("gm", "g64"):
        z = jnp.zeros_like(c[nm])
        c[nm] = jnp.block([[c[nm], z], [z, c[nm]]])
    return c


TIME_TILE = 3 * LANE
TOKEN_TILE = 4 * LANE
ATTN_HEADS_PER_STEP = 4


def _tile_or_lane(n, tile):
    return tile if n % tile == 0 else LANE


def kernel(x, positions, meta, ev_ln, ev_w_in, ev_b_if, ev_a_norm, ev_conv_w, ev_conv_b, ev_w_ra, ev_b_ra, ev_w_rx, ev_b_rx, ev_lam, ev_w_out, od_ln, od_w_in, od_b_f, od_g_qa, od_g_kva, od_w_uq, od_w_ukv, od_g_qn, od_g_qr, od_g_kn, od_g_kr, od_g_fq, od_g_fk, od_w_out, mlp_ln, w_ff1, w_ff2):
    bsz, seq, d = x.shape
    tp = LANE + seq
    r = bsz * tp
    tm = _tile_or_lane(seq, TOKEN_TILE)
    tt = _tile_or_lane(tp, TIME_TILE)

    x2 = x.reshape(bsz * seq, d)
    mb = jnp.concatenate([jnp.zeros((PAD, d), x.dtype), meta.astype(x.dtype)], axis=0)
    pos_full = jnp.concatenate([jnp.zeros((bsz, PAD), jnp.int32),
                                jnp.broadcast_to(jnp.arange(N_META, dtype=jnp.int32)[None, :], (bsz, N_META)),
                                positions + N_META], axis=1)
    pos_col = pos_full.astype(F32)[:, :, None]

    e = 0
    w_a, bif = _ev_consts(ev_w_in[e], ev_b_if[e])
    q, k, v, o, g, xc, gate = _ev_in(x2, mb, ev_ln[e][None, :], w_a, ev_conv_w[e], ev_conv_b[e][None, :],
                                     bsz, tp, tt)
    ha = _mlstm(q, k, v, o, g, bif, ev_a_norm[e][None, :], bsz, tp, tt)
    hb = _lru(xc, gate, (0.5 * ev_w_ra[e]).astype(BF16), 0.5 * ev_b_ra[e][None, :],
              (0.5 * ev_w_rx[e]).astype(BF16), 0.5 * ev_b_rx[e][None, :], ev_lam[e][None, :], bsz, tp, tt)
    nv = A_HEADS * A_DV
    wo = ev_w_out[e].astype(BF16)
    h2 = _out_mlp((x2, mb), [ha, hb], [wo[:nv], wo[nv:]], mlp_ln[0][None, :],
                  w_ff1[0].astype(BF16), w_ff2[0].astype(BF16), tt, ("input", bsz, tp, seq))

    oi = 0
    consts = _od_consts(dict(ln=od_ln[oi], w_in=od_w_in[oi], b_f=od_b_f[oi], g_qa=od_g_qa[oi],
                             g_kva=od_g_kva[oi], w_uq=od_w_uq[oi], w_ukv=od_w_ukv[oi],
                             g_qn=od_g_qn[oi], g_qr=od_g_qr[oi], g_kn=od_g_kn[oi], g_kr=od_g_kr[oi],
                             g_fq=od_g_fq[oi], g_fk=od_g_fk[oi]))
    qm, km, vmt, qf, kf, vft = _od_in(h2.reshape(bsz, tp, d), pos_col, consts, tt)
    hc = _attention(qm, km, vmt, tm, ATTN_HEADS_PER_STEP, "mla_attn")
    hd = _attention(qf, kf, vft, tm, ATTN_HEADS_PER_STEP, "fox_attn")
    nc = C_HEADS * C_V
    wo = od_w_out[oi].astype(BF16)
    out = _out_mlp(h2, [hc.reshape(r, nc), hd.reshape(r, nc)], [wo[:nc], wo[nc:]], mlp_ln[1][None, :],
                   w_ff1[1].astype(BF16), w_ff2[1].astype(BF16), tm, ("tokens", bsz, tp, seq))
    return out.reshape(bsz, seq, d)
```

```python
import functools
import math

import numpy as np
import jax
import jax.numpy as jnp
from jax import lax
from jax.experimental import pallas as pl
from jax.experimental.pallas import tpu as pltpu

F32 = jnp.float32
BF16 = jnp.bfloat16

N_META = 16
EPS = 1e-6
A_HEADS = 4
A_DQK = 128
A_DV = 256
B_BLOCKS = 8
B_BLOCK = 128
CONV_W = 4
LRU_C = 8.0
C_HEADS = 8
C_Q_LORA = 384
C_KV_LORA = 256
C_NOPE = 64
C_ROPE = 32
C_V = 64
ROPE_THETA = 10000.0
D_HEADS = 8
D_HD = 64

LANE = 128
SUBLANE = 8
VMEM_LIMIT = 56 * 1024 * 1024

PAD = LANE - N_META
NEG = -1e30
LOG2E = math.log2(math.e)

HALF = C_ROPE // 2
ROPE_LO = C_NOPE
ROPE_MID = C_NOPE + HALF
ROPE_HI = C_NOPE + C_ROPE
MASK_LANE = ROPE_HI


def _cparams(sem):
    return pltpu.CompilerParams(dimension_semantics=sem, vmem_limit_bytes=VMEM_LIMIT)


def _dot(a, b):
    return jnp.dot(a, b, preferred_element_type=F32)


def _dot_nt(a, b):
    return lax.dot_general(a, b, (((1,), (1,)), ((), ())), preferred_element_type=F32)


def _split2(x):
    hi = x.astype(BF16)
    lo = (x - hi.astype(F32)).astype(BF16)
    return hi, lo


def _split3(x):
    hi = x.astype(BF16)
    r = x - hi.astype(F32)
    mid = r.astype(BF16)
    lo = (r - mid.astype(F32)).astype(BF16)
    return hi, mid, lo


def _dot_sel_l(sel, x, parts):
    pieces = _split3(x) if parts == 3 else _split2(x)
    acc = _dot(sel, pieces[0])
    for p in pieces[1:]:
        acc = acc + _dot(sel, p)
    return acc


def _rms_rows(x, g):
    ms = jnp.mean(x * x, axis=-1, keepdims=True)
    return x * lax.rsqrt(ms + EPS) * g


def _log_sigmoid(x):
    return -(jnp.maximum(-x, 0.0) + jnp.log1p(jnp.exp(-jnp.abs(x))))


def _sigmoid(x):
    return 0.5 * jnp.tanh(0.5 * x) + 0.5


def _tril_bf16(n):
    r = lax.broadcasted_iota(jnp.int32, (n, n), 0)
    c = lax.broadcasted_iota(jnp.int32, (n, n), 1)
    return jnp.where(r >= c, 1.0, 0.0).astype(BF16)


EV_SLABS = (("q", 512, BF16), ("k", 512, BF16), ("v", 1024, BF16), ("o", 1024, BF16),
            ("g", 128, F32), ("xb", 1024, BF16), ("gb", 1024, BF16))


def _input_tile(x_ref, mb_ref):
    xt = x_ref[...]
    first = jnp.concatenate([mb_ref[...], xt[:xt.shape[0] - LANE]], axis=0)
    return jnp.where(pl.program_id(1) == 0, first, xt)


def _input_specs(seq, tt, d):
    x_spec = pl.BlockSpec(
        (pl.Element(tt), pl.Element(d)),
        lambda b, t: (pl.multiple_of(b * seq + jnp.maximum(t * tt - LANE, 0), LANE), 0))
    return [x_spec, pl.BlockSpec((LANE, d), lambda b, t: (0, 0))]


def _ev_in_kernel(x_ref, mb_ref, ln_ref, w_ref, cw_ref, cb_ref, *refs):
    out_refs, hist_ref = refs[:-1], refs[-1]
    tt = x_ref.shape[0]

    @pl.when(pl.program_id(1) == 0)
    def _():
        hist_ref[...] = jnp.zeros_like(hist_ref)

    xn = _rms_rows(_input_tile(x_ref, mb_ref), ln_ref[...]).astype(BF16)
    offsets = np.cumsum([0] + [width for _, width, _ in EV_SLABS])
    order = sorted(range(len(EV_SLABS)), key=lambda s: {"xb": 0, "q": 1, "gb": 2, "k": 3, "o": 4}.get(EV_SLABS[s][0], 5))
    for s in order:
        name, width, dt = EV_SLABS[s]
        ref = out_refs[s]
        z = _dot(xn, w_ref[:, int(offsets[s]):int(offsets[s]) + width])
        if name == "q":
            z = z * (A_DQK ** -0.5)
        elif name == "o":
            z = _sigmoid(z)
        elif name == "gb":
            z = _gelu_tanh(z)
        elif name == "xb":
            xfull = jnp.concatenate([hist_ref[...], z], axis=0)
            hist_ref[...] = z[tt - SUBLANE:, :]
            conv = cb_ref[...] + cw_ref[CONV_W - 1:CONV_W, :] * z
            for d in range(1, CONV_W):
                conv = conv + cw_ref[CONV_W - 1 - d:CONV_W - d, :] * pltpu.roll(xfull, d, 0)[SUBLANE:, :]
            z = conv
        ref[...] = z.astype(dt)


def _ev_in(x2, mb, ln, w, cw, cb, bsz, tp, tt):
    d = x2.shape[1]
    n = w.shape[1]
    nt = tp // tt
    const = lambda b, t: (0, 0)
    outs = [jax.ShapeDtypeStruct((bsz * tp, width), dt) for (_, width, dt) in EV_SLABS]
    return pl.pallas_call(
        _ev_in_kernel,
        out_shape=outs,
        grid=(bsz, nt),
        in_specs=_input_specs(x2.shape[0] // bsz, tt, d) + [
            pl.BlockSpec((1, d), const), pl.BlockSpec((d, n), const),
            pl.BlockSpec(cw.shape, const), pl.BlockSpec(cb.shape, const)],
        out_specs=[pl.BlockSpec((tt, width), lambda b, t: (b * nt + t, 0)) for (_, width, _) in EV_SLABS],
        scratch_shapes=[pltpu.VMEM((SUBLANE, cw.shape[1]), F32)],
        compiler_params=_cparams(("parallel", "arbitrary")),
        name="ev_in",
    )(x2, mb, ln, w, cw, cb)


def _mlstm_kernel(q_ref, k_ref, v_ref, o_ref, g_ref, bif_ref, an_ref, out_ref, cn_ref, m_ref):
    c = pl.program_id(1)
    L = q_ref.shape[0]

    @pl.when(c == 0)
    def _():
        cn_ref[...] = jnp.zeros_like(cn_ref)
        m_ref[...] = jnp.zeros_like(m_ref)

    lane = lax.broadcasted_iota(jnp.int32, (L, LANE), 1)
    row = lax.broadcasted_iota(jnp.int32, (L, LANE), 0)
    valid = (row >= PAD) | (c > 0)
    g = g_ref[...] + bif_ref[...]
    li = jnp.where(valid, g, NEG)
    lf = jnp.where(valid, _log_sigmoid(g), 0.0)
    is_f = (lane >= A_HEADS) & (lane < 2 * A_HEADS)
    lf = jnp.where(is_f, lf, 0.0)
    tril = _tril_bf16(L)
    bcum = _dot_sel_l(tril, lf, 2)
    li_t = li.T
    bcum_t = bcum.T

    rr = lax.broadcasted_iota(jnp.int32, (L, L), 0)
    cc = lax.broadcasted_iota(jnp.int32, (L, L), 1)
    causal = rr >= cc
    e0 = jnp.where(lane == 0, 1.0, 0.0).astype(BF16)

    heads = []
    for h in range(A_HEADS):
        qs = q_ref[:, h * A_DQK:(h + 1) * A_DQK]
        kk = k_ref[:, h * A_DQK:(h + 1) * A_DQK]
        vext = jnp.concatenate([v_ref[:, h * A_DV:(h + 1) * A_DV], e0], axis=1)
        i_col = li[:, h:h + 1]
        b_col = bcum[:, A_HEADS + h:A_HEADS + h + 1]
        i_row = li_t[h:h + 1, :]
        b_row = bcum_t[A_HEADS + h:A_HEADS + h + 1, :]
        m_state = m_ref[:, h:h + 1]
        cn = cn_ref[h]
        qk = _dot_nt(qs, kk)
        qc = _dot(qs, cn.astype(BF16))

        gsum = b_col[L - 1:L, :]
        dk_col = gsum - b_col + i_col
        dk_row = gsum - b_row + i_row
        m_new = jnp.maximum(gsum + m_state, jnp.max(dk_row, axis=1, keepdims=True))
        wk = jnp.exp(dk_col - m_new)
        sc = jnp.exp(gsum + m_state - m_new)
        kw_t = (kk.astype(F32) * wk).T.astype(BF16)
        cn_ref[h] = sc * cn + _dot(kw_t, vext)
        m_ref[:, h:h + 1] = m_new
        heads.append((qk, qc, vext, b_col, b_row, i_row, m_state))

    weighted = []
    for qk, qc, vext, b_col, b_row, i_row, m_state in heads:
        dmat = jnp.where(causal, b_col - b_row + i_row, NEG)
        inter = b_col + m_state
        m_row = jnp.maximum(inter, jnp.max(dmat, axis=1, keepdims=True))
        w_intra = jnp.exp(dmat - m_row)
        w_inter = jnp.exp(inter - m_row)
        s = (qk * w_intra).astype(BF16)
        weighted.append((w_inter * qc + _dot(s, vext), m_row))

    for h, (numden, m_row) in enumerate(weighted):
        num = numden[:, :A_DV]
        den = numden[:, A_DV:A_DV + 1]
        hh = num / jnp.maximum(jnp.abs(den), jnp.exp(-m_row))
        hn = _rms_rows(hh, an_ref[...])
        og = o_ref[:, h * A_DV:(h + 1) * A_DV].astype(F32)
        out_ref[:, h * A_DV:(h + 1) * A_DV] = (hn * og).astype(out_ref.dtype)


def _mlstm(q, k, v, o, g, bif, an, bsz, tp, L):
    nq, nv = A_HEADS * A_DQK, A_HEADS * A_DV
    nc = tp // L
    row_map = lambda b, c: (b * nc + c, 0)
    const = lambda b, c: (0, 0)
    return pl.pallas_call(
        _mlstm_kernel,
        out_shape=jax.ShapeDtypeStruct((bsz * tp, nv), BF16),
        grid=(bsz, nc),
        in_specs=[pl.BlockSpec((L, nq), row_map), pl.BlockSpec((L, nq), row_map),
                  pl.BlockSpec((L, nv), row_map), pl.BlockSpec((L, nv), row_map),
                  pl.BlockSpec((L, LANE), row_map),
                  pl.BlockSpec((1, LANE), const), pl.BlockSpec((1, A_DV), const)],
        out_specs=pl.BlockSpec((L, nv), row_map),
        scratch_shapes=[pltpu.VMEM((A_HEADS, A_DQK, A_DV + LANE), F32),
                        pltpu.VMEM((1, LANE), F32)],
        compiler_params=_cparams(("parallel", "arbitrary")),
        name="mlstm",
    )(q, k, v, o, g, bif, an)


def _gelu_tanh(x):
    c0 = math.sqrt(2.0 / math.pi)
    return x * (0.5 * jnp.tanh(x * (c0 + (c0 * 0.044715) * (x * x))) + 0.5)


def _lru_kernel(xc_ref, gate_ref, wra_ref, bra_ref, wrx_ref, brx_ref, lam_ref,
                out_ref, hc_ref, a_s, u_s):
    t = pl.program_id(1)
    tt, w = xc_ref.shape

    @pl.when(t == 0)
    def _():
        hc_ref[...] = jnp.zeros_like(hc_ref)

    row = lax.broadcasted_iota(jnp.int32, (tt, w), 0)
    valid = (row >= PAD) | (t > 0)
    xcb = xc_ref[...]
    conv = xcb.astype(F32)
    r_parts, i_parts = [], []
    for blk in range(B_BLOCKS):
        xs = xcb[:, blk * B_BLOCK:(blk + 1) * B_BLOCK]
        r_parts.append(_dot(xs, wra_ref[blk]))
        i_parts.append(_dot(xs, wrx_ref[blk]))
    r = 0.5 * jnp.tanh(jnp.concatenate(r_parts, axis=1) + bra_ref[...]) + 0.5
    ig = 0.5 * jnp.tanh(jnp.concatenate(i_parts, axis=1) + brx_ref[...]) + 0.5
    nlam = -lam_ref[...]
    softplus = jnp.maximum(nlam, 0.0) + jnp.log1p(jnp.exp(-jnp.abs(nlam)))
    a = jnp.exp2(((-LRU_C * LOG2E) * softplus) * r)
    y = 1.0 - a * a
    u = jnp.where(y > 0.0, y * lax.rsqrt(y), 0.0) * (ig * conv)
    a_s[...] = a
    u_s[...] = jnp.where(valid, u, 0.0)

    srow = lax.broadcasted_iota(jnp.int32, (SUBLANE, w), 0)

    def group(gi, hprev):
        sl = pl.ds(pl.multiple_of(gi * SUBLANE, SUBLANE), SUBLANE)
        ag = a_s[sl, :]
        ug = u_s[sl, :]
        for d in (1, 2, 4):
            ok = srow >= d
            a_sh = jnp.where(ok, pltpu.roll(ag, d, 0), 1.0)
            u_sh = jnp.where(ok, pltpu.roll(ug, d, 0), 0.0)
            ug = ag * u_sh + ug
            ag = ag * a_sh
        hg = ug + ag * hprev
        u_s[sl, :] = hg
        return jnp.broadcast_to(hg[SUBLANE - 1:SUBLANE, :], (SUBLANE, w))

    hc_ref[...] = lax.fori_loop(0, tt // SUBLANE, group, hc_ref[...], unroll=4)
    out_ref[...] = (u_s[...] * gate_ref[...].astype(F32)).astype(out_ref.dtype)


def _lru(xc, gate, wra, bra, wrx, brx, lam, bsz, tp, tt):
    w = xc.shape[1]
    nt = tp // tt
    row_map = lambda b, t: (b * nt + t, 0)
    const2 = lambda b, t: (0, 0)
    const3 = lambda b, t: (0, 0, 0)
    return pl.pallas_call(
        _lru_kernel,
        out_shape=jax.ShapeDtypeStruct(xc.shape, BF16),
        grid=(bsz, nt),
        in_specs=[pl.BlockSpec((tt, w), row_map), pl.BlockSpec((tt, w), row_map),
                  pl.BlockSpec((B_BLOCKS, B_BLOCK, B_BLOCK), const3), pl.BlockSpec((1, w), const2),
                  pl.BlockSpec((B_BLOCKS, B_BLOCK, B_BLOCK), const3), pl.BlockSpec((1, w), const2),
                  pl.BlockSpec((1, w), const2)],
        out_specs=pl.BlockSpec((tt, w), row_map),
        scratch_shapes=[pltpu.VMEM((SUBLANE, w), F32),
                        pltpu.VMEM((tt, w), F32), pltpu.VMEM((tt, w), F32)],
        compiler_params=_cparams(("parallel", "arbitrary")),
        name="rglru",
    )(xc, gate, wra, bra, wrx, brx, lam)


def _out_mlp_kernel(n_mix, from_input, *refs):
    n_h = 2 if from_input else 1
    h_refs, refs = refs[:n_h], refs[n_h:]
    mix_refs = refs[:n_mix]
    ln_ref = refs[n_mix]
    n_w = n_mix + 2
    w_hbm = refs[n_mix + 1:n_mix + 1 + n_w]
    out_ref = refs[n_mix + 1 + n_w]
    w_vmem = refs[n_mix + 2 + n_w:n_mix + 2 + 2 * n_w]
    sem = refs[n_mix + 2 + 2 * n_w]

    @pl.when((pl.program_id(0) == 0) & (pl.program_id(1) == 0))
    def _():
        copies = [pltpu.make_async_copy(src, dst, sem.at[i])
                  for i, (src, dst) in enumerate(zip(w_hbm, w_vmem))]
        for cp in copies:
            cp.start()
        for cp in copies:
            cp.wait()

    wo_refs, w1_ref, w2_ref = w_vmem[:n_mix], w_vmem[n_mix], w_vmem[n_mix + 1]
    res = _input_tile(*h_refs) if from_input else h_refs[0][...]
    for m_ref, w_ref in zip(mix_refs, wo_refs):
        res = res + _dot(m_ref[...], w_ref[...])
    hn = _rms_rows(res, ln_ref[...]).astype(BF16)
    a = jnp.maximum(_dot(hn, w1_ref[...]), 0.0)
    out_ref[...] = res + _dot((a * a).astype(BF16), w2_ref[...])


def _out_mlp(h_in, mixes, wouts, ln, w1, w2, tm, rows):
    d = w2.shape[1]
    n_mix = len(mixes)
    kind = rows[0]
    if kind == "all":
        r = rows[1]
        grid, out_rows = (r // tm, 1), r
        mix_spec = lambda n: pl.BlockSpec((tm, n), lambda i, j: (i, 0))
        h_specs, h_args = [mix_spec(d)], [h_in]
        out_spec = pl.BlockSpec((tm, d), lambda i, j: (i, 0))
    elif kind == "tokens":
        _, bsz, tp, seq = rows
        per_seq = seq // tm
        grid, out_rows = (bsz, per_seq), bsz * seq
        mix_spec = lambda n: pl.BlockSpec(
            (pl.Element(tm), pl.Element(n)),
            lambda b, i: (pl.multiple_of(b * tp + (tp - seq) + i * tm, LANE), 0))
        h_specs, h_args = [mix_spec(d)], [h_in]
        out_spec = pl.BlockSpec((tm, d), lambda b, i: (b * per_seq + i, 0))
    else:
        _, bsz, tp, seq = rows
        nt = tp // tm
        grid, out_rows = (bsz, nt), bsz * tp
        mix_spec = lambda n: pl.BlockSpec((tm, n), lambda b, t: (b * nt + t, 0))
        h_specs, h_args = _input_specs(seq, tm, d), list(h_in)
        out_spec = mix_spec(d)
    weights = [*wouts, w1, w2]
    in_specs = h_specs + [mix_spec(m.shape[1]) for m in mixes]
    in_specs += [pl.BlockSpec((1, d), lambda i, j: (0, 0))]
    in_specs += [pl.BlockSpec(memory_space=pl.ANY) for _ in weights]
    scratch = [pltpu.VMEM(w.shape, w.dtype) for w in weights]
    scratch.append(pltpu.SemaphoreType.DMA((len(weights),)))
    return pl.pallas_call(
        functools.partial(_out_mlp_kernel, n_mix, kind == "input"),
        out_shape=jax.ShapeDtypeStruct((out_rows, d), F32),
        grid=grid,
        in_specs=in_specs,
        out_specs=out_spec,
        scratch_shapes=scratch,
        compiler_params=_cparams(("arbitrary", "arbitrary")),
        name="out_mlp",
    )(*h_args, *mixes, ln, *weights)


OD_COLS = dict(cq=(0, 384), ckv=(384, 640), kr=(640, 768), fq=(768, 1280), fk=(1280, 1792),
               fv=(1792, 2304), ff=(2304, 2432))


def _od_in_kernel(h_ref, ln_ref, w_ref, pos_ref, frq_ref, phs_ref, gqa_ref, gkva_ref,
                  wq_ref, wkn_ref, wv_ref, gq_ref, gk_ref, gkr_ref, gfq_ref, gfk_ref, bf_ref,
                  gm_ref, g64_ref, sel_ref,
                  qm_ref, km_ref, vmt_ref, qf_ref, kf_ref, vft_ref, fc_ref):
    t = pl.program_id(1)
    tm = h_ref.shape[1]

    @pl.when(t == 0)
    def _():
        fc_ref[...] = jnp.zeros_like(fc_ref)

    xn = _rms_rows(h_ref[0], ln_ref[...]).astype(BF16)

    def proj(name):
        a, b = OD_COLS[name]
        return _dot(xn, w_ref[:, a:b])

    z_cq, z_ckv, z_kr = proj("cq"), proj("ckv"), proj("kr")

    wide = 2 * LANE
    lane2 = lax.broadcasted_iota(jnp.int32, (tm, wide), 1) % LANE
    row2 = lax.broadcasted_iota(jnp.int32, (tm, wide), 0)
    unused_row2 = (row2 < PAD) & (t == 0)
    mask_lane = lane2 == MASK_LANE
    n_grp = LANE // C_ROPE
    tq = tm // n_grp
    pos = pos_ref[0]
    angc = phs_ref[...]
    for g in range(n_grp):
        angc = angc + pos[g * tq:(g + 1) * tq, :] * frq_ref[g:g + 1, :]
    cc = jnp.cos(angc)
    lane1 = lax.broadcasted_iota(jnp.int32, (tq, LANE), 1)
    in_lo = (lane1 >= ROPE_LO) & (lane1 < ROPE_MID)
    in_hi = (lane1 >= ROPE_MID) & (lane1 < ROPE_HI)
    cs_parts, sn_parts = [], []
    for g in range(n_grp):
        shift = (ROPE_LO - C_ROPE * g) % LANE
        c = pltpu.roll(cc, shift, 1) if shift else cc
        cs_parts.append(jnp.where(in_hi, pltpu.roll(c, HALF, 1), jnp.where(in_lo, c, 1.0)))
        sn_parts.append(jnp.where(in_lo, -pltpu.roll(c, LANE - HALF, 1), jnp.where(in_hi, c, 0.0)))
    cs = jnp.concatenate(cs_parts, axis=0)
    sn = jnp.concatenate(sn_parts, axis=0)
    cs = jnp.concatenate([cs, cs], axis=1)
    sn = jnp.concatenate([sn, sn], axis=1)
    first_half = lane2 < ROPE_MID

    def seg_rms(x, sel, gain):
        ms = _dot((x * x).astype(BF16), sel)
        return x * lax.rsqrt(ms + EPS) * gain

    def rope(y):
        sw = jnp.where(first_half, pltpu.roll(y, wide - HALF, 1), pltpu.roll(y, HALF, 1))
        return y * cs + sw * sn

    gm = gm_ref[...]
    g64 = g64_ref[...]

    cqn = _rms_rows(z_cq, gqa_ref[...]).astype(BF16)
    ckvn = _rms_rows(z_ckv, gkva_ref[...]).astype(BF16)
    qall = _dot(cqn, wq_ref[...])
    knall = _dot(ckvn, wkn_ref[...])
    scale_m = (C_NOPE + C_ROPE) ** -0.5 * LOG2E
    for h2 in range(C_HEADS // 2):
        cols = slice(h2 * wide, (h2 + 1) * wide)
        y = rope(seg_rms(qall[:, cols], gm,gq_ref[...]))
        qm_ref[0, :, cols] = jnp.where(mask_lane, 1.0, y * scale_m).astype(BF16)
    krt = rope(seg_rms(jnp.concatenate([z_kr, z_kr], axis=1), gm,gkr_ref[...]))
    krt = jnp.where(mask_lane & unused_row2, NEG, krt)
    for h2 in range(C_HEADS // 2):
        cols = slice(h2 * wide, (h2 + 1) * wide)
        y = seg_rms(knall[:, cols], gm,gk_ref[...]) + krt
        km_ref[0, :, cols] = y.astype(BF16)

    lane = lax.broadcasted_iota(jnp.int32, (tm, LANE), 1)
    unused_row = (lax.broadcasted_iota(jnp.int32, (tm, LANE), 0) < PAD) & (t == 0)
    valid = jnp.logical_not(unused_row) & (lane < D_HEADS)
    lf = jnp.where(valid, _log_sigmoid(proj("ff") + bf_ref[...]), 0.0)
    fc = _dot_sel_l(_tril_bf16(tm), lf, 2) + fc_ref[...]
    fc_ref[...] = fc[tm - 1:tm, :]
    hi, mid, lo = _split3(jnp.where(unused_row, NEG, fc * (-LOG2E)))
    bias = _dot(jnp.concatenate([hi, mid, lo], axis=1), sel_ref[...])
    feat = lane < D_HD
    ones3 = jnp.where((lane >= D_HD) & (lane < D_HD + 3), 1.0, 0.0)

    fq = proj("fq")
    fk = proj("fk")
    scale_f = D_HD ** -0.5 * LOG2E
    for p2 in range(D_HEADS // 4):
        cols = slice(p2 * wide, (p2 + 1) * wide)
        yq = seg_rms(fq[:, cols], g64,gfq_ref[...]) * scale_f
        yk = seg_rms(fk[:, cols], g64,gfk_ref[...])
        for j in range(4):
            h = 4 * p2 + j
            tq = yq[:, (j // 2) * LANE:(j // 2 + 1) * LANE]
            tk = yk[:, (j // 2) * LANE:(j // 2 + 1) * LANE]
            if j % 2:
                tq, tk = pltpu.roll(tq, D_HD, 1), pltpu.roll(tk, D_HD, 1)
            hs = slice(h * LANE, (h + 1) * LANE)
            qf_ref[0, :, hs] = jnp.where(feat, tq, ones3).astype(BF16)
            kf_ref[0, :, hs] = jnp.where(feat, tk, bias[:, hs]).astype(BF16)

    vft_ref[0] = proj("fv").T.astype(BF16)
    vmt_ref[0] = _dot(ckvn, wv_ref[...]).T.astype(BF16)


def _od_in(h3, pos_col, consts, tm):
    bsz, tp, d = h3.shape
    nt = tp // tm
    names = ["ln", "w", None, "frq", "phs", "gqa", "gkva", "wq", "wkn", "wv", "gq", "gk", "gkr",
             "gfq", "gfk", "bf", "gm", "g64", "sel"]
    args, in_specs = [h3], [pl.BlockSpec((1, tm, d), lambda b, t: (b, t, 0))]
    for nm in names:
        if nm is None:
            args.append(pos_col)
            in_specs.append(pl.BlockSpec((1, tm, 1), lambda b, t: (b, t, 0)))
            continue
        a = consts[nm]
        args.append(a)
        in_specs.append(pl.BlockSpec(a.shape, (lambda nd: (lambda b, t: (0,) * nd))(a.ndim)))
    hl = C_HEADS * LANE
    pk = D_HEADS * D_HD
    outs = [jax.ShapeDtypeStruct((bsz, tp, hl), BF16), jax.ShapeDtypeStruct((bsz, tp, hl), BF16),
            jax.ShapeDtypeStruct((bsz, pk, tp), BF16), jax.ShapeDtypeStruct((bsz, tp, hl), BF16),
            jax.ShapeDtypeStruct((bsz, tp, 2 * pk), BF16), jax.ShapeDtypeStruct((bsz, pk, tp), BF16)]
    rowspec = lambda n: pl.BlockSpec((1, tm, n), lambda b, t: (b, t, 0))
    tspec = pl.BlockSpec((1, pk, tm), lambda b, t: (b, 0, t))
    return pl.pallas_call(
        _od_in_kernel,
        out_shape=outs,
        grid=(bsz, nt),
        in_specs=in_specs,
        out_specs=[rowspec(hl), rowspec(hl), tspec, rowspec(hl), rowspec(hl), tspec],
        scratch_shapes=[pltpu.VMEM((1, LANE), F32)],
        compiler_params=_cparams(("parallel", "arbitrary")),
        name="od_in",
    )(*args)


def _attn_kernel(blk, nh, q_ref, k_ref, vt_ref, out_ref, st_ref, acc_ref, buf0, buf1, bufm,
                 mx0, mx1, mxm):
    tp = k_ref.shape[1]
    n_blocks = (tp - LANE) // blk
    half = LANE // 2

    def start_qblock(q0, nq):
        st_ref[0:nh, 0:nq] = jnp.full((nh, nq), NEG, F32)
        st_ref[nh:2 * nh, 0:nq] = jnp.zeros((nh, nq), F32)
        acc_ref[:, :, 0:nq] = jnp.zeros((nh, LANE, nq), F32)

    def logits_to(buf, mx, q0, nq, k0, nk, diagonal):
        if diagonal:
            ok = (lax.broadcasted_iota(jnp.int32, (nk, nq), 0)
                  <= lax.broadcasted_iota(jnp.int32, (nk, nq), 1))
        for hd in range(nh):
            s = _dot_nt(k_ref[0, pl.ds(k0, nk), hd * LANE:(hd + 1) * LANE],
                        q_ref[0, pl.ds(q0, nq), hd * LANE:(hd + 1) * LANE])
            if diagonal:
                s = jnp.where(ok, s, NEG)
            buf[hd, 0:nk, 0:nq] = s
            mx[hd:hd + 1, 0:nq] = jnp.max(s, axis=0, keepdims=True)

    def consume(buf, mx, nq, k0, nk):
        for hd in range(nh):
            pair = hd // 2
            vt = vt_ref[0, pair * LANE:(pair + 1) * LANE, pl.ds(k0, nk)]
            m_old = st_ref[hd:hd + 1, 0:nq]
            l_old = st_ref[nh + hd:nh + hd + 1, 0:nq]
            m_new = jnp.maximum(m_old, mx[hd:hd + 1, 0:nq])
            alpha = jnp.exp2(m_old - m_new)
            p = jnp.exp2(buf[hd, 0:nk, 0:nq] - m_new)
            st_ref[hd:hd + 1, 0:nq] = m_new
            st_ref[nh + hd:nh + hd + 1, 0:nq] = alpha * l_old + jnp.sum(p, axis=0, keepdims=True)
            acc_ref[hd, :, 0:nq] = acc_ref[hd, :, 0:nq] * alpha + _dot(vt, p.astype(BF16))

    def finish_qblock(q0, nq):
        for pair in range(nh // 2):
            a, b = 2 * pair, 2 * pair + 1
            o_a = acc_ref[a, 0:half, 0:nq] / st_ref[nh + a:nh + a + 1, 0:nq]
            o_b = acc_ref[b, half:LANE, 0:nq] / st_ref[nh + b:nh + b + 1, 0:nq]
            out_ref[0, pl.ds(q0, nq), pair * LANE:(pair + 1) * LANE] = (
                jnp.concatenate([o_a, o_b], axis=0).T.astype(out_ref.dtype))

    start_qblock(0, LANE)
    logits_to(bufm, mxm, 0, LANE, 0, LANE, True)
    consume(bufm, mxm, LANE, 0, LANE)
    finish_qblock(0, LANE)

    n_static = min(3, n_blocks)
    for i in range(n_static):
        q0 = LANE + i * blk
        tiles = [(bufm, mxm, 0, LANE, False)]
        for n in range(i):
            tiles.append(((buf0, buf1)[n % 2], (mx0, mx1)[n % 2], LANE + n * blk, blk, False))
        tiles.append(((buf0, buf1)[i % 2], (mx0, mx1)[i % 2], q0, blk, True))
        start_qblock(q0, blk)
        logits_to(tiles[0][0], tiles[0][1], q0, blk, tiles[0][2], tiles[0][3], tiles[0][4])
        for n, (buf, mx, k0, nk, _) in enumerate(tiles):
            if n + 1 < len(tiles):
                nbuf, nmx, nk0, nnk, ndiag = tiles[n + 1]
                logits_to(nbuf, nmx, q0, blk, nk0, nnk, ndiag)
            consume(buf, mx, blk, k0, nk)
        finish_qblock(q0, blk)

    def qblock(i, carry):
        q0 = pl.multiple_of(LANE + i * blk, LANE)

        def k_of(n):
            return pl.multiple_of(LANE + n * blk, LANE)

        def pair(m, c):
            logits_to(buf1, mx1, q0, blk, k_of(2 * m + 1), blk, False)
            consume(buf0, mx0, blk, k_of(2 * m), blk)
            logits_to(buf0, mx0, q0, blk, k_of(2 * m + 2), blk, False)
            consume(buf1, mx1, blk, k_of(2 * m + 1), blk)
            return c

        start_qblock(q0, blk)
        logits_to(bufm, mxm, q0, blk, 0, LANE, False)
        logits_to(buf0, mx0, q0, blk, k_of(0), blk, False)
        consume(bufm, mxm, blk, 0, LANE)
        pair(0, 0)
        n_pairs = (i - 1) // 2
        lax.fori_loop(1, n_pairs, pair, 0)
        held = 2 * n_pairs

        @pl.when(i % 2 == 1)
        def _():
            logits_to(buf1, mx1, q0, blk, q0, blk, True)
            consume(buf0, mx0, blk, k_of(held), blk)
            consume(buf1, mx1, blk, q0, blk)
            finish_qblock(q0, blk)

        @pl.when(i % 2 == 0)
        def _():
            logits_to(buf1, mx1, q0, blk, k_of(held + 1), blk, False)
            consume(buf0, mx0, blk, k_of(held), blk)
            logits_to(buf0, mx0, q0, blk, q0, blk, True)
            consume(buf1, mx1, blk, k_of(held + 1), blk)
            consume(buf0, mx0, blk, q0, blk)
            finish_qblock(q0, blk)

        return carry

    lax.fori_loop(n_static, n_blocks, qblock, 0)


def _attention(q, k, vt, blk, nh, name):
    bsz, tp, _ = q.shape
    npair = vt.shape[1] // LANE
    g = nh // 2
    assert 2 * nh <= SUBLANE and npair % g == 0
    scratch = [pltpu.VMEM((SUBLANE, blk), F32), pltpu.VMEM((nh, LANE, blk), F32),
               pltpu.VMEM((nh, blk, blk), F32), pltpu.VMEM((nh, blk, blk), F32),
               pltpu.VMEM((nh, LANE, blk), F32),
               pltpu.VMEM((SUBLANE, blk), F32), pltpu.VMEM((SUBLANE, blk), F32),
               pltpu.VMEM((SUBLANE, blk), F32)]
    return pl.pallas_call(
        functools.partial(_attn_kernel, blk, nh),
        out_shape=jax.ShapeDtypeStruct((bsz, tp, npair * LANE), BF16),
        grid=(bsz, npair // g),
        in_specs=[pl.BlockSpec((1, tp, nh * LANE), lambda b, p: (b, 0, p)),
                  pl.BlockSpec((1, tp, nh * LANE), lambda b, p: (b, 0, p)),
                  pl.BlockSpec((1, LANE * g, tp), lambda b, p: (b, p, 0))],
        out_specs=pl.BlockSpec((1, tp, LANE * g), lambda b, p: (b, 0, p)),
        scratch_shapes=scratch,
        compiler_params=_cparams(("parallel", "parallel")),
        name=name,
    )(q, k, vt)


def _place_cols(w, width, offset):
    return jnp.pad(w, ((0, 0), (offset, width - offset - w.shape[1])))


def _ev_consts(w_in, b_if):
    nq, nv = A_HEADS * A_DQK, A_HEADS * A_DV
    o = 0
    wq = w_in[:, o:o + nq]; o += nq
    wk = w_in[:, o:o + nq]; o += nq
    wv = w_in[:, o:o + nv]; o += nv
    wo = w_in[:, o:o + nv]; o += nv
    wg = w_in[:, o:o + 2 * A_HEADS]; o += 2 * A_HEADS
    wxb = w_in[:, o:o + 1024]; o += 1024
    wgb = w_in[:, o:o + 1024]
    w = jnp.concatenate([wq, wk, wv, wo, _place_cols(wg, LANE, 0), wxb, wgb], axis=1).astype(BF16)
    bif = _place_cols(b_if[None, :], LANE, 0)
    return w, bif


def _head_tile_cols(w, n_heads, per_head, take, offset):
    k = w.shape[0]
    wh = w.reshape(k, n_heads, per_head)[:, :, take[0]:take[0] + take[1]]
    wh = jnp.pad(wh, ((0, 0), (0, 0), (offset, LANE - offset - take[1])))
    return wh.reshape(k, n_heads * LANE)


def _row128(parts):
    r = jnp.zeros((LANE,), F32)
    for off, v in parts:
        r = r.at[off:off + v.shape[0]].set(v.astype(F32))
    return r[None, :]


def _od_consts(p):
    c = {}
    w_in = p["w_in"]
    o = 0
    wcq = w_in[:, o:o + C_Q_LORA]; o += C_Q_LORA
    wckv = w_in[:, o:o + C_KV_LORA]; o += C_KV_LORA
    wkr = w_in[:, o:o + C_ROPE]; o += C_ROPE
    nf = D_HEADS * D_HD
    wfq = w_in[:, o:o + nf]; o += nf
    wfk = w_in[:, o:o + nf]; o += nf
    wfv = w_in[:, o:o + nf]; o += nf
    wff = w_in[:, o:o + D_HEADS]
    c["w"] = jnp.concatenate([wcq, wckv, _place_cols(wkr, LANE, ROPE_LO), wfq, wfk, wfv,
                              _place_cols(wff, LANE, 0)], axis=1).astype(BF16)
    c["ln"] = p["ln"][None, :]
    freqs = ROPE_THETA ** (-jnp.arange(HALF, dtype=F32) / HALF)
    n_grp = LANE // C_ROPE
    c["frq"] = jnp.concatenate([_row128([(C_ROPE * g, freqs), (C_ROPE * g + HALF, freqs)])
                                for g in range(n_grp)], axis=0)
    c["phs"] = _row128([(C_ROPE * g + HALF, jnp.full((HALF,), -0.5 * math.pi, F32)) for g in range(n_grp)])
    c["gqa"] = p["g_qa"][None, :]
    c["gkva"] = p["g_kva"][None, :]
    dq = C_NOPE + C_ROPE
    wuq, wukv = p["w_uq"], p["w_ukv"]
    c["wq"] = (_head_tile_cols(wuq, C_HEADS, dq, (0, dq), 0)).astype(BF16)
    c["wkn"] = _head_tile_cols(wukv, C_HEADS, C_NOPE + C_V, (0, C_NOPE), 0).astype(BF16)
    c["wv"] = wukv.reshape(-1, C_HEADS, C_NOPE + C_V)[:, :, C_NOPE:].reshape(-1, C_HEADS * C_V).astype(BF16)
    c["gq"] = _row128([(0, p["g_qn"]), (ROPE_LO, p["g_qr"])])
    c["gk"] = _row128([(0, p["g_kn"])])
    c["gkr"] = _row128([(ROPE_LO, p["g_kr"])])
    c["gfq"] = _row128([(0, p["g_fq"]), (D_HD, p["g_fq"])])
    c["gfk"] = _row128([(0, p["g_fk"]), (D_HD, p["g_fk"])])
    c["bf"] = _row128([(0, p["b_f"])])
    seg = np.where(np.arange(LANE) < C_NOPE, 0, np.where(np.arange(LANE) < ROPE_HI, 1, 2))
    inv_len = np.where(seg == 0, 1.0 / C_NOPE, 1.0 / C_ROPE)[None, :]
    c["gm"] = jnp.asarray((seg[:, None] == seg[None, :]) * inv_len, F32).astype(BF16)
    seg64 = np.arange(LANE) // D_HD
    c["g64"] = jnp.asarray((seg64[:, None] == seg64[None, :]) * (1.0 / D_HD), F32).astype(BF16)
    sel = np.zeros((3 * LANE, D_HEADS * LANE), np.float32)
    for h in range(D_HEADS):
        for part in range(3):
            sel[part * LANE + h, h * LANE + D_HD + part] = 1.0
    c["sel"] = jnp.asarray(sel).astype(BF16)
    for nm in ("gq", "gk", "gkr", "gfq", "gfk"):
        c[nm] = jnp.tile(c[nm], (1, 2))
    for nm in ("gm", "g64"):
        z = jnp.zeros_like(c[nm])
        c[nm] = jnp.block([[c[nm], z], [z, c[nm]]])
    return c


TIME_TILE = 3 * LANE
SCAN_TILE = 11 * LANE
TOKEN_TILE = 4 * LANE
ATTN_HEADS_PER_STEP = 4


def _tile_or_lane(n, tile):
    return tile if n % tile == 0 else LANE


def kernel(x, positions, meta, ev_ln, ev_w_in, ev_b_if, ev_a_norm, ev_conv_w, ev_conv_b, ev_w_ra, ev_b_ra, ev_w_rx, ev_b_rx, ev_lam, ev_w_out, od_ln, od_w_in, od_b_f, od_g_qa, od_g_kva, od_w_uq, od_w_ukv, od_g_qn, od_g_qr, od_g_kn, od_g_kr, od_g_fq, od_g_fk, od_w_out, mlp_ln, w_ff1, w_ff2):
    bsz, seq, d = x.shape
    tp = LANE + seq
    r = bsz * tp
    tm = _tile_or_lane(seq, TOKEN_TILE)
    tt = _tile_or_lane(tp, TIME_TILE)

    x2 = x.reshape(bsz * seq, d)
    mb = jnp.concatenate([jnp.zeros((PAD, d), x.dtype), meta.astype(x.dtype)], axis=0)
    pos_full = jnp.concatenate([jnp.zeros((bsz, PAD), jnp.int32),
                                jnp.broadcast_to(jnp.arange(N_META, dtype=jnp.int32)[None, :], (bsz, N_META)),
                                positions + N_META], axis=1)
    pos_col = pos_full.astype(F32)[:, :, None]

    e = 0
    w_a, bif = _ev_consts(ev_w_in[e], ev_b_if[e])
    q, k, v, o, g, xc, gate = _ev_in(x2, mb, ev_ln[e][None, :], w_a, ev_conv_w[e], ev_conv_b[e][None, :],
                                     bsz, tp, tt)
    ha = _mlstm(q, k, v, o, g, bif, ev_a_norm[e][None, :], bsz, tp, tt)
    hb = _lru(xc, gate, (0.5 * ev_w_ra[e]).astype(BF16), 0.5 * ev_b_ra[e][None, :],
              (0.5 * ev_w_rx[e]).astype(BF16), 0.5 * ev_b_rx[e][None, :], ev_lam[e][None, :], bsz, tp,
              _tile_or_lane(tp, SCAN_TILE))
    nv = A_HEADS * A_DV
    wo = ev_w_out[e].astype(BF16)
    h2 = _out_mlp((x2, mb), [ha, hb], [wo[:nv], wo[nv:]], mlp_ln[0][None, :],
                  w_ff1[0].astype(BF16), w_ff2[0].astype(BF16), tt, ("input", bsz, tp, seq))

    oi = 0
    consts = _od_consts(dict(ln=od_ln[oi], w_in=od_w_in[oi], b_f=od_b_f[oi], g_qa=od_g_qa[oi],
                             g_kva=od_g_kva[oi], w_uq=od_w_uq[oi], w_ukv=od_w_ukv[oi],
                             g_qn=od_g_qn[oi], g_qr=od_g_qr[oi], g_kn=od_g_kn[oi], g_kr=od_g_kr[oi],
                             g_fq=od_g_fq[oi], g_fk=od_g_fk[oi]))
    qm, km, vmt, qf, kf, vft = _od_in(h2.reshape(bsz, tp, d), pos_col, consts, tt)
    hc = _attention(qm, km, vmt, tm, ATTN_HEADS_PER_STEP, "mla_attn")
    hd = _attention(qf, kf, vft, tm, ATTN_HEADS_PER_STEP, "fox_attn")
    nc = C_HEADS * C_V
    wo = od_w_out[oi].astype(BF16)
    out = _out_mlp(h2, [hc.reshape(r, nc), hd.reshape(r, nc)], [wo[:nc], wo[nc:]], mlp_ln[1][None, :],
                   w_ff1[1].astype(BF16), w_ff2[1].astype(BF16), tm, ("tokens", bsz, tp, seq))
    return out.reshape(bsz, seq, d)
```
